```python
import math
import jax
import jax.numpy as jnp
from jax import lax
import numpy as np

D_MODEL = 2048
BATCH = 4
SEQ = 2048
DEPTH = 4
DEC_BATCH = 32
DEC_SEQ = 4
PAST_LEN = 16384
PAGE_SIZE = 128

HEAD_DIM = 64
N_HEADS = D_MODEL // HEAD_DIM
N_KV_HEADS = 8
ATTN_SCALE = HEAD_DIM ** -0.5
ROPE_THETA = 10000.0
QKV_COLS = (N_HEADS + 2 * N_KV_HEADS) * HEAD_DIM
N_MIXERS = 3
N_SWA_LAYERS = (DEPTH + 2) // 3
N_DIL_LAYERS = (DEPTH + 1) // 3
N_MOBA_LAYERS = DEPTH // 3
SWA_WINDOW = 128
DIL_PAIRS = ((128, 1), (512, 4), (2048, 16))
MOBA_BLOCK = 256
MOBA_TOPK = 3
Q_BLOCK = 128
N_EXPERTS = 16
N_EXPERT_GROUPS = 4
EXPERTS_PER_GROUP = N_EXPERTS // N_EXPERT_GROUPS
MOE_TOPK = 2
D_EXPERT = D_MODEL // 4
MOE_BLOCK = 128
DEEPNORM_ALPHA = (2 * DEPTH) ** 0.25
DEEPNORM_BETA = (8 * DEPTH) ** -0.25
LN_EPS = 1e-5
F32 = jnp.float32

kernel_name = 'hybrid_swa_dilated_moba_grouped_moe_step'


def layer_norm(x, gain, bias):
    xf = x.astype(F32)
    mu = jnp.mean(xf, axis=-1, keepdims=True)
    var = jnp.mean(jnp.square(xf - mu), axis=-1, keepdims=True)
    y = (xf - mu) * lax.rsqrt(var + LN_EPS) * gain.astype(F32) + bias.astype(F32)
    return y.astype(x.dtype)


def rope(x, pos):
    half = HEAD_DIM // 2
    inv_freq = jnp.exp(jnp.arange(half, dtype=F32) * (-2.0 * math.log(ROPE_THETA) / HEAD_DIM))
    ang = pos.astype(F32)[:, None] * inv_freq[None, :]
    cos = jnp.cos(ang)[None, :, None, :]
    sin = jnp.sin(ang)[None, :, None, :]
    xf = x.astype(F32)
    x1, x2 = xf[..., :half], xf[..., half:]
    return jnp.concatenate([x1 * cos - x2 * sin, x2 * cos + x1 * sin], axis=-1).astype(x.dtype)


def split_qkv_rope(z, pos):
    n, t = z.shape[:2]
    nq, nk = N_HEADS * HEAD_DIM, N_KV_HEADS * HEAD_DIM
    q = z[..., :nq].reshape(n, t, N_HEADS, HEAD_DIM)
    k = z[..., nq:nq + nk].reshape(n, t, N_KV_HEADS, HEAD_DIM)
    v = z[..., nq + nk:].reshape(n, t, N_KV_HEADS, HEAD_DIM)
    return rope(q, pos), rope(k, pos), v


def out_proj(o, w_o, dtype):
    n, t = o.shape[:2]
    return jnp.einsum('btc,cd->btd', o.reshape(n, t, -1).astype(dtype), w_o)


def masked_softmax(s, mask, sink):
    s = jnp.where(mask, s, -jnp.inf)
    if sink is not None:
        n_kv, g = s.shape[-4], s.shape[-3]
        col = jnp.broadcast_to(sink.astype(F32).reshape(n_kv, g, 1, 1), s.shape[:-1] + (1,))
        s = jnp.concatenate([s, col], axis=-1)
    lse = jax.nn.logsumexp(s, axis=-1)
    p = jnp.exp(s - lse[..., None])
    if sink is not None:
        p = p[..., :-1]
    return p, lse


def local_attn_blocks(q, k, v, window, sink):
    n, length = q.shape[:2]
    blk = window
    lp = -(-length // blk) * blk
    nb = lp // blk
    pad = ((0, 0), (0, lp - length), (0, 0), (0, 0))
    q, k, v = jnp.pad(q, pad), jnp.pad(k, pad), jnp.pad(v, pad)
    n_kv = k.shape[2]
    g = q.shape[2] // n_kv
    qb = q.reshape(n, nb, blk, n_kv, g, HEAD_DIM)

    def prev_and_cur(x):
        xb = x.reshape(n, nb, blk, n_kv, HEAD_DIM)
        prev = jnp.concatenate([jnp.zeros_like(xb[:, :1]), xb[:, :-1]], axis=1)
        return jnp.concatenate([prev, xb], axis=2)

    kb, vb = prev_and_cur(k), prev_and_cur(v)
    s = jnp.einsum('nbqkgd,nbckd->nbkgqc', qb, kb, preferred_element_type=F32) * ATTN_SCALE
    qi = jnp.arange(blk)[:, None]
    ci = jnp.arange(2 * blk)[None, :]
    delta = qi + blk - ci
    in_band = (delta >= 0) & (delta <= window)
    exists = (jnp.arange(nb)[:, None, None] > 0) | (ci >= blk)[None]
    mask = (in_band[None] & exists)[None, :, None, None]
    p, lse = masked_softmax(s, mask, sink)
    o = jnp.einsum('nbkgqc,nbckd->nbqkgd', p, vb, preferred_element_type=F32)
    o = o.reshape(n, lp, n_kv * g, HEAD_DIM)[:, :length]
    lse = lse.transpose(0, 1, 4, 2, 3).reshape(n, lp, n_kv * g)[:, :length]
    return o, lse


def local_attn_gather(q, k_buf, v_buf, k_new, v_new, window, dilation, sink):
    n, t = q.shape[:2]
    n_kv = k_new.shape[2]
    g = q.shape[2] // n_kv
    lb = k_buf.shape[1]
    kc = jnp.concatenate([k_buf, k_new], axis=1)
    vc = jnp.concatenate([v_buf, v_new], axis=1)
    rows = lb + jnp.arange(t)[:, None] - dilation * jnp.arange(window // dilation + 1)[None, :]
    valid = rows >= 0
    rows = jnp.maximum(rows, 0)
    kg, vg = kc[:, rows], vc[:, rows]
    qg = q.reshape(n, t, n_kv, g, HEAD_DIM)
    s = jnp.einsum('ntkgd,ntckd->nkgtc', qg, kg, preferred_element_type=F32) * ATTN_SCALE
    p, lse = masked_softmax(s, valid, sink)
    o = jnp.einsum('nkgtc,ntckd->ntkgd', p, vg, preferred_element_type=F32)
    return o.reshape(n, t, n_kv * g, HEAD_DIM), lse.transpose(0, 3, 1, 2).reshape(n, t, n_kv * g)


def fold_residues(x, d):
    n, length = x.shape[:2]
    rest = x.shape[2:]
    x = x.reshape((n, length // d, d) + rest)
    return jnp.moveaxis(x, 2, 1).reshape((n * d, length // d) + rest)


def unfold_residues(x, d, n):
    length = x.shape[1]
    rest = x.shape[2:]
    x = x.reshape((n, d, length) + rest)
    return jnp.moveaxis(x, 1, 2).reshape((n, length * d) + rest)


def block_means(k):
    n, length = k.shape[:2]
    return k.reshape(n, length // MOBA_BLOCK, MOBA_BLOCK, k.shape[2], HEAD_DIM).astype(F32).mean(axis=2)


def moba_attend(q, q_pos, k, v, k_mean):
    n, t = q.shape[:2]
    length, n_kv = k.shape[1], k.shape[2]
    g = q.shape[2] // n_kv
    n_blk = length // MOBA_BLOCK
    qg = q.reshape(n, t, n_kv, g, HEAD_DIM)
    blk_ids = jnp.arange(n_blk)
    own = q_pos // MOBA_BLOCK
    gate = jnp.einsum('ntkgd,nbkd->nkgtb', qg, k_mean, preferred_element_type=F32)
    gate = jnp.where(blk_ids[None, :] < own[:, None], gate, -jnp.inf)
    top_val, top_idx = lax.top_k(gate, min(MOBA_TOPK, n_blk))
    chosen = jnp.any((top_idx[..., None] == blk_ids) & jnp.isfinite(top_val)[..., None], axis=-2)
    k_pos = jnp.arange(length)
    own_causal = ((k_pos[None, :] // MOBA_BLOCK) == own[:, None]) & (k_pos[None, :] <= q_pos[:, None])
    keep = jnp.repeat(chosen, MOBA_BLOCK, axis=-1) | own_causal
    s = jnp.einsum('ntkgd,nlkd->nkgtl', qg, k, preferred_element_type=F32) * ATTN_SCALE
    p, _ = masked_softmax(s, keep, None)
    o = jnp.einsum('nkgtl,nlkd->ntkgd', p, v, preferred_element_type=F32)
    return o.reshape(n, t, n_kv * g, HEAD_DIM)


def swa_mixer(xp, xs, cache, w_qkv, w_o, sinks, past_len):
    s_len, t = xp.shape[1], xs.shape[1]
    qp, kp, vp = split_qkv_rope(jnp.einsum('btd,dc->btc', xp, w_qkv), jnp.arange(s_len))
    op, _ = local_attn_blocks(qp, kp, vp, SWA_WINDOW, sinks)
    qs, ks, vs = split_qkv_rope(jnp.einsum('btd,dc->btc', xs, w_qkv), past_len + jnp.arange(t))
    cache = cache.astype(ks.dtype)
    os_, _ = local_attn_gather(qs, cache[:, :, 0], cache[:, :, 1], ks, vs, SWA_WINDOW, 1, sinks)
    state_p = jnp.stack([kp, vp], axis=2)[:, -min(SWA_WINDOW, s_len):]
    state_s = jnp.concatenate([cache, jnp.stack([ks, vs], axis=2)], axis=1)[:, -cache.shape[1]:]
    return out_proj(op, w_o, xp.dtype), out_proj(os_, w_o, xs.dtype), state_p, state_s


def dilated_mixer(xp, xs, caches, w_qkv, w_o, past_len):
    n_p, s_len = xp.shape[:2]
    n_s, t = xs.shape[:2]
    n_grp = len(DIL_PAIRS)
    zp = jnp.einsum('btd,dc->btc', xp, w_qkv).reshape(n_p, s_len, n_grp, QKV_COLS)
    zs = jnp.einsum('btd,dc->btc', xs, w_qkv).reshape(n_s, t, n_grp, QKV_COLS)
    pos_p, pos_s = jnp.arange(s_len), past_len + jnp.arange(t)
    outs_p, lses_p, outs_s, lses_s, states_p, states_s = [], [], [], [], [], []
    for g, (win, dil) in enumerate(DIL_PAIRS):
        q, k, v = split_qkv_rope(zp[:, :, g], pos_p)
        o, lse = local_attn_blocks(fold_residues(q, dil), fold_residues(k, dil), fold_residues(v, dil), win // dil, None)
        outs_p.append(unfold_residues(o, dil, n_p))
        lses_p.append(unfold_residues(lse, dil, n_p))
        states_p.append(jnp.stack([k, v], axis=2)[:, -min(win, s_len):])
        qs, ks, vs = split_qkv_rope(zs[:, :, g], pos_s)
        cache = caches[g].astype(ks.dtype)
        o, lse = local_attn_gather(qs, cache[:, :, 0], cache[:, :, 1], ks, vs, win, dil, None)
        outs_s.append(o)
        lses_s.append(lse)
        states_s.append(jnp.concatenate([cache, jnp.stack([ks, vs], axis=2)], axis=1)[:, -cache.shape[1]:])

    def merge(outs, lses):
        wgt = jax.nn.softmax(jnp.stack(lses, axis=0), axis=0)
        return jnp.sum(wgt[..., None] * jnp.stack(outs, axis=0), axis=0)

    return (out_proj(merge(outs_p, lses_p), w_o, xp.dtype), out_proj(merge(outs_s, lses_s), w_o, xs.dtype),
            states_p, states_s)


def moba_mixer(xp, xs, cache_moba, layer, page_table, w_qkv, w_o):
    n_p, s_len = xp.shape[:2]
    n_s, t = xs.shape[:2]
    past_len = page_table.shape[1] * cache_moba.shape[2]
    qp, kp, vp = split_qkv_rope(jnp.einsum('btd,dc->btc', xp, w_qkv), jnp.arange(s_len))
    lp = -(-s_len // MOBA_BLOCK) * MOBA_BLOCK
    pad = ((0, 0), (0, lp - s_len), (0, 0), (0, 0))
    kpp, vpp = jnp.pad(kp, pad), jnp.pad(vp, pad)
    kmean_p = block_means(kpp)
    nq = s_len // Q_BLOCK
    q_blocks = jnp.swapaxes(qp.reshape(n_p, nq, Q_BLOCK, N_HEADS, HEAD_DIM), 0, 1)
    pos_blocks = jnp.arange(s_len).reshape(nq, Q_BLOCK)
    op = lax.map(lambda a: moba_attend(a[0], a[1], kpp, vpp, kmean_p), (q_blocks, pos_blocks))
    op = jnp.swapaxes(op, 0, 1).reshape(n_p, s_len, N_HEADS, HEAD_DIM)
    pos_s = past_len + jnp.arange(t)
    qs, ks, vs = split_qkv_rope(jnp.einsum('btd,dc->btc', xs, w_qkv), pos_s)
    past = cache_moba[layer, page_table].reshape(n_s, past_len, 2, N_KV_HEADS, HEAD_DIM).astype(ks.dtype)
    total = past_len + t
    lt = -(-total // MOBA_BLOCK) * MOBA_BLOCK
    zpad = jnp.zeros((n_s, lt - total, N_KV_HEADS, HEAD_DIM), ks.dtype)
    k_all = jnp.concatenate([past[:, :, 0], ks, zpad], axis=1)
    v_all = jnp.concatenate([past[:, :, 1], vs, zpad], axis=1)
    os_ = moba_attend(qs, pos_s, k_all, v_all, block_means(k_all))
    return (out_proj(op, w_o, xp.dtype), out_proj(os_, w_o, xs.dtype),
            jnp.stack([kp, vp], axis=2), jnp.stack([ks, vs], axis=2))


def moe_ffn(h, router_w, router_bias, w_gate, w_up, w_down):
    n, d = h.shape
    n_exp = w_gate.shape[0]
    scores = jax.nn.sigmoid(jnp.einsum('nd,de->ne', h, router_w, preferred_element_type=F32))
    grouped = (scores + router_bias.astype(F32)).reshape(n, N_EXPERT_GROUPS, EXPERTS_PER_GROUP)
    group_score = lax.top_k(grouped, MOE_TOPK)[0].sum(axis=-1)
    g_sel = jnp.argmax(group_score, axis=-1)
    in_group = jnp.take_along_axis(grouped, g_sel[:, None, None], axis=1)[:, 0]
    _, local = lax.top_k(in_group, MOE_TOPK)
    expert_idx = g_sel[:, None] * EXPERTS_PER_GROUP + local
    gate = jnp.take_along_axis(scores, expert_idx, axis=1)
    gate = gate / jnp.sum(gate, axis=-1, keepdims=True)
    n_asg = n * MOE_TOPK
    e_flat = expert_idx.reshape(n_asg)
    order = jnp.argsort(e_flat)
    e_sorted = e_flat[order]
    tok = order // MOE_TOPK
    g_sorted = gate.reshape(n_asg)[order]
    counts = jnp.zeros((n_exp,), jnp.int32).at[e_flat].add(1)
    starts = jnp.cumsum(counts) - counts
    padded = (counts + MOE_BLOCK - 1) // MOE_BLOCK * MOE_BLOCK
    pend = jnp.cumsum(padded)
    pstart = pend - padded
    dest = pstart[e_sorted] + jnp.arange(n_asg) - starts[e_sorted]
    n_chunks = -(-n_asg // MOE_BLOCK) + n_exp
    xbuf = jnp.zeros((n_chunks * MOE_BLOCK, d), h.dtype).at[dest].set(h[tok])
    chunk_expert = jnp.minimum(jnp.searchsorted(pend, jnp.arange(n_chunks) * MOE_BLOCK, side='right'), n_exp - 1)

    def expert_rows(args):
        xc, e = args
        return (jax.nn.silu(xc @ w_gate[e]) * (xc @ w_up[e])) @ w_down[e]

    ybuf = lax.map(expert_rows, (xbuf.reshape(n_chunks, MOE_BLOCK, d), chunk_expert)).reshape(-1, d)
    out = jnp.zeros((n, d), F32).at[tok].add(g_sorted[:, None] * ybuf[dest].astype(F32))
    return out.astype(h.dtype)


def setup_inputs(seed: int = 0) -> dict:
    key = jax.random.key(seed)
    ks = jax.random.split(key, 24)
    n_pages = PAST_LEN // PAGE_SIZE
    n_used = DEC_BATCH * n_pages
    n_pool = n_used + max(1, n_used // 4)
    kv_row = (2, N_KV_HEADS, HEAD_DIM)
    attn_w = N_HEADS * HEAD_DIM

    def nrm(k, shape, scale=1.0):
        return jax.random.normal(k, shape, jnp.float32) * scale

    return {
        'x_prompt': nrm(ks[0], (BATCH, SEQ, D_MODEL)),
        'x_sample': nrm(ks[1], (DEC_BATCH, DEC_SEQ, D_MODEL)),
        'cache_swa': nrm(ks[2], (N_SWA_LAYERS, DEC_BATCH, min(SWA_WINDOW, PAST_LEN)) + kv_row),
        'cache_dil0': nrm(ks[3], (N_DIL_LAYERS, DEC_BATCH, min(DIL_PAIRS[0][0], PAST_LEN)) + kv_row),
        'cache_dil1': nrm(ks[4], (N_DIL_LAYERS, DEC_BATCH, min(DIL_PAIRS[1][0], PAST_LEN)) + kv_row),
        'cache_dil2': nrm(ks[5], (N_DIL_LAYERS, DEC_BATCH, min(DIL_PAIRS[2][0], PAST_LEN)) + kv_row),
        'cache_moba': nrm(ks[6], (N_MOBA_LAYERS, n_pool, PAGE_SIZE) + kv_row),
        'page_table': jax.random.permutation(ks[7], n_pool)[:n_used].reshape(DEC_BATCH, n_pages).astype(jnp.int32),
        'w_qkv_swa': nrm(ks[8], (N_SWA_LAYERS, D_MODEL, QKV_COLS), D_MODEL ** -0.5),
        'w_o_swa': nrm(ks[9], (N_SWA_LAYERS, attn_w, D_MODEL), DEEPNORM_BETA * attn_w ** -0.5),
        'sinks_swa': nrm(ks[10], (N_SWA_LAYERS, N_HEADS), 0.5),
        'w_qkv_dil': nrm(ks[11], (N_DIL_LAYERS, D_MODEL, len(DIL_PAIRS) * QKV_COLS), D_MODEL ** -0.5),
        'w_o_dil': nrm(ks[12], (N_DIL_LAYERS, attn_w, D_MODEL), DEEPNORM_BETA * attn_w ** -0.5),
        'w_qkv_moba': nrm(ks[13], (N_MOBA_LAYERS, D_MODEL, QKV_COLS), D_MODEL ** -0.5),
        'w_o_moba': nrm(ks[14], (N_MOBA_LAYERS, attn_w, D_MODEL), DEEPNORM_BETA * attn_w ** -0.5),
        'ln_gain': 1.0 + nrm(ks[15], (DEPTH, 2, D_MODEL), 0.02),
        'ln_bias': nrm(ks[16], (DEPTH, 2, D_MODEL), 0.02),
        'router_w': nrm(ks[17], (D_MODEL, N_EXPERTS), D_MODEL ** -0.5),
        'router_bias': nrm(ks[18], (N_EXPERTS,), 0.01),
        'w_gate_e': nrm(ks[19], (DEPTH, N_EXPERTS, D_MODEL, D_EXPERT), D_MODEL ** -0.5),
        'w_up_e': nrm(ks[20], (DEPTH, N_EXPERTS, D_MODEL, D_EXPERT), D_MODEL ** -0.5),
        'w_down_e': nrm(ks[21], (DEPTH, N_EXPERTS, D_EXPERT, D_MODEL), DEEPNORM_BETA * D_EXPERT ** -0.5),
    }


def reference(x_prompt, x_sample, cache_swa, cache_dil0, cache_dil1, cache_dil2, cache_moba, page_table,
              w_qkv_swa, w_o_swa, sinks_swa, w_qkv_dil, w_o_dil, w_qkv_moba, w_o_moba,
              ln_gain, ln_bias, router_w, router_bias, w_gate_e, w_up_e, w_down_e):
    past_len = page_table.shape[1] * cache_moba.shape[2]
    n_p, s_len, d = x_prompt.shape
    n_s, t = x_sample.shape[:2]
    xp, xs = x_prompt, x_sample
    swa_p, swa_s, moba_p, moba_s = [], [], [], []
    dil_p, dil_s = ([], [], []), ([], [], [])
    for i in range(DEPTH):
        kind, j = i % N_MIXERS, i // N_MIXERS
        if kind == 0:
            mp, ms, sp, ss = swa_mixer(xp, xs, cache_swa[j], w_qkv_swa[j], w_o_swa[j], sinks_swa[j], past_len)
            swa_p.append(sp)
            swa_s.append(ss)
        elif kind == 1:
            mp, ms, sps, sss = dilated_mixer(xp, xs, (cache_dil0[j], cache_dil1[j], cache_dil2[j]),
                                             w_qkv_dil[j], w_o_dil[j], past_len)
            for g in range(len(DIL_PAIRS)):
                dil_p[g].append(sps[g])
                dil_s[g].append(sss[g])
        else:
            mp, ms, sp, ss = moba_mixer(xp, xs, cache_moba, j, page_table, w_qkv_moba[j], w_o_moba[j])
            moba_p.append(sp)
            moba_s.append(ss)
        xp = layer_norm(DEEPNORM_ALPHA * xp + mp, ln_gain[i, 0], ln_bias[i, 0])
        xs = layer_norm(DEEPNORM_ALPHA * xs + ms, ln_gain[i, 0], ln_bias[i, 0])
        h = jnp.concatenate([xp.reshape(n_p * s_len, d), xs.reshape(n_s * t, d)], axis=0)
        f = moe_ffn(h, router_w, router_bias, w_gate_e[i], w_up_e[i], w_down_e[i])
        xp = layer_norm(DEEPNORM_ALPHA * xp + f[:n_p * s_len].reshape(n_p, s_len, d), ln_gain[i, 1], ln_bias[i, 1])
        xs = layer_norm(DEEPNORM_ALPHA * xs + f[n_p * s_len:].reshape(n_s, t, d), ln_gain[i, 1], ln_bias[i, 1])
    y_prompt, y_sample = xp, xs
    swa_prompt, swa_sample = jnp.stack(swa_p), jnp.stack(swa_s)
    dil0_prompt, dil0_sample = jnp.stack(dil_p[0]), jnp.stack(dil_s[0])
    dil1_prompt, dil1_sample = jnp.stack(dil_p[1]), jnp.stack(dil_s[1])
    dil2_prompt, dil2_sample = jnp.stack(dil_p[2]), jnp.stack(dil_s[2])
    moba_prompt, moba_sample = jnp.stack(moba_p), jnp.stack(moba_s)
    return (y_prompt, y_sample, swa_prompt, swa_sample, dil0_prompt, dil0_sample, dil1_prompt, dil1_sample,
            dil2_prompt, dil2_sample, moba_prompt, moba_sample)
```

```python
import functools
import math

import jax
import jax.numpy as jnp
from jax import lax
from jax.experimental import pallas as pl
from jax.experimental.pallas import tpu as pltpu

F32, BF16, I32 = jnp.float32, jnp.bfloat16, jnp.int32

D_MODEL = 2048
HEAD_DIM = 64
N_HEADS = 32
N_KV_HEADS = 8
Q_COLS = N_HEADS * HEAD_DIM
KV_COLS = N_KV_HEADS * HEAD_DIM
QKV_COLS = Q_COLS + 2 * KV_COLS
ATTN_SCALE = HEAD_DIM ** -0.5
ROPE_THETA = 10000.0
LOCAL_WINDOW = 128
DIL_PAIRS = ((128, 1), (512, 4), (2048, 16))
MOBA_BLOCK = 256
MOBA_TOPK = 3
PAGE_SIZE = 128
N_EXPERTS = 16
N_EXPERT_GROUPS = 4
EXPERTS_PER_GROUP = 4
D_EXPERT = D_MODEL // 4
DEPTH = 4
DEEPNORM_ALPHA = (2 * DEPTH) ** 0.25
LN_EPS = 1e-5
NEG = -1e30

LANES = 128
Q_TILE = 128
QKV_COL_TILE = 512
MOE_CHUNK = 256
PAGES_PER_STEP = 8
VMEM_LIMIT = 56 * 1024 * 1024


def _params(n_axes, vmem=VMEM_LIMIT):
    return pltpu.CompilerParams(dimension_semantics=("arbitrary",) * n_axes, vmem_limit_bytes=vmem)


def _nt_dot(a, b):
    return lax.dot_general(a, b, (((1,), (1,)), ((), ())), preferred_element_type=F32)


def _row_tile(m, pref, align):
    best = m
    for t in range(align, min(pref, m) + 1, align):
        if m % t == 0:
            best = t
    return best


def _rope_tables(pos):
    half = HEAD_DIM // 2
    inv_freq = jnp.exp(jnp.arange(half, dtype=F32) * (-2.0 * math.log(ROPE_THETA) / HEAD_DIM))
    ang = pos.astype(F32)[:, None] * inv_freq[None, :]
    cos, sin = jnp.cos(ang), jnp.sin(ang)
    return jnp.tile(cos, (1, 4)), jnp.concatenate([-sin, sin, -sin, sin], axis=1)


def _qkv_rope_kernel(x_ref, w_ref, cos_ref, sin_ref, q_ref, kv_ref, kvb_ref):
    j = pl.program_id(2)
    z = jnp.dot(x_ref[...].astype(BF16), w_ref[...], preferred_element_type=F32)
    tm = z.shape[0]

    def rope(z):
        cos, sin = cos_ref[...], sin_ref[...]
        lane = lax.broadcasted_iota(I32, (tm, LANES), 1)
        first_half = (lane & (HEAD_DIM - 1)) < HEAD_DIM // 2
        parts = []
        for c in range(z.shape[1] // LANES):
            zc = z[:, c * LANES:(c + 1) * LANES]
            other = jnp.where(first_half, pltpu.roll(zc, LANES - HEAD_DIM // 2, 1), pltpu.roll(zc, HEAD_DIM // 2, 1))
            parts.append(zc * cos + other * sin)
        return jnp.concatenate(parts, axis=1)

    n_q = Q_COLS // QKV_COL_TILE

    @pl.when(j < n_q)
    def _():
        q_ref[...] = (rope(z) * ATTN_SCALE).astype(BF16)

    @pl.when(j == n_q)
    def _():
        k = rope(z)
        kv_ref[...] = k
        kvb_ref[...] = k.astype(BF16)

    @pl.when(j == n_q + 1)
    def _():
        kv_ref[...] = z
        kvb_ref[...] = z.astype(BF16)


def _qkv_rope(x, w_bf16, cos, sin, n_groups):
    m = x.shape[0]
    tm = _row_tile(m, 640, 16)
    n_q = Q_COLS // QKV_COL_TILE
    n_j = QKV_COLS // QKV_COL_TILE
    return pl.pallas_call(
        _qkv_rope_kernel,
        grid=(n_groups, m // tm, n_j),
        in_specs=[
            pl.BlockSpec((tm, D_MODEL), lambda g, i, j: (i, 0)),
            pl.BlockSpec((D_MODEL, QKV_COL_TILE), lambda g, i, j: (0, g * n_j + j)),
            pl.BlockSpec((tm, LANES), lambda g, i, j: (i, 0)),
            pl.BlockSpec((tm, LANES), lambda g, i, j: (i, 0)),
        ],
        out_specs=[
            pl.BlockSpec((None, tm, QKV_COL_TILE), lambda g, i, j: (g, i, jnp.minimum(j, n_q - 1))),
            pl.BlockSpec((None, tm, QKV_COL_TILE), lambda g, i, j: (g, i, jnp.maximum(j - n_q, 0))),
            pl.BlockSpec((None, tm, QKV_COL_TILE), lambda g, i, j: (g, i, jnp.maximum(j - n_q, 0))),
        ],
        out_shape=[
            jax.ShapeDtypeStruct((n_groups, m, Q_COLS), BF16),
            jax.ShapeDtypeStruct((n_groups, m, 2 * KV_COLS), F32),
            jax.ShapeDtypeStruct((n_groups, m, 2 * KV_COLS), BF16),
        ],
        compiler_params=_params(3),
        name="qkv_rope",
    )(x, w_bf16, cos, sin)


def _stack_q_heads(q_ref, kv):
    qa = q_ref[:, (2 * kv) * LANES:(2 * kv + 1) * LANES]
    qb = q_ref[:, (2 * kv + 1) * LANES:(2 * kv + 2) * LANES]
    return jnp.concatenate([qa[:, :HEAD_DIM], qa[:, HEAD_DIM:], qb[:, :HEAD_DIM], qb[:, HEAD_DIM:]], axis=0)


def _kv_head(chunk, kv):
    off = (kv % 2) * HEAD_DIM
    return chunk[:, off:off + HEAD_DIM]


def _store_heads(ref, kv, val):
    rows = val.shape[0] // 4
    lane = lax.broadcasted_iota(I32, (rows, LANES), 1)
    for pair in range(2):
        a = val[(2 * pair) * rows:(2 * pair + 1) * rows]
        b = val[(2 * pair + 1) * rows:(2 * pair + 2) * rows]
        if val.shape[1] == 1:
            chunk = jnp.where(lane < HEAD_DIM, a, b)
        else:
            chunk = jnp.concatenate([a, b], axis=1)
        c = 2 * kv + pair
        ref[:, c * LANES:(c + 1) * LANES] = chunk.astype(ref.dtype)


def _local_attn_kernel(*refs, has_sink, want_lse, nb, steps):
    q_ref, kc_ref, kp_ref = refs[:3]
    pos = 3
    sink_ref = None
    if has_sink:
        sink_ref = refs[pos]
        pos += 1
    o_ref = refs[pos]
    lse_ref = refs[pos + 1] if want_lse else None
    t = pl.program_id(0)

    @pl.when(t >= steps)
    def _():
        o_ref[...] = jnp.zeros(o_ref.shape, o_ref.dtype)
        if want_lse:
            lse_ref[...] = jnp.zeros(lse_ref.shape, lse_ref.dtype)

    @pl.when(t < steps)
    def _():
        _local_attn_step(q_ref, kc_ref, kp_ref, sink_ref, o_ref, lse_ref, t % nb)


def _local_attn_step(q_ref, kc_ref, kp_ref, sink_ref, o_ref, lse_ref, i):
    has_sink, want_lse = sink_ref is not None, lse_ref is not None
    w = LOCAL_WINDOW
    rows = 4 * Q_TILE
    qi = lax.broadcasted_iota(I32, (rows, 2 * w), 0) & (Q_TILE - 1)
    ci = lax.broadcasted_iota(I32, (rows, 2 * w), 1)
    delta = qi + w - ci
    mask = (delta >= 0) & (delta <= w) & ((ci >= w) | (i > 0))
    for kv in range(N_KV_HEADS):
        q4 = _stack_q_heads(q_ref, kv)
        c = kv // 2
        kcur = _kv_head(kc_ref[:, c * LANES:(c + 1) * LANES], kv)
        kprev = _kv_head(kp_ref[:, c * LANES:(c + 1) * LANES], kv)
        vcur = _kv_head(kc_ref[:, KV_COLS + c * LANES:KV_COLS + (c + 1) * LANES], kv)
        vprev = _kv_head(kp_ref[:, KV_COLS + c * LANES:KV_COLS + (c + 1) * LANES], kv)
        k = jnp.concatenate([kprev, kcur], axis=0)
        v = jnp.concatenate([vprev, vcur], axis=0)
        s = jnp.where(mask, _nt_dot(q4, k), NEG)
        m = jnp.max(s, axis=1, keepdims=True)
        if has_sink:
            sink = sink_ref[kv]
            m = jnp.maximum(m, sink)
        p = jnp.exp(s - m)
        l = jnp.sum(p, axis=1, keepdims=True)
        if has_sink:
            l = l + jnp.exp(sink - m)
        o = jnp.dot(p.astype(BF16), v, preferred_element_type=F32) / l
        _store_heads(o_ref, kv, o)
        if want_lse:
            _store_heads(lse_ref, kv, m + jnp.log(l))


def _local_attn_prompt(q, kvb, g, dil, n_batch, seq, sink_rows, want_lse):
    n_g, m, _ = q.shape
    assert m % dil == 0 and seq % (dil * Q_TILE) == 0
    nb = seq // dil // Q_TILE
    qv = q.reshape(n_g, m // dil, dil * Q_COLS)
    kvv = kvb.reshape(n_g, m // dil, dil * 2 * KV_COLS)
    has_sink = sink_rows is not None
    steps = n_batch * dil * nb

    def decode(t):
        tt = jnp.minimum(t, steps - 1)
        return tt // (dil * nb), (tt // nb) % dil, tt % nb

    def cur_map(t):
        n, r, i = decode(t)
        return (g, n * nb + i, r)

    def prev_map(t):
        n, r, i = decode(t)
        return (g, n * nb + jnp.maximum(i - 1, 0), r)

    def out_map(t):
        n, r, i = decode(t)
        tail = t >= steps
        return (jnp.where(tail, n_batch * nb, n * nb + i), jnp.where(tail, t - steps, r))

    in_specs = [
        pl.BlockSpec((None, Q_TILE, Q_COLS), cur_map),
        pl.BlockSpec((None, Q_TILE, 2 * KV_COLS), cur_map),
        pl.BlockSpec((None, Q_TILE, 2 * KV_COLS), prev_map),
    ]
    args = [qv, kvv, kvv]
    if has_sink:
        in_specs.append(pl.BlockSpec((N_KV_HEADS, 4 * Q_TILE, 1), lambda t: (0, 0, 0)))
        args.append(sink_rows)
    out_spec = pl.BlockSpec((Q_TILE, Q_COLS), out_map)
    out_specs = [out_spec]
    out_shape = [jax.ShapeDtypeStruct((m // dil, dil * Q_COLS), BF16)]
    if want_lse:
        out_specs.append(out_spec)
        out_shape.append(jax.ShapeDtypeStruct((m // dil, dil * Q_COLS), F32))
    outs = pl.pallas_call(
        functools.partial(_local_attn_kernel, has_sink=has_sink, want_lse=want_lse, nb=nb, steps=steps),
        grid=(steps + dil,),
        in_specs=in_specs,
        out_specs=out_specs,
        out_shape=out_shape,
        compiler_params=_params(1),
        name=f"local_attn_prompt_d{dil}",
    )(*args)
    return [o.reshape(m, Q_COLS) for o in outs]


def _head_block_mask(rows, heads_per_row_group=1):
    head = lax.broadcasted_iota(I32, (rows, KV_COLS), 0) & (N_HEADS - 1)
    lane = lax.broadcasted_iota(I32, (rows, KV_COLS), 1)
    return (lane // HEAD_DIM) == (head // (N_HEADS // N_KV_HEADS))


def _fold_lanes(x):
    return x[:, 0:LANES] + x[:, LANES:2 * LANES] + x[:, 2 * LANES:3 * LANES] + x[:, 3 * LANES:4 * LANES]


def _sample_attn_kernel(*refs, dil, n_tok, has_sink, want_lse):
    q_ref, c_ref, n_ref = refs[:3]
    pos = 3
    sink_ref = None
    if has_sink:
        sink_ref = refs[pos]
        pos += 1
    o_ref = refs[pos]
    lse_ref = refs[pos + 1] if want_lse else None
    w = LOCAL_WINDOW
    row_w = 2 * KV_COLS
    knew = n_ref[0, :, 0:KV_COLS]
    vnew = n_ref[0, :, KV_COLS:row_w]
    pad = jnp.zeros((w - knew.shape[0], KV_COLS), F32)
    keyi = lax.broadcasted_iota(I32, (N_HEADS, 2 * w), 1)
    bmask = _head_block_mask(N_HEADS)
    for t in range(n_tok):
        r = t if dil > 1 else 0
        kc = c_ref[0, :, r * row_w:r * row_w + KV_COLS]
        vc = c_ref[0, :, r * row_w + KV_COLS:(r + 1) * row_w]
        k_all = jnp.concatenate([kc, knew, pad], axis=0).astype(BF16)
        v_all = jnp.concatenate([vc, vnew, pad], axis=0).astype(BF16)
        s = _nt_dot(q_ref[0, t], k_all)
        if dil == 1:
            valid = ((keyi < w) & (keyi >= t)) | ((keyi >= w) & (keyi <= w + t))
        else:
            valid = (keyi < w) | (keyi == w + t)
        s = jnp.where(valid, s, NEG)
        m = jnp.max(s, axis=1, keepdims=True)
        if has_sink:
            sink = sink_ref[...]
            m = jnp.maximum(m, sink)
        p = jnp.exp(s - m)
        l = jnp.sum(p, axis=1, keepdims=True)
        if has_sink:
            l = l + jnp.exp(sink - m)
        of = jnp.dot(p.astype(BF16), v_all, preferred_element_type=F32)
        o_ref[0, t] = _fold_lanes(jnp.where(bmask, of, 0.0)) / l
        if want_lse:
            lse_ref[0, t] = jnp.broadcast_to(m + jnp.log(l), (N_HEADS, LANES))


def _sample_attn(qbd, cache, kv_new, dil, sink_col, want_lse):
    n_b, n_tok = qbd.shape[:2]
    win = cache.shape[1]
    assert win == LOCAL_WINDOW * dil and (dil == 1 or n_tok <= dil)
    row_w = 2 * KV_COLS
    cv = cache.reshape(n_b, win // dil, dil * row_w)
    n_res = n_tok if dil > 1 else 1
    has_sink = sink_col is not None
    in_specs = [
        pl.BlockSpec((1, n_tok, N_HEADS, KV_COLS), lambda b: (b, 0, 0, 0)),
        pl.BlockSpec((1, LOCAL_WINDOW, n_res * row_w), lambda b: (b, 0, 0)),
        pl.BlockSpec((1, 8, row_w), lambda b: (b, 0, 0)),
    ]
    args = [qbd, cv, kv_new]
    if has_sink:
        in_specs.append(pl.BlockSpec((N_HEADS, 1), lambda b: (0, 0)))
        args.append(sink_col)
    out_spec = pl.BlockSpec((1, n_tok, N_HEADS, LANES), lambda b: (b, 0, 0, 0))
    out_specs = [out_spec]
    out_shape = [jax.ShapeDtypeStruct((n_b, n_tok, N_HEADS, LANES), F32)]
    if want_lse:
        out_specs.append(out_spec)
        out_shape.append(jax.ShapeDtypeStruct((n_b, n_tok, N_HEADS, LANES), F32))
    return pl.pallas_call(
        functools.partial(_sample_attn_kernel, dil=dil, n_tok=n_tok, has_sink=has_sink, want_lse=want_lse),
        grid=(n_b,),
        in_specs=in_specs,
        out_specs=out_specs,
        out_shape=out_shape,
        compiler_params=_params(1),
        name=f"local_attn_sample_d{dil}",
    )(*args)


def _block_mean_kernel(k_ref, o_ref):
    n_blk = k_ref.shape[0] // MOBA_BLOCK
    rows = [jnp.mean(k_ref[b * MOBA_BLOCK:(b + 1) * MOBA_BLOCK, :], axis=0, keepdims=True) for b in range(n_blk)]
    o_ref[...] = jnp.concatenate(rows, axis=0)


def _block_means(kv, n_batch, seq):
    n_blk = seq // MOBA_BLOCK
    return pl.pallas_call(
        _block_mean_kernel,
        grid=(n_batch,),
        in_specs=[pl.BlockSpec((seq, KV_COLS), lambda n: (n, 0))],
        out_specs=pl.BlockSpec((None, n_blk, KV_COLS), lambda n: (n, 0, 0)),
        out_shape=jax.ShapeDtypeStruct((n_batch, n_blk, KV_COLS), F32),
        compiler_params=_params(1),
        name="moba_block_means",
    )(kv)


def _top_blocks(gate, lane, k):
    chosen = jnp.zeros(gate.shape, F32)
    for _ in range(k):
        mx = jnp.max(gate, axis=1, keepdims=True)
        idx = jnp.min(jnp.where(gate == mx, lane, LANES), axis=1, keepdims=True)
        hit = lane == idx
        chosen = jnp.where(hit & (mx > 0.5 * NEG), 1.0, chosen)
        gate = jnp.where(hit, NEG, gate)
    return chosen


def _moba_prompt_kernel(q_ref, kv_ref, km_ref, o_ref, m_scr, l_scr, acc_scr, *, nq, steps):
    t = pl.program_id(0)

    @pl.when(t >= steps)
    def _():
        o_ref[...] = jnp.zeros(o_ref.shape, o_ref.dtype)

    @pl.when(t < steps)
    def _():
        _moba_prompt_step(q_ref, kv_ref, km_ref, o_ref, m_scr, l_scr, acc_scr, t % nq)


def _moba_prompt_step(q_ref, kv_ref, km_ref, o_ref, m_scr, l_scr, acc_scr, i):
    own = (i * Q_TILE) // MOBA_BLOCK
    rows = 4 * Q_TILE
    n_blk = km_ref.shape[0]
    qpos = i * Q_TILE + (lax.broadcasted_iota(I32, (rows, MOBA_BLOCK), 0) & (Q_TILE - 1))
    kcol = lax.broadcasted_iota(I32, (rows, MOBA_BLOCK), 1)
    lane = lax.broadcasted_iota(I32, (rows, LANES), 1)
    km = jnp.concatenate([km_ref[...], jnp.zeros((LANES - n_blk, KV_COLS), F32)], axis=0).astype(BF16)
    for kv in range(N_KV_HEADS):
        q4 = _stack_q_heads(q_ref, kv)
        c = kv // 2
        gate = _nt_dot(q4, _kv_head(km[:, c * LANES:(c + 1) * LANES], kv))
        gate = jnp.where(lane < own, gate, NEG)
        chosen = _top_blocks(gate, lane, min(MOBA_TOPK, n_blk))
        m_scr[...] = jnp.full((rows, 1), NEG, F32)
        l_scr[...] = jnp.zeros((rows, 1), F32)
        acc_scr[...] = jnp.zeros((rows, HEAD_DIM), F32)

        def body(j, carry):
            start = pl.multiple_of(j * MOBA_BLOCK, MOBA_BLOCK)
            k = _kv_head(kv_ref[pl.ds(start, MOBA_BLOCK), c * LANES:(c + 1) * LANES], kv)
            v = _kv_head(kv_ref[pl.ds(start, MOBA_BLOCK), KV_COLS + c * LANES:KV_COLS + (c + 1) * LANES], kv)
            s = _nt_dot(q4, k)
            picked = jnp.max(jnp.where(lane == j, chosen, 0.0), axis=1, keepdims=True) > 0.5
            is_own = j == own
            keep = (is_own & ((start + kcol) <= qpos)) | (jnp.logical_not(is_own) & picked)
            s = jnp.where(keep, s, NEG)
            m_old = m_scr[...]
            m_new = jnp.maximum(m_old, jnp.max(s, axis=1, keepdims=True))
            p = jnp.where(keep, jnp.exp(s - m_new), 0.0)
            alpha = jnp.exp(m_old - m_new)
            l_scr[...] = alpha * l_scr[...] + jnp.sum(p, axis=1, keepdims=True)
            acc_scr[...] = alpha * acc_scr[...] + jnp.dot(p.astype(BF16), v, preferred_element_type=F32)
            m_scr[...] = m_new
            return carry

        lax.fori_loop(0, own + 1, body, 0)
        _store_heads(o_ref, kv, acc_scr[...] / l_scr[...])


def _moba_prompt(q, kvb, kmean, n_batch, seq):
    m = q.shape[0]
    nq = seq // Q_TILE
    n_blk = seq // MOBA_BLOCK
    rows = 4 * Q_TILE
    steps = n_batch * nq
    assert m - steps * Q_TILE <= Q_TILE
    batch_of = lambda t: jnp.minimum(t, steps - 1) // nq
    return pl.pallas_call(
        functools.partial(_moba_prompt_kernel, nq=nq, steps=steps),
        grid=(steps + 1,),
        in_specs=[
            pl.BlockSpec((Q_TILE, Q_COLS), lambda t: (jnp.minimum(t, steps - 1), 0)),
            pl.BlockSpec((seq, 2 * KV_COLS), lambda t: (batch_of(t), 0)),
            pl.BlockSpec((None, n_blk, KV_COLS), lambda t: (batch_of(t), 0, 0)),
        ],
        out_specs=pl.BlockSpec((Q_TILE, Q_COLS), lambda t: (t, 0)),
        out_shape=jax.ShapeDtypeStruct((m, Q_COLS), BF16),
        scratch_shapes=[pltpu.VMEM((rows, 1), F32), pltpu.VMEM((rows, 1), F32), pltpu.VMEM((rows, HEAD_DIM), F32)],
        compiler_params=_params(1),
        name="moba_prompt",
    )(q, kvb, kmean)


def _moba_sample_kernel(pt_ref, q_ref, n_ref, *refs, n_chunks, n_tok):
    k_refs = refs[:PAGES_PER_STEP]
    v_refs = refs[PAGES_PER_STEP:2 * PAGES_PER_STEP]
    o_ref, s_scr, acc_scr, l_scr, m_scr, ch_scr = refs[2 * PAGES_PER_STEP:]
    c = pl.program_id(1)
    rows = q_ref.shape[1]
    step_keys = PAGES_PER_STEP * PAGE_SIZE
    blk_per_step = step_keys // MOBA_BLOCK
    n_blk = n_chunks * blk_per_step
    lane = lax.broadcasted_iota(I32, (rows, LANES), 1)

    @pl.when(c < n_chunks)
    def _scores():
        kc = jnp.concatenate([r[...] for r in k_refs], axis=0).astype(BF16)
        s_scr[:, pl.ds(pl.multiple_of(c * step_keys, step_keys), step_keys)] = _nt_dot(q_ref[0], kc)

    @pl.when(c == n_chunks)
    def _select():
        gate = jnp.full((rows, LANES), NEG, F32)
        bmax = jnp.full((rows, LANES), NEG, F32)
        for b in range(n_blk):
            sb = s_scr[:, b * MOBA_BLOCK:(b + 1) * MOBA_BLOCK]
            gate = jnp.where(lane == b, jnp.sum(sb, axis=1, keepdims=True), gate)
            bmax = jnp.where(lane == b, jnp.max(sb, axis=1, keepdims=True), bmax)
        chosen = _top_blocks(gate, lane, min(MOBA_TOPK, n_blk))
        pad = jnp.zeros((LANES - n_ref.shape[1], KV_COLS), F32)
        knew = jnp.concatenate([n_ref[0, :, 0:KV_COLS], pad], axis=0).astype(BF16)
        vnew = jnp.concatenate([n_ref[0, :, KV_COLS:2 * KV_COLS], pad], axis=0).astype(BF16)
        tok = lax.broadcasted_iota(I32, (rows, LANES), 0) // N_HEADS
        valid = (lane <= tok) & (lane < n_tok)
        s_new = jnp.where(valid, _nt_dot(q_ref[0], knew), NEG)
        m = jnp.maximum(jnp.max(jnp.where(chosen > 0.5, bmax, NEG), axis=1, keepdims=True),
                        jnp.max(s_new, axis=1, keepdims=True))
        p_new = jnp.where(valid, jnp.exp(s_new - m), 0.0)
        acc_scr[...] = jnp.dot(p_new.astype(BF16), vnew, preferred_element_type=F32)
        l_scr[...] = jnp.sum(p_new, axis=1, keepdims=True)
        m_scr[...] = m
        ch_scr[...] = chosen

    @pl.when(c >= n_chunks)
    def _accumulate():
        cc = c - n_chunks
        vc = jnp.concatenate([r[...] for r in v_refs], axis=0).astype(BF16)
        s = s_scr[:, pl.ds(pl.multiple_of(cc * step_keys, step_keys), step_keys)]
        chosen = ch_scr[...]
        keep = jnp.concatenate(
            [jnp.broadcast_to(jnp.max(jnp.where(lane == cc * blk_per_step + u, chosen, 0.0), axis=1, keepdims=True) > 0.5,
                              (rows, MOBA_BLOCK)) for u in range(blk_per_step)], axis=1)
        p = jnp.where(keep, jnp.exp(s - m_scr[...]), 0.0)
        l_scr[...] += jnp.sum(p, axis=1, keepdims=True)
        acc_scr[...] += jnp.dot(p.astype(BF16), vc, preferred_element_type=F32)

    @pl.when(c == 2 * n_chunks - 1)
    def _finish():
        o_ref[0] = _fold_lanes(jnp.where(_head_block_mask(rows), acc_scr[...], 0.0)) / l_scr[...]


def _moba_sample(qbd, kv_new, cache, page_table):
    n_b, rows, _ = qbd.shape
    n_tok = rows // N_HEADS
    n_pages = page_table.shape[1]
    assert n_pages % PAGES_PER_STEP == 0 and (n_pages * PAGE_SIZE) % MOBA_BLOCK == 0
    n_chunks = n_pages // PAGES_PER_STEP
    pool = cache.shape[0]
    cv = cache.reshape(pool, PAGE_SIZE, 2 * KV_COLS)

    def k_map(u):
        return lambda b, c, pt: (pt[b, jnp.minimum(c, n_chunks - 1) * PAGES_PER_STEP + u], 0, 0)

    def v_map(u):
        return lambda b, c, pt: (pt[b, jnp.maximum(c - n_chunks, 0) * PAGES_PER_STEP + u], 0, 1)

    page_specs = ([pl.BlockSpec((None, PAGE_SIZE, KV_COLS), k_map(u)) for u in range(PAGES_PER_STEP)]
                  + [pl.BlockSpec((None, PAGE_SIZE, KV_COLS), v_map(u)) for u in range(PAGES_PER_STEP)])
    grid_spec = pltpu.PrefetchScalarGridSpec(
        num_scalar_prefetch=1,
        grid=(n_b, 2 * n_chunks),
        in_specs=[
            pl.BlockSpec((1, rows, KV_COLS), lambda b, c, pt: (b, 0, 0)),
            pl.BlockSpec((1, 8, 2 * KV_COLS), lambda b, c, pt: (b, 0, 0)),
        ] + page_specs,
        out_specs=pl.BlockSpec((1, rows, LANES), lambda b, c, pt: (b, 0, 0)),
        scratch_shapes=[
            pltpu.VMEM((rows, n_pages * PAGE_SIZE), F32),
            pltpu.VMEM((rows, KV_COLS), F32),
            pltpu.VMEM((rows, 1), F32),
            pltpu.VMEM((rows, 1), F32),
            pltpu.VMEM((rows, LANES), F32),
        ],
    )
    return pl.pallas_call(
        functools.partial(_moba_sample_kernel, n_chunks=n_chunks, n_tok=n_tok),
        grid_spec=grid_spec,
        out_shape=jax.ShapeDtypeStruct((n_b, rows, LANES), F32),
        compiler_params=_params(2),
        name="moba_sample",
    )(page_table, qbd, kv_new, *([cv] * (2 * PAGES_PER_STEP)))


def _layer_norm(h, gain, bias):
    mu = jnp.mean(h, axis=-1, keepdims=True)
    d = h - mu
    var = jnp.mean(d * d, axis=-1, keepdims=True)
    return d * lax.rsqrt(var + LN_EPS) * gain + bias


def _oproj_ln_kernel(*refs, n_merge):
    if n_merge == 1:
        x_ref, o_ref, w_ref, g_ref, b_ref, out_ref = refs
        o = o_ref[...]
    else:
        x_ref, o_ref, lse_ref, w_ref, g_ref, b_ref, out_ref = refs
        lses = [lse_ref[g] for g in range(n_merge)]
        mx = functools.reduce(jnp.maximum, lses)
        es = [jnp.exp(l - mx) for l in lses]
        num = sum(e * o_ref[g].astype(F32) for g, e in enumerate(es))
        o = (num / sum(es)).astype(BF16)
    y = jnp.dot(o, w_ref[...], preferred_element_type=F32)
    out_ref[...] = _layer_norm(DEEPNORM_ALPHA * x_ref[...] + y, g_ref[...], b_ref[...])


def _oproj_ln(x, o, lse, w_bf16, gain, bias):
    m = x.shape[0]
    n_merge = 1 if lse is None else o.shape[0]
    tm = _row_tile(m, 320 if n_merge == 1 else 160, 16)
    row = pl.BlockSpec((tm, D_MODEL), lambda i: (i, 0))
    grp = pl.BlockSpec((n_merge, tm, Q_COLS), lambda i: (0, i, 0))
    vec = pl.BlockSpec((1, D_MODEL), lambda i: (0, 0))
    wspec = pl.BlockSpec((Q_COLS, D_MODEL), lambda i: (0, 0))
    if n_merge == 1:
        in_specs, args = [row, pl.BlockSpec((tm, Q_COLS), lambda i: (i, 0)), wspec, vec, vec], [x, o, w_bf16, gain, bias]
    else:
        in_specs, args = [row, grp, grp, wspec, vec, vec], [x, o, lse, w_bf16, gain, bias]
    return pl.pallas_call(
        functools.partial(_oproj_ln_kernel, n_merge=n_merge),
        grid=(m // tm,),
        in_specs=in_specs,
        out_specs=row,
        out_shape=jax.ShapeDtypeStruct((m, D_MODEL), F32),
        compiler_params=_params(1),
        name=f"oproj_ln_m{n_merge}",
    )(*args)


def _router_kernel(x_ref, rwt_ref, bias_ref, idx_ref, gate_ref, cnt_ref, carry_scr):
    i = pl.program_id(0)
    tm = x_ref.shape[0]

    @pl.when(i == 0)
    def _():
        carry_scr[...] = jnp.zeros(carry_scr.shape, F32)

    logits = _nt_dot(rwt_ref[...], x_ref[...].astype(BF16))
    scores = 1.0 / (1.0 + jnp.exp(-logits))
    biased = scores + bias_ref[...]
    srow = [scores[e:e + 1, :] for e in range(N_EXPERTS)]
    brow = [biased[e:e + 1, :] for e in range(N_EXPERTS)]

    gscore = []
    for g in range(N_EXPERT_GROUPS):
        a0, a1, a2, a3 = brow[4 * g:4 * g + 4]
        hi01, lo01 = jnp.maximum(a0, a1), jnp.minimum(a0, a1)
        hi23, lo23 = jnp.maximum(a2, a3), jnp.minimum(a2, a3)
        gscore.append(jnp.maximum(hi01, hi23) + jnp.maximum(jnp.minimum(hi01, hi23), jnp.maximum(lo01, lo23)))
    gsel = jnp.zeros((1, tm), I32)
    best = gscore[0]
    for g in range(1, N_EXPERT_GROUPS):
        better = gscore[g] > best
        gsel = jnp.where(better, g, gsel)
        best = jnp.where(better, gscore[g], best)

    def in_group(rows_, j):
        v = rows_[j]
        for g in range(1, N_EXPERT_GROUPS):
            v = jnp.where(gsel == g, rows_[4 * g + j], v)
        return v

    ab = [in_group(brow, j) for j in range(EXPERTS_PER_GROUP)]
    au = [in_group(srow, j) for j in range(EXPERTS_PER_GROUP)]
    i1 = jnp.zeros((1, tm), I32)
    v1 = ab[0]
    for j in range(1, EXPERTS_PER_GROUP):
        better = ab[j] > v1
        i1 = jnp.where(better, j, i1)
        v1 = jnp.where(better, ab[j], v1)
    i2 = jnp.full((1, tm), -1, I32)
    v2 = jnp.full((1, tm), -jnp.inf, F32)
    for j in range(EXPERTS_PER_GROUP):
        better = (i1 != j) & ((ab[j] > v2) | (i2 < 0))
        i2 = jnp.where(better, j, i2)
        v2 = jnp.where(better, ab[j], v2)

    def pick(vals, idx):
        v = vals[0]
        for j in range(1, EXPERTS_PER_GROUP):
            v = jnp.where(idx == j, vals[j], v)
        return v

    s1, s2 = pick(au, i1), pick(au, i2)
    den = s1 + s2
    e1 = gsel * EXPERTS_PER_GROUP + i1
    e2 = gsel * EXPERTS_PER_GROUP + i2

    eid = lax.broadcasted_iota(I32, (N_EXPERTS, tm), 0)
    hit1, hit2 = eid == e1, eid == e2
    onehot = jnp.where(hit1 | hit2, 1.0, 0.0)
    before = lax.broadcasted_iota(I32, (tm, tm), 0) < lax.broadcasted_iota(I32, (tm, tm), 1)
    prefix = jnp.dot(onehot.astype(BF16), jnp.where(before, 1.0, 0.0).astype(BF16), preferred_element_type=F32)
    offset = carry_scr[...][:, 0:1] + prefix
    r1 = jnp.sum(jnp.where(hit1, offset, 0.0), axis=0, keepdims=True).astype(I32)
    r2 = jnp.sum(jnp.where(hit2, offset, 0.0), axis=0, keepdims=True).astype(I32)
    carry_scr[...] = carry_scr[...] + jnp.sum(onehot, axis=1, keepdims=True)

    row8 = lax.broadcasted_iota(I32, (8, tm), 0)
    idx_ref[...] = jnp.where(row8 == 0, e1, jnp.where(row8 == 1, e2, jnp.where(row8 == 2, r1, jnp.where(row8 == 3, r2, 0))))
    gate_ref[...] = jnp.where(row8 == 0, s1 / den, jnp.where(row8 == 1, s2 / den, 0.0))
    cnt_ref[...] = carry_scr[...]


def _router(x, rwt_bf16, bias_col):
    m = x.shape[0]
    tm = _row_tile(m, 640, LANES)
    return pl.pallas_call(
        _router_kernel,
        grid=(m // tm,),
        in_specs=[
            pl.BlockSpec((tm, D_MODEL), lambda i: (i, 0)),
            pl.BlockSpec((N_EXPERTS, D_MODEL), lambda i: (0, 0)),
            pl.BlockSpec((N_EXPERTS, 1), lambda i: (0, 0)),
        ],
        out_specs=[
            pl.BlockSpec((8, tm), lambda i: (0, i)),
            pl.BlockSpec((8, tm), lambda i: (0, i)),
            pl.BlockSpec((N_EXPERTS, LANES), lambda i: (0, 0)),
        ],
        out_shape=[
            jax.ShapeDtypeStruct((8, m), I32),
            jax.ShapeDtypeStruct((8, m), F32),
            jax.ShapeDtypeStruct((N_EXPERTS, LANES), F32),
        ],
        scratch_shapes=[pltpu.VMEM((N_EXPERTS, LANES), F32)],
        compiler_params=_params(1),
        name="router",
    )(x, rwt_bf16, bias_col)


def _row_copy(src_hbm, row, dst, slot, r, sem):
    return pltpu.make_async_copy(src_hbm.at[pl.ds(row, 1), :], dst.at[slot, pl.ds(r, 1), :], sem.at[slot])


def _moe_ffn_kernel(ce_ref, tok_ref, x_hbm, wg_ref, wu_ref, wd_ref, y_ref, xbuf, sem, wgb, wub, wdb):
    c = pl.program_id(0)
    n_c = pl.num_programs(0)
    slot = c % 2

    def start(chunk, slot_):
        def body(r, carry):
            _row_copy(x_hbm, tok_ref[chunk * MOE_CHUNK + r], xbuf, slot_, r, sem).start()
            return carry
        lax.fori_loop(0, MOE_CHUNK, body, 0)

    @pl.when(c == 0)
    def _():
        start(0, 0)

    @pl.when(c + 1 < n_c)
    def _():
        start(c + 1, 1 - slot)

    @pl.when((c == 0) | (ce_ref[c] != ce_ref[jnp.maximum(c - 1, 0)]))
    def _():
        wgb[...] = wg_ref[...].astype(BF16)
        wub[...] = wu_ref[...].astype(BF16)
        wdb[...] = wd_ref[...].astype(BF16)

    def wait_row(r, carry):
        _row_copy(x_hbm, 0, xbuf, slot, r, sem).wait()
        return carry
    lax.fori_loop(0, MOE_CHUNK, wait_row, 0)

    xb = xbuf[slot].astype(BF16)
    a = jnp.dot(xb, wgb[...], preferred_element_type=F32)
    u = jnp.dot(xb, wub[...], preferred_element_type=F32)
    h = (a / (1.0 + jnp.exp(-a))) * u
    y_ref[...] = jnp.dot(h.astype(BF16), wdb[...], preferred_element_type=F32)


def _moe_ffn(x, chunk_expert, tok_of_row, w_gate, w_up, w_down):
    n_chunks = chunk_expert.shape[0]
    grid_spec = pltpu.PrefetchScalarGridSpec(
        num_scalar_prefetch=2,
        grid=(n_chunks,),
        in_specs=[
            pl.BlockSpec(memory_space=pl.ANY),
            pl.BlockSpec((None, D_MODEL, D_EXPERT), lambda c, ce, tok: (ce[c], 0, 0)),
            pl.BlockSpec((None, D_MODEL, D_EXPERT), lambda c, ce, tok: (ce[c], 0, 0)),
            pl.BlockSpec((None, D_EXPERT, D_MODEL), lambda c, ce, tok: (ce[c], 0, 0)),
        ],
        out_specs=pl.BlockSpec((MOE_CHUNK, D_MODEL), lambda c, ce, tok: (c, 0)),
        scratch_shapes=[
            pltpu.VMEM((2, MOE_CHUNK, D_MODEL), F32),
            pltpu.SemaphoreType.DMA((2,)),
            pltpu.VMEM((D_MODEL, D_EXPERT), BF16),
            pltpu.VMEM((D_MODEL, D_EXPERT), BF16),
            pltpu.VMEM((D_EXPERT, D_MODEL), BF16),
        ],
    )
    return pl.pallas_call(
        _moe_ffn_kernel,
        grid_spec=grid_spec,
        out_shape=jax.ShapeDtypeStruct((n_chunks * MOE_CHUNK, D_MODEL), F32),
        compiler_params=_params(1),
        name="moe_ffn",
    )(chunk_expert, tok_of_row, x, w_gate, w_up, w_down)


def _combine_ln_kernel(dest_ref, y_hbm, x_ref, g1_ref, g2_ref, gain_ref, bias_ref, out_ref, ybuf, sem):
    i = pl.program_id(0)
    n_i = pl.num_programs(0)
    tm = x_ref.shape[0]
    m = n_i * tm
    slot = i % 2

    def start(tile, slot_):
        def body(r, carry):
            _row_copy(y_hbm, dest_ref[tile * tm + r], ybuf, slot_, r, sem).start()
            _row_copy(y_hbm, dest_ref[m + tile * tm + r], ybuf, slot_, tm + r, sem).start()
            return carry
        lax.fori_loop(0, tm, body, 0)

    @pl.when(i == 0)
    def _():
        start(0, 0)

    @pl.when(i + 1 < n_i)
    def _():
        start(i + 1, 1 - slot)

    def wait_row(r, carry):
        _row_copy(y_hbm, 0, ybuf, slot, r, sem).wait()
        return carry
    lax.fori_loop(0, 2 * tm, wait_row, 0)

    f = g1_ref[...] * ybuf[slot, 0:tm, :] + g2_ref[...] * ybuf[slot, tm:2 * tm, :]
    out_ref[...] = _layer_norm(DEEPNORM_ALPHA * x_ref[...] + f, gain_ref[...], bias_ref[...])


def _combine_ln(x, ybuf, dest_flat, g1, g2, gain, bias):
    m = x.shape[0]
    tm = _row_tile(m, 128, 8)
    row = pl.BlockSpec((tm, D_MODEL), lambda i, d: (i, 0))
    col = pl.BlockSpec((tm, 1), lambda i, d: (i, 0))
    vec = pl.BlockSpec((1, D_MODEL), lambda i, d: (0, 0))
    grid_spec = pltpu.PrefetchScalarGridSpec(
        num_scalar_prefetch=1,
        grid=(m // tm,),
        in_specs=[pl.BlockSpec(memory_space=pl.ANY), row, col, col, vec, vec],
        out_specs=row,
        scratch_shapes=[pltpu.VMEM((2, 2 * tm, D_MODEL), F32), pltpu.SemaphoreType.DMA((2,))],
    )
    return pl.pallas_call(
        _combine_ln_kernel,
        grid_spec=grid_spec,
        out_shape=jax.ShapeDtypeStruct((m, D_MODEL), F32),
        compiler_params=_params(1),
        name="moe_combine_ln",
    )(dest_flat, ybuf, x, g1, g2, gain, bias)


def _moe_layer(x, rwt_bf16, rbias_col, w_gate, w_up, w_down, gain, bias):
    m = x.shape[0]
    idx, gates, counts = _router(x, rwt_bf16, rbias_col)
    experts, ranks = idx[0:2], idx[2:4]
    cnt = counts[:, 0].astype(I32)
    padded = (cnt + MOE_CHUNK - 1) // MOE_CHUNK * MOE_CHUNK
    pend = jnp.cumsum(padded)
    dest = (pend - padded)[experts] + ranks
    n_chunks = -(-2 * m // MOE_CHUNK) + N_EXPERTS
    chunk_expert = jnp.minimum(jnp.searchsorted(pend, jnp.arange(n_chunks, dtype=I32) * MOE_CHUNK, side='right'),
                               N_EXPERTS - 1).astype(I32)
    tok = jnp.tile(jnp.arange(m, dtype=I32), 2)
    tok_of_row = jnp.zeros((n_chunks * MOE_CHUNK,), I32).at[dest.reshape(-1)].set(tok)
    ybuf = _moe_ffn(x, chunk_expert, tok_of_row, w_gate, w_up, w_down)
    return _combine_ln(x, ybuf, dest.reshape(-1), gates[0].reshape(m, 1), gates[1].reshape(m, 1), gain, bias)


def _block_diag_queries(q_rows, n_b, n_tok):
    q4 = q_rows.reshape(n_b, n_tok, N_HEADS, HEAD_DIM)
    tiled = jnp.tile(q4, (1, 1, 1, N_KV_HEADS))
    head = jnp.arange(N_HEADS)[:, None] // (N_HEADS // N_KV_HEADS)
    lane_kv = jnp.arange(KV_COLS)[None, :] // HEAD_DIM
    return jnp.where(head == lane_kv, tiled, jnp.zeros((), tiled.dtype))


def _unfold_sample_out(o, n_b, n_tok):
    return (o[..., :HEAD_DIM] + o[..., HEAD_DIM:]).reshape(n_b * n_tok, Q_COLS)


def _new_kv_rows(kv, n_prompt_rows, n_b, n_tok):
    rows = kv[n_prompt_rows:].reshape(n_b, n_tok, 2 * KV_COLS)
    return jnp.pad(rows, ((0, 0), (0, 8 - n_tok), (0, 0)))


def _window_state(kv, cache, n_batch, seq, n_b, n_tok, window):
    mp = n_batch * seq
    row_shape = (2, N_KV_HEADS, HEAD_DIM)
    kvp = kv[:mp].reshape((n_batch, seq) + row_shape)
    kvs = kv[mp:].reshape((n_b, n_tok) + row_shape)
    state_p = kvp[:, -min(window, seq):]
    state_s = jnp.concatenate([cache, kvs], axis=1)[:, -cache.shape[1]:]
    return state_p, state_s


def kernel(x_prompt, x_sample, cache_swa, cache_dil0, cache_dil1, cache_dil2, cache_moba, page_table,
           w_qkv_swa, w_o_swa, sinks_swa, w_qkv_dil, w_o_dil, w_qkv_moba, w_o_moba,
           ln_gain, ln_bias, router_w, router_bias, w_gate_e, w_up_e, w_down_e):
    n_batch, seq, d = x_prompt.shape
    n_b, n_tok = x_sample.shape[:2]
    past_len = page_table.shape[1] * cache_moba.shape[2]
    mp, ms = n_batch * seq, n_b * n_tok
    assert d == D_MODEL and cache_moba.shape[2] == PAGE_SIZE and n_tok <= 4
    x = jnp.concatenate([x_prompt.reshape(mp, d), x_sample.reshape(ms, d)], axis=0)
    pos = jnp.concatenate([jnp.tile(jnp.arange(seq), n_batch), jnp.tile(past_len + jnp.arange(n_tok), n_b)])
    cos, sin = _rope_tables(pos)
    rwt = router_w.T.astype(BF16)
    rbias = router_bias.astype(F32).reshape(N_EXPERTS, 1)
    dil_caches = (cache_dil0, cache_dil1, cache_dil2)

    def sample_local(q_g, kv_g, cache, dil, sink_col, want_lse):
        qbd = _block_diag_queries(q_g[mp:], n_b, n_tok)
        outs = _sample_attn(qbd, cache, _new_kv_rows(kv_g, mp, n_b, n_tok), dil, sink_col, want_lse)
        o = _unfold_sample_out(outs[0], n_b, n_tok)
        if not want_lse:
            return o, None
        return o, jnp.repeat(outs[1][..., 0].reshape(ms, N_HEADS), HEAD_DIM, axis=1)

    swa_p, swa_s, moba_p, moba_s = [], [], [], []
    dil_p, dil_s = ([], [], []), ([], [], [])
    for i in range(DEPTH):
        kind, j = i % 3, i // 3
        gain1, bias1 = ln_gain[i, 0].reshape(1, d), ln_bias[i, 0].reshape(1, d)
        gain2, bias2 = ln_gain[i, 1].reshape(1, d), ln_bias[i, 1].reshape(1, d)
        if kind == 0:
            q, kv, kvb = _qkv_rope(x, w_qkv_swa[j].astype(BF16), cos, sin, 1)
            sinks = sinks_swa[j].astype(F32)
            sink_rows = jnp.repeat(sinks, Q_TILE).reshape(N_KV_HEADS, 4 * Q_TILE, 1)
            (o,) = _local_attn_prompt(q, kvb, 0, 1, n_batch, seq, sink_rows, False)
            o_s, _ = sample_local(q[0], kv[0], cache_swa[j], 1, sinks.reshape(N_HEADS, 1), False)
            o = lax.dynamic_update_slice(o, o_s.astype(BF16), (mp, 0))
            sp, ss = _window_state(kv[0], cache_swa[j], n_batch, seq, n_b, n_tok, DIL_PAIRS[0][0])
            swa_p.append(sp)
            swa_s.append(ss)
            x = _oproj_ln(x, o, None, w_o_swa[j].astype(BF16), gain1, bias1)
        elif kind == 1:
            q, kv, kvb = _qkv_rope(x, w_qkv_dil[j].astype(BF16), cos, sin, len(DIL_PAIRS))
            os_, lses = [], []
            for g, (win, dil) in enumerate(DIL_PAIRS):
                o, lse = _local_attn_prompt(q, kvb, g, dil, n_batch, seq, None, True)
                o_s, lse_s = sample_local(q[g], kv[g], dil_caches[g][j], dil, None, True)
                os_.append(lax.dynamic_update_slice(o, o_s.astype(BF16), (mp, 0)))
                lses.append(lax.dynamic_update_slice(lse, lse_s, (mp, 0)))
                sp, ss = _window_state(kv[g], dil_caches[g][j], n_batch, seq, n_b, n_tok, win)
                dil_p[g].append(sp)
                dil_s[g].append(ss)
            x = _oproj_ln(x, jnp.stack(os_), jnp.stack(lses), w_o_dil[j].astype(BF16), gain1, bias1)
        else:
            q, kv, kvb = _qkv_rope(x, w_qkv_moba[j].astype(BF16), cos, sin, 1)
            o = _moba_prompt(q[0], kvb[0], _block_means(kv[0], n_batch, seq), n_batch, seq)
            qbd = _block_diag_queries(q[0][mp:], n_b, n_tok).reshape(n_b, n_tok * N_HEADS, KV_COLS)
            o_s = _moba_sample(qbd, _new_kv_rows(kv[0], mp, n_b, n_tok), cache_moba[j], page_table)
            o_s = _unfold_sample_out(o_s.reshape(n_b, n_tok, N_HEADS, LANES), n_b, n_tok)
            o = lax.dynamic_update_slice(o, o_s.astype(BF16), (mp, 0))
            row_shape = (2, N_KV_HEADS, HEAD_DIM)
            moba_p.append(kv[0][:mp].reshape((n_batch, seq) + row_shape))
            moba_s.append(kv[0][mp:].reshape((n_b, n_tok) + row_shape))
            x = _oproj_ln(x, o, None, w_o_moba[j].astype(BF16), gain1, bias1)
        x = _moe_layer(x, rwt, rbias, w_gate_e[i], w_up_e[i], w_down_e[i], gain2, bias2)
    y_prompt = x[:mp].reshape(n_batch, seq, d)
    y_sample = x[mp:].reshape(n_b, n_tok, d)
    return (y_prompt, y_sample, jnp.stack(swa_p), jnp.stack(swa_s),
            jnp.stack(dil_p[0]), jnp.stack(dil_s[0]), jnp.stack(dil_p[1]), jnp.stack(dil_s[1]),
            jnp.stack(dil_p[2]), jnp.stack(dil_s[2]), jnp.stack(moba_p), jnp.stack(moba_s))
```

```python
import functools
import math

import jax
import jax.numpy as jnp
from jax import lax
from jax.experimental import pallas as pl
from jax.experimental.pallas import tpu as pltpu

F32, BF16, I32 = jnp.float32, jnp.bfloat16, jnp.int32

D_MODEL = 2048
HEAD_DIM = 64
N_HEADS = 32
N_KV_HEADS = 8
GQA = N_HEADS // N_KV_HEADS
Q_COLS = N_HEADS * HEAD_DIM
KV_COLS = N_KV_HEADS * HEAD_DIM
QKV_COLS = Q_COLS + 2 * KV_COLS
ATTN_SCALE = HEAD_DIM ** -0.5
ROPE_THETA = 10000.0
SWA_WINDOW = 128
LOCAL_WINDOW = 128
DIL_PAIRS = ((128, 1), (512, 4), (2048, 16))
MOBA_BLOCK = 256
MOBA_TOPK = 3
PAGE_SIZE = 128
N_EXPERTS = 16
N_EXPERT_GROUPS = 4
EXPERTS_PER_GROUP = 4
D_EXPERT = D_MODEL // 4
DEPTH = 4
DEEPNORM_ALPHA = (2 * DEPTH) ** 0.25
LN_EPS = 1e-5
NEG = -1e30

LANES = 128
Q_TILE = 128
QKV_COL_TILE = 512
QKV_ROW_TILE = 512
MOE_CHUNK = 256
PAGES_PER_STEP = 8
VMEM_LIMIT = 56 * 1024 * 1024


def _params(n_axes, vmem=VMEM_LIMIT):
    return pltpu.CompilerParams(dimension_semantics=("arbitrary",) * n_axes, vmem_limit_bytes=vmem)


def _nt_dot(a, b):
    return lax.dot_general(a, b, (((1,), (1,)), ((), ())), preferred_element_type=F32)


def _row_tile(m, pref, align):
    best = m
    for t in range(align, min(pref, m) + 1, align):
        if m % t == 0:
            best = t
    return best


def _rope_tables(pos):
    half = HEAD_DIM // 2
    inv_freq = jnp.exp(jnp.arange(half, dtype=F32) * (-2.0 * math.log(ROPE_THETA) / HEAD_DIM))
    ang = pos.astype(F32)[:, None] * inv_freq[None, :]
    cos, sin = jnp.cos(ang), jnp.sin(ang)
    return jnp.tile(cos, (1, 4)), jnp.concatenate([-sin, sin, -sin, sin], axis=1)


def _qkv_rope_kernel(x_ref, w_ref, cos_ref, sin_ref, q_ref, kv_ref, *kvt_ref):
    j = pl.program_id(2)
    z = jnp.dot(x_ref[...].astype(BF16), w_ref[...], preferred_element_type=F32)
    tm = z.shape[0]

    def rope(z):
        cos, sin = cos_ref[...], sin_ref[...]
        lane = lax.broadcasted_iota(I32, (tm, LANES), 1)
        first_half = (lane & (HEAD_DIM - 1)) < HEAD_DIM // 2
        parts = []
        for c in range(z.shape[1] // LANES):
            zc = z[:, c * LANES:(c + 1) * LANES]
            other = jnp.where(first_half, pltpu.roll(zc, LANES - HEAD_DIM // 2, 1), pltpu.roll(zc, HEAD_DIM // 2, 1))
            parts.append(zc * cos + other * sin)
        return jnp.concatenate(parts, axis=1)

    n_q = Q_COLS // QKV_COL_TILE

    @pl.when(j < n_q)
    def _():
        q_ref[...] = (rope(z) * ATTN_SCALE).astype(BF16)

    def put_kv(val):
        kv_ref[...] = val.astype(kv_ref.dtype)
        if kvt_ref:
            kvt_ref[0][...] = val.T

    @pl.when(j == n_q)
    def _():
        put_kv(rope(z))

    @pl.when(j == n_q + 1)
    def _():
        put_kv(z)


def _qkv_rope(x, w_bf16, cos, sin, n_groups, row0, n_rows, seq):
    tm = QKV_ROW_TILE if seq is not None else n_rows
    assert n_rows % tm == 0 and row0 % tm == 0 and (seq is None or seq % tm == 0)
    blk0 = row0 // tm
    n_q = Q_COLS // QKV_COL_TILE
    n_j = QKV_COLS // QKV_COL_TILE
    kv_col = lambda j: jnp.maximum(j - n_q, 0)
    out_specs = [
        pl.BlockSpec((None, tm, QKV_COL_TILE), lambda g, i, j: (g, i, jnp.minimum(j, n_q - 1))),
        pl.BlockSpec((None, tm, QKV_COL_TILE), lambda g, i, j: (g, i, kv_col(j))),
    ]
    out_shape = [
        jax.ShapeDtypeStruct((n_groups, n_rows, Q_COLS), BF16),
        jax.ShapeDtypeStruct((n_groups, n_rows, 2 * KV_COLS), BF16 if seq is not None else F32),
    ]
    if seq is not None:
        per = seq // tm
        out_specs.append(pl.BlockSpec((None, None, QKV_COL_TILE, tm), lambda g, i, j: (g, i // per, kv_col(j), i % per)))
        out_shape.append(jax.ShapeDtypeStruct((n_groups, n_rows // seq, 2 * KV_COLS, seq), F32))
    return pl.pallas_call(
        _qkv_rope_kernel,
        grid=(n_groups, n_rows // tm, n_j),
        in_specs=[
            pl.BlockSpec((tm, D_MODEL), lambda g, i, j: (blk0 + i, 0)),
            pl.BlockSpec((D_MODEL, QKV_COL_TILE), lambda g, i, j: (0, g * n_j + j)),
            pl.BlockSpec((tm, LANES), lambda g, i, j: (blk0 + i, 0)),
            pl.BlockSpec((tm, LANES), lambda g, i, j: (blk0 + i, 0)),
        ],
        out_specs=out_specs,
        out_shape=out_shape,
        compiler_params=_params(3),
        name="qkv_rope_prompt" if seq is not None else "qkv_rope_sample",
    )(x, w_bf16, cos, sin)


def _stack_q_heads(q_ref, kv):
    qa = q_ref[:, (2 * kv) * LANES:(2 * kv + 1) * LANES]
    qb = q_ref[:, (2 * kv + 1) * LANES:(2 * kv + 2) * LANES]
    return jnp.concatenate([qa[:, :HEAD_DIM], qa[:, HEAD_DIM:], qb[:, :HEAD_DIM], qb[:, HEAD_DIM:]], axis=0)


def _store_transposed_heads(ref, t_scr):
    for g in range(GQA):
        ref[:, g * KV_COLS:(g + 1) * KV_COLS] = jnp.transpose(t_scr[:, g * Q_TILE:(g + 1) * Q_TILE]).astype(ref.dtype)


def _grouped_head_order(w_o):
    return w_o.reshape(N_KV_HEADS, GQA, HEAD_DIM, -1).transpose(1, 0, 2, 3).reshape(Q_COLS, -1)


def _local_attn_kernel(*refs, has_sink, want_lse, nb, steps):
    q_ref, kc_ref, kp_ref = refs[:3]
    pos = 3
    sink_ref = None
    if has_sink:
        sink_ref = refs[pos]
        pos += 1
    o_ref = refs[pos]
    pos += 1
    lse_ref = None
    if want_lse:
        lse_ref = refs[pos]
        pos += 1
    ot_scr = refs[pos]
    lt_scr = refs[pos + 1] if want_lse else None
    t = pl.program_id(0)

    @pl.when(t >= steps)
    def _():
        o_ref[...] = jnp.zeros(o_ref.shape, o_ref.dtype)
        if want_lse:
            lse_ref[...] = jnp.zeros(lse_ref.shape, lse_ref.dtype)

    @pl.when(t < steps)
    def _():
        _local_attn_step(q_ref, kc_ref, kp_ref, sink_ref, o_ref, lse_ref, ot_scr, lt_scr, t % nb)


def _local_attn_step(q_ref, kc_ref, kp_ref, sink_ref, o_ref, lse_ref, ot_scr, lt_scr, i):
    w = LOCAL_WINDOW
    cols = GQA * Q_TILE
    ci = lax.broadcasted_iota(I32, (2 * w, cols), 0)
    qi = lax.broadcasted_iota(I32, (2 * w, cols), 1) & (Q_TILE - 1)
    delta = qi + w - ci
    mask = (delta >= 0) & (delta <= w) & ((ci >= w) | (i > 0))
    for c in range(N_KV_HEADS // 2):
        ksl = slice(c * LANES, (c + 1) * LANES)
        vsl = slice(KV_COLS + c * LANES, KV_COLS + (c + 1) * LANES)
        kpair = jnp.concatenate([kp_ref[:, ksl], kc_ref[:, ksl]], axis=0)
        vpair = jnp.concatenate([kp_ref[:, vsl], kc_ref[:, vsl]], axis=0)
        vpair_t = jnp.transpose(vpair.astype(F32)).astype(BF16)
        for half in range(2):
            kv = 2 * c + half
            hs = slice(half * HEAD_DIM, (half + 1) * HEAD_DIM)
            st = jnp.where(mask, _nt_dot(kpair[:, hs], _stack_q_heads(q_ref, kv)), NEG)
            m = jnp.max(st, axis=0, keepdims=True)
            if sink_ref is not None:
                sink = sink_ref[kv:kv + 1, :]
                m = jnp.maximum(m, sink)
            pt = jnp.exp(st - m)
            l = jnp.sum(pt, axis=0, keepdims=True)
            if sink_ref is not None:
                l = l + jnp.exp(sink - m)
            acc = jnp.dot(vpair_t[hs, :], pt.astype(BF16), preferred_element_type=F32)
            rows = slice(kv * HEAD_DIM, (kv + 1) * HEAD_DIM)
            ot_scr[rows, :] = acc / l
            if lt_scr is not None:
                lt_scr[rows, :] = jnp.broadcast_to(m + jnp.log(l), (HEAD_DIM, cols))
    _store_transposed_heads(o_ref, ot_scr)
    if lt_scr is not None:
        _store_transposed_heads(lse_ref, lt_scr)


def _local_attn_prompt(q, kvb, m, g, dil, n_batch, seq, sink_rows, want_lse):
    n_g, mp, _ = q.shape
    assert m % dil == 0 and seq % (dil * Q_TILE) == 0 and (m - mp) // dil <= Q_TILE
    nb = seq // dil // Q_TILE
    qv = q.reshape(n_g, mp // dil, dil * Q_COLS)
    kvv = kvb.reshape(n_g, mp // dil, dil * 2 * KV_COLS)
    has_sink = sink_rows is not None
    steps = n_batch * dil * nb

    def decode(t):
        tt = jnp.minimum(t, steps - 1)
        return tt // (dil * nb), (tt // nb) % dil, tt % nb

    def cur_map(t):
        n, r, i = decode(t)
        return (g, n * nb + i, r)

    def prev_map(t):
        n, r, i = decode(t)
        return (g, n * nb + jnp.maximum(i - 1, 0), r)

    def out_map(t):
        n, r, i = decode(t)
        tail = t >= steps
        return (jnp.where(tail, n_batch * nb, n * nb + i), jnp.where(tail, t - steps, r))

    in_specs = [
        pl.BlockSpec((None, Q_TILE, Q_COLS), cur_map),
        pl.BlockSpec((None, Q_TILE, 2 * KV_COLS), cur_map),
        pl.BlockSpec((None, Q_TILE, 2 * KV_COLS), prev_map),
    ]
    args = [qv, kvv, kvv]
    if has_sink:
        in_specs.append(pl.BlockSpec((N_KV_HEADS, GQA * Q_TILE), lambda t: (0, 0)))
        args.append(sink_rows)
    out_spec = pl.BlockSpec((Q_TILE, Q_COLS), out_map)
    out_specs = [out_spec]
    out_shape = [jax.ShapeDtypeStruct((m // dil, dil * Q_COLS), BF16)]
    scratch = [pltpu.VMEM((KV_COLS, GQA * Q_TILE), F32)]
    if want_lse:
        out_specs.append(out_spec)
        out_shape.append(jax.ShapeDtypeStruct((m // dil, dil * Q_COLS), F32))
        scratch.append(pltpu.VMEM((KV_COLS, GQA * Q_TILE), F32))
    outs = pl.pallas_call(
        functools.partial(_local_attn_kernel, has_sink=has_sink, want_lse=want_lse, nb=nb, steps=steps),
        grid=(steps + dil,),
        in_specs=in_specs,
        out_specs=out_specs,
        out_shape=out_shape,
        scratch_shapes=scratch,
        compiler_params=_params(1),
        name=f"local_attn_prompt_d{dil}",
    )(*args)
    return [o.reshape(m, Q_COLS) for o in outs]


def _head_block_mask(rows):
    head = lax.broadcasted_iota(I32, (rows, KV_COLS), 0) & (N_HEADS - 1)
    lane = lax.broadcasted_iota(I32, (rows, KV_COLS), 1)
    return (lane // HEAD_DIM) == (head // GQA)


def _fold_lanes(x):
    return x[:, 0:LANES] + x[:, LANES:2 * LANES] + x[:, 2 * LANES:3 * LANES] + x[:, 3 * LANES:4 * LANES]


def _padded_new_rows(n_ref):
    pad = jnp.zeros((LANES - n_ref.shape[1], KV_COLS), F32)
    knew = jnp.concatenate([n_ref[0, :, 0:KV_COLS], pad], axis=0).astype(BF16)
    vnew = jnp.concatenate([n_ref[0, :, KV_COLS:2 * KV_COLS], pad], axis=0).astype(BF16)
    return knew, vnew


def _sample_attn_kernel(*refs, dil, n_tok, has_sink, want_lse):
    q_ref, c_ref, n_ref = refs[:3]
    pos = 3
    sink_ref = None
    if has_sink:
        sink_ref = refs[pos]
        pos += 1
    o_ref = refs[pos]
    lse_ref = refs[pos + 1] if want_lse else None
    rows = n_tok * N_HEADS
    win = c_ref.shape[-1]
    q = q_ref[0]
    knew, vnew = _padded_new_rows(n_ref)
    tok_c = lax.broadcasted_iota(I32, (rows, win), 0) // N_HEADS
    pos_c = lax.broadcasted_iota(I32, (rows, win), 1)
    valid_c = (((pos_c - tok_c) & (dil - 1)) == 0) & (pos_c >= tok_c)
    tok_n = lax.broadcasted_iota(I32, (rows, LANES), 0) // N_HEADS
    pos_n = lax.broadcasted_iota(I32, (rows, LANES), 1)
    valid_n = (((tok_n - pos_n) & (dil - 1)) == 0) & (pos_n <= tok_n)
    s_c = jnp.where(valid_c, jnp.dot(q, c_ref[0, 0].astype(BF16), preferred_element_type=F32), NEG)
    s_n = jnp.where(valid_n, _nt_dot(q, knew), NEG)
    m = jnp.maximum(jnp.max(s_c, axis=1, keepdims=True), jnp.max(s_n, axis=1, keepdims=True))
    if has_sink:
        sink = sink_ref[...]
        m = jnp.maximum(m, sink)
    p_c = jnp.exp(s_c - m)
    p_n = jnp.exp(s_n - m)
    l = jnp.sum(p_c, axis=1, keepdims=True) + jnp.sum(p_n, axis=1, keepdims=True)
    if has_sink:
        l = l + jnp.exp(sink - m)
    of = (_nt_dot(p_c.astype(BF16), c_ref[0, 1].astype(BF16))
          + jnp.dot(p_n.astype(BF16), vnew, preferred_element_type=F32))
    o_ref[0] = _fold_lanes(jnp.where(_head_block_mask(rows), of, 0.0)) / l
    if want_lse:
        lse_ref[0] = jnp.broadcast_to(m + jnp.log(l), (rows, LANES))


def _sample_attn(qbd, cache_t, kv_new, dil, sink_col, want_lse):
    n_b, rows, _ = qbd.shape
    n_tok = rows // N_HEADS
    win = cache_t.shape[-1]
    assert win == LOCAL_WINDOW * dil and (dil & (dil - 1)) == 0 and n_tok <= 8
    has_sink = sink_col is not None
    in_specs = [
        pl.BlockSpec((1, rows, KV_COLS), lambda b: (b, 0, 0)),
        pl.BlockSpec((1, 2, KV_COLS, win), lambda b: (b, 0, 0, 0)),
        pl.BlockSpec((1, 8, 2 * KV_COLS), lambda b: (b, 0, 0)),
    ]
    args = [qbd, cache_t, kv_new]
    if has_sink:
        in_specs.append(pl.BlockSpec((rows, 1), lambda b: (0, 0)))
        args.append(sink_col)
    out_spec = pl.BlockSpec((1, rows, LANES), lambda b: (b, 0, 0))
    out_specs = [out_spec]
    out_shape = [jax.ShapeDtypeStruct((n_b, rows, LANES), F32)]
    if want_lse:
        out_specs.append(out_spec)
        out_shape.append(jax.ShapeDtypeStruct((n_b, rows, LANES), F32))
    return pl.pallas_call(
        functools.partial(_sample_attn_kernel, dil=dil, n_tok=n_tok, has_sink=has_sink, want_lse=want_lse),
        grid=(n_b,),
        in_specs=in_specs,
        out_specs=out_specs,
        out_shape=out_shape,
        compiler_params=_params(1),
        name=f"local_attn_sample_d{dil}",
    )(*args)


def _block_mean_kernel(k_ref, o_ref):
    n_blk = k_ref.shape[1] // MOBA_BLOCK
    cols = [jnp.mean(k_ref[:, b * MOBA_BLOCK:(b + 1) * MOBA_BLOCK], axis=1, keepdims=True) for b in range(n_blk)]
    lane = lax.broadcasted_iota(I32, (k_ref.shape[0], LANES), 1)
    out = jnp.zeros((k_ref.shape[0], LANES), F32)
    for b in range(n_blk):
        out = jnp.where(lane == b, cols[b], out)
    o_ref[...] = out


def _block_means(kvt):
    n_batch, _, seq = kvt.shape
    assert seq // MOBA_BLOCK <= LANES
    return pl.pallas_call(
        _block_mean_kernel,
        grid=(n_batch,),
        in_specs=[pl.BlockSpec((None, KV_COLS, seq), lambda n: (n, 0, 0))],
        out_specs=pl.BlockSpec((None, KV_COLS, LANES), lambda n: (n, 0, 0)),
        out_shape=jax.ShapeDtypeStruct((n_batch, KV_COLS, LANES), F32),
        compiler_params=_params(1),
        name="moba_block_means",
    )(kvt)


def _moba_prompt_kernel(q_ref, kv_ref, km_ref, o_ref, vt_scr, ch_scr, ot_scr, *, nq, steps, n_blk):
    t = pl.program_id(0)

    @pl.when(t >= steps)
    def _():
        o_ref[...] = jnp.zeros(o_ref.shape, o_ref.dtype)

    @pl.when(t < steps)
    def _():
        _moba_prompt_step(q_ref, kv_ref, km_ref, o_ref, vt_scr, ch_scr, ot_scr, t % nq, n_blk)


def _moba_prompt_step(q_ref, kv_ref, km_ref, o_ref, vt_scr, ch_scr, ot_scr, i, n_blk):
    own = (i * Q_TILE) // MOBA_BLOCK
    cols = GQA * Q_TILE

    @pl.when(i == 0)
    def _():
        for b in range(n_blk):
            for c in range(KV_COLS // LANES):
                v = kv_ref[b * MOBA_BLOCK:(b + 1) * MOBA_BLOCK, KV_COLS + c * LANES:KV_COLS + (c + 1) * LANES]
                vt_scr[c * LANES:(c + 1) * LANES, b * MOBA_BLOCK:(b + 1) * MOBA_BLOCK] = (
                    jnp.transpose(v.astype(F32)).astype(BF16))

    qpos = i * Q_TILE + (lax.broadcasted_iota(I32, (MOBA_BLOCK, cols), 1) & (Q_TILE - 1))
    krow = lax.broadcasted_iota(I32, (MOBA_BLOCK, cols), 0)
    bid = lax.broadcasted_iota(I32, (LANES, cols), 0)
    km_t = jnp.transpose(km_ref[...]).astype(BF16)
    for kv in range(N_KV_HEADS):
        q4 = _stack_q_heads(q_ref, kv)
        c = kv // 2
        hs = slice((kv % 2) * HEAD_DIM, (kv % 2 + 1) * HEAD_DIM)
        gate = _nt_dot(km_t[:, kv * HEAD_DIM:(kv + 1) * HEAD_DIM], q4)
        gate = jnp.where(bid < own, gate, NEG)
        rank = jnp.zeros((LANES, cols), F32)
        for b in range(n_blk):
            gb = gate[b:b + 1, :]
            rank = rank + jnp.where((gb > gate) | ((gb == gate) & (b < bid)), 1.0, 0.0)
        ch_scr[...] = jnp.where((rank < min(MOBA_TOPK, n_blk)) & (bid < own), 1.0, 0.0)

        def body(j, carry):
            m_old, l_old, acc = carry
            start = pl.multiple_of(j * MOBA_BLOCK, MOBA_BLOCK)
            k = kv_ref[pl.ds(start, MOBA_BLOCK), c * LANES:(c + 1) * LANES][:, hs]
            st = _nt_dot(k, q4)
            picked = ch_scr[pl.ds(j, 1), :] > 0.5
            is_own = j == own
            keep = (is_own & ((start + krow) <= qpos)) | (jnp.logical_not(is_own) & picked)
            st = jnp.where(keep, st, NEG)
            m_new = jnp.maximum(m_old, jnp.max(st, axis=0, keepdims=True))
            pt = jnp.where(keep, jnp.exp(st - m_new), 0.0)
            alpha = jnp.exp(m_old - m_new)
            vt = vt_scr[kv * HEAD_DIM:(kv + 1) * HEAD_DIM, pl.ds(start, MOBA_BLOCK)]
            acc = alpha * acc + jnp.dot(vt, pt.astype(BF16), preferred_element_type=F32)
            return m_new, alpha * l_old + jnp.sum(pt, axis=0, keepdims=True), acc

        init = (jnp.full((1, cols), NEG, F32), jnp.zeros((1, cols), F32), jnp.zeros((HEAD_DIM, cols), F32))
        _, l, acc = lax.fori_loop(0, own + 1, body, init)
        ot_scr[kv * HEAD_DIM:(kv + 1) * HEAD_DIM, :] = acc / l
    _store_transposed_heads(o_ref, ot_scr)


def _moba_prompt(q, kvb, kmean, m, n_batch, seq):
    nq = seq // Q_TILE
    n_blk = seq // MOBA_BLOCK
    steps = n_batch * nq
    assert m - steps * Q_TILE <= Q_TILE
    batch_of = lambda t: jnp.minimum(t, steps - 1) // nq
    return pl.pallas_call(
        functools.partial(_moba_prompt_kernel, nq=nq, steps=steps, n_blk=n_blk),
        grid=(steps + 1,),
        in_specs=[
            pl.BlockSpec((Q_TILE, Q_COLS), lambda t: (jnp.minimum(t, steps - 1), 0)),
            pl.BlockSpec((seq, 2 * KV_COLS), lambda t: (batch_of(t), 0)),
            pl.BlockSpec((None, KV_COLS, LANES), lambda t: (batch_of(t), 0, 0)),
        ],
        out_specs=pl.BlockSpec((Q_TILE, Q_COLS), lambda t: (t, 0)),
        out_shape=jax.ShapeDtypeStruct((m, Q_COLS), BF16),
        scratch_shapes=[
            pltpu.VMEM((KV_COLS, seq), BF16),
            pltpu.VMEM((LANES, GQA * Q_TILE), F32),
            pltpu.VMEM((KV_COLS, GQA * Q_TILE), F32),
        ],
        compiler_params=_params(1),
        name="moba_prompt",
    )(q, kvb, kmean)


def _top_blocks(gate, lane, k):
    chosen = jnp.zeros(gate.shape, F32)
    for _ in range(k):
        mx = jnp.max(gate, axis=1, keepdims=True)
        idx = jnp.min(jnp.where(gate == mx, lane, LANES), axis=1, keepdims=True)
        hit = lane == idx
        chosen = jnp.where(hit & (mx > 0.5 * NEG), 1.0, chosen)
        gate = jnp.where(hit, NEG, gate)
    return chosen


def _moba_sample_kernel(pt_ref, q_ref, n_ref, *refs, n_chunks, n_tok):
    k_refs = refs[:PAGES_PER_STEP]
    v_refs = refs[PAGES_PER_STEP:2 * PAGES_PER_STEP]
    o_ref, s_scr, acc_scr, l_scr, m_scr, ch_scr = refs[2 * PAGES_PER_STEP:]
    c = pl.program_id(1)
    rows = q_ref.shape[1]
    step_keys = PAGES_PER_STEP * PAGE_SIZE
    blk_per_step = step_keys // MOBA_BLOCK
    n_blk = n_chunks * blk_per_step
    lane = lax.broadcasted_iota(I32, (rows, LANES), 1)

    @pl.when(c < n_chunks)
    def _scores():
        kt = jnp.concatenate([r[...] for r in k_refs], axis=1).astype(BF16)
        s_scr[:, pl.ds(pl.multiple_of(c * step_keys, step_keys), step_keys)] = (
            jnp.dot(q_ref[0], kt, preferred_element_type=F32))

    @pl.when(c == n_chunks)
    def _select():
        gate = jnp.full((rows, LANES), NEG, F32)
        bmax = jnp.full((rows, LANES), NEG, F32)
        for b in range(n_blk):
            sb = s_scr[:, b * MOBA_BLOCK:(b + 1) * MOBA_BLOCK]
            gate = jnp.where(lane == b, jnp.sum(sb, axis=1, keepdims=True), gate)
            bmax = jnp.where(lane == b, jnp.max(sb, axis=1, keepdims=True), bmax)
        chosen = _top_blocks(gate, lane, min(MOBA_TOPK, n_blk))
        knew, vnew = _padded_new_rows(n_ref)
        tok = lax.broadcasted_iota(I32, (rows, LANES), 0) // N_HEADS
        valid = (lane <= tok) & (lane < n_tok)
        s_new = jnp.where(valid, _nt_dot(q_ref[0], knew), NEG)
        m = jnp.maximum(jnp.max(jnp.where(chosen > 0.5, bmax, NEG), axis=1, keepdims=True),
                        jnp.max(s_new, axis=1, keepdims=True))
        p_new = jnp.where(valid, jnp.exp(s_new - m), 0.0)
        acc_scr[...] = jnp.dot(p_new.astype(BF16), vnew, preferred_element_type=F32)
        l_scr[...] = jnp.sum(p_new, axis=1, keepdims=True)
        m_scr[...] = m
        ch_scr[...] = chosen

    @pl.when(c >= n_chunks)
    def _accumulate():
        cc = c - n_chunks
        vt = jnp.concatenate([r[...] for r in v_refs], axis=1).astype(BF16)
        s = s_scr[:, pl.ds(pl.multiple_of(cc * step_keys, step_keys), step_keys)]
        chosen = ch_scr[...]
        keep = jnp.concatenate(
            [jnp.broadcast_to(jnp.max(jnp.where(lane == cc * blk_per_step + u, chosen, 0.0), axis=1, keepdims=True) > 0.5,
                              (rows, MOBA_BLOCK)) for u in range(blk_per_step)], axis=1)
        p = jnp.where(keep, jnp.exp(s - m_scr[...]), 0.0)
        l_scr[...] += jnp.sum(p, axis=1, keepdims=True)
        acc_scr[...] += _nt_dot(p.astype(BF16), vt)

    @pl.when(c == 2 * n_chunks - 1)
    def _finish():
        o_ref[0] = _fold_lanes(jnp.where(_head_block_mask(rows), acc_scr[...], 0.0)) / l_scr[...]


def _moba_sample(qbd, kv_new, cache_t, page_table):
    n_b, rows, _ = qbd.shape
    n_tok = rows // N_HEADS
    n_pages = page_table.shape[1]
    assert n_pages % PAGES_PER_STEP == 0 and (n_pages * PAGE_SIZE) % MOBA_BLOCK == 0
    assert n_pages * PAGE_SIZE // MOBA_BLOCK <= LANES
    n_chunks = n_pages // PAGES_PER_STEP

    def k_map(u):
        return lambda b, c, pt: (pt[b, jnp.minimum(c, n_chunks - 1) * PAGES_PER_STEP + u], 0, 0, 0)

    def v_map(u):
        return lambda b, c, pt: (pt[b, jnp.maximum(c - n_chunks, 0) * PAGES_PER_STEP + u], 1, 0, 0)

    page_specs = ([pl.BlockSpec((None, None, KV_COLS, PAGE_SIZE), k_map(u)) for u in range(PAGES_PER_STEP)]
                  + [pl.BlockSpec((None, None, KV_COLS, PAGE_SIZE), v_map(u)) for u in range(PAGES_PER_STEP)])
    grid_spec = pltpu.PrefetchScalarGridSpec(
        num_scalar_prefetch=1,
        grid=(n_b, 2 * n_chunks),
        in_specs=[
            pl.BlockSpec((1, rows, KV_COLS), lambda b, c, pt: (b, 0, 0)),
            pl.BlockSpec((1, 8, 2 * KV_COLS), lambda b, c, pt: (b, 0, 0)),
        ] + page_specs,
        out_specs=pl.BlockSpec((1, rows, LANES), lambda b, c, pt: (b, 0, 0)),
        scratch_shapes=[
            pltpu.VMEM((rows, n_pages * PAGE_SIZE), F32),
            pltpu.VMEM((rows, KV_COLS), F32),
            pltpu.VMEM((rows, 1), F32),
            pltpu.VMEM((rows, 1), F32),
            pltpu.VMEM((rows, LANES), F32),
        ],
    )
    return pl.pallas_call(
        functools.partial(_moba_sample_kernel, n_chunks=n_chunks, n_tok=n_tok),
        grid_spec=grid_spec,
        out_shape=jax.ShapeDtypeStruct((n_b, rows, LANES), F32),
        compiler_params=_params(2),
        name="moba_sample",
    )(page_table, qbd, kv_new, *([cache_t] * (2 * PAGES_PER_STEP)))


def _layer_norm(h, gain, bias):
    mu = jnp.mean(h, axis=-1, keepdims=True)
    d = h - mu
    var = jnp.mean(d * d, axis=-1, keepdims=True)
    return d * lax.rsqrt(var + LN_EPS) * gain + bias


def _oproj_ln_kernel(*refs, n_merge):
    x_ref = refs[0]
    o_refs = refs[1:1 + n_merge]
    w_ref, g_ref, b_ref, out_ref = refs[-4:]
    if n_merge == 1:
        o = o_refs[0][...]
    else:
        lses = [r[...] for r in refs[1 + n_merge:1 + 2 * n_merge]]
        mx = functools.reduce(jnp.maximum, lses)
        es = [jnp.exp(l - mx) for l in lses]
        num = sum(e * r[...].astype(F32) for r, e in zip(o_refs, es))
        o = (num / sum(es)).astype(BF16)
    y = jnp.dot(o, w_ref[...], preferred_element_type=F32)
    out_ref[...] = _layer_norm(DEEPNORM_ALPHA * x_ref[...] + y, g_ref[...], b_ref[...])


def _oproj_ln(x, o, lse, w_bf16, gain, bias):
    m = x.shape[0]
    n_merge = len(o)
    tm = _row_tile(m, 320 if n_merge == 1 else 160, 16)
    row = pl.BlockSpec((tm, D_MODEL), lambda i: (i, 0))
    vec = pl.BlockSpec((1, D_MODEL), lambda i: (0, 0))
    wspec = pl.BlockSpec((Q_COLS, D_MODEL), lambda i: (0, 0))
    groups = list(o) + (list(lse) if n_merge > 1 else [])
    in_specs = [row] + [pl.BlockSpec((tm, Q_COLS), lambda i: (i, 0))] * len(groups) + [wspec, vec, vec]
    args = [x] + groups + [w_bf16, gain, bias]
    return pl.pallas_call(
        functools.partial(_oproj_ln_kernel, n_merge=n_merge),
        grid=(m // tm,),
        in_specs=in_specs,
        out_specs=row,
        out_shape=jax.ShapeDtypeStruct((m, D_MODEL), F32),
        compiler_params=_params(1),
        name=f"oproj_ln_m{n_merge}",
    )(*args)


def _router_kernel(x_ref, rwt_ref, bias_ref, idx_ref, gate_ref, cnt_ref, carry_scr):
    i = pl.program_id(0)
    tm = x_ref.shape[0]

    @pl.when(i == 0)
    def _():
        carry_scr[...] = jnp.zeros(carry_scr.shape, F32)

    logits = _nt_dot(rwt_ref[...], x_ref[...].astype(BF16))
    scores = 1.0 / (1.0 + jnp.exp(-logits))
    biased = scores + bias_ref[...]
    srow = [scores[e:e + 1, :] for e in range(N_EXPERTS)]
    brow = [biased[e:e + 1, :] for e in range(N_EXPERTS)]

    gscore = []
    for g in range(N_EXPERT_GROUPS):
        a0, a1, a2, a3 = brow[4 * g:4 * g + 4]
        hi01, lo01 = jnp.maximum(a0, a1), jnp.minimum(a0, a1)
        hi23, lo23 = jnp.maximum(a2, a3), jnp.minimum(a2, a3)
        gscore.append(jnp.maximum(hi01, hi23) + jnp.maximum(jnp.minimum(hi01, hi23), jnp.maximum(lo01, lo23)))
    gsel = jnp.zeros((1, tm), I32)
    best = gscore[0]
    for g in range(1, N_EXPERT_GROUPS):
        better = gscore[g] > best
        gsel = jnp.where(better, g, gsel)
        best = jnp.where(better, gscore[g], best)

    def in_group(rows_, j):
        v = rows_[j]
        for g in range(1, N_EXPERT_GROUPS):
            v = jnp.where(gsel == g, rows_[4 * g + j], v)
        return v

    ab = [in_group(brow, j) for j in range(EXPERTS_PER_GROUP)]
    au = [in_group(srow, j) for j in range(EXPERTS_PER_GROUP)]
    i1 = jnp.zeros((1, tm), I32)
    v1 = ab[0]
    for j in range(1, EXPERTS_PER_GROUP):
        better = ab[j] > v1
        i1 = jnp.where(better, j, i1)
        v1 = jnp.where(better, ab[j], v1)
    i2 = jnp.full((1, tm), -1, I32)
    v2 = jnp.full((1, tm), -jnp.inf, F32)
    for j in range(EXPERTS_PER_GROUP):
        better = (i1 != j) & ((ab[j] > v2) | (i2 < 0))
        i2 = jnp.where(better, j, i2)
        v2 = jnp.where(better, ab[j], v2)

    def pick(vals, idx):
        v = vals[0]
        for j in range(1, EXPERTS_PER_GROUP):
            v = jnp.where(idx == j, vals[j], v)
        return v

    s1, s2 = pick(au, i1), pick(au, i2)
    den = s1 + s2
    e1 = gsel * EXPERTS_PER_GROUP + i1
    e2 = gsel * EXPERTS_PER_GROUP + i2

    eid = lax.broadcasted_iota(I32, (N_EXPERTS, tm), 0)
    hit1, hit2 = eid == e1, eid == e2
    onehot = jnp.where(hit1 | hit2, 1.0, 0.0)
    before = lax.broadcasted_iota(I32, (tm, tm), 0) < lax.broadcasted_iota(I32, (tm, tm), 1)
    prefix = jnp.dot(onehot.astype(BF16), jnp.where(before, 1.0, 0.0).astype(BF16), preferred_element_type=F32)
    offset = carry_scr[...][:, 0:1] + prefix
    r1 = jnp.sum(jnp.where(hit1, offset, 0.0), axis=0, keepdims=True).astype(I32)
    r2 = jnp.sum(jnp.where(hit2, offset, 0.0), axis=0, keepdims=True).astype(I32)
    carry_scr[...] = carry_scr[...] + jnp.sum(onehot, axis=1, keepdims=True)

    row8 = lax.broadcasted_iota(I32, (8, tm), 0)
    idx_ref[...] = jnp.where(row8 == 0, e1, jnp.where(row8 == 1, e2, jnp.where(row8 == 2, r1, jnp.where(row8 == 3, r2, 0))))
    gate_ref[...] = jnp.where(row8 == 0, s1 / den, jnp.where(row8 == 1, s2 / den, 0.0))
    cnt_ref[...] = carry_scr[...]


def _router(x, rwt_bf16, bias_col):
    m = x.shape[0]
    tm = _row_tile(m, 640, LANES)
    return pl.pallas_call(
        _router_kernel,
        grid=(m // tm,),
        in_specs=[
            pl.BlockSpec((tm, D_MODEL), lambda i: (i, 0)),
            pl.BlockSpec((N_EXPERTS, D_MODEL), lambda i: (0, 0)),
            pl.BlockSpec((N_EXPERTS, 1), lambda i: (0, 0)),
        ],
        out_specs=[
            pl.BlockSpec((8, tm), lambda i: (0, i)),
            pl.BlockSpec((8, tm), lambda i: (0, i)),
            pl.BlockSpec((N_EXPERTS, LANES), lambda i: (0, 0)),
        ],
        out_shape=[
            jax.ShapeDtypeStruct((8, m), I32),
            jax.ShapeDtypeStruct((8, m), F32),
            jax.ShapeDtypeStruct((N_EXPERTS, LANES), F32),
        ],
        scratch_shapes=[pltpu.VMEM((N_EXPERTS, LANES), F32)],
        compiler_params=_params(1),
        name="router",
    )(x, rwt_bf16, bias_col)


def _start_row_gather(src_hbm, row_of, n_rows, dst, sem):
    def body(r, carry):
        pltpu.make_async_copy(src_hbm.at[pl.ds(row_of(r), 1), :], dst.at[pl.ds(r, 1), :], sem).start()
        return carry
    lax.fori_loop(0, n_rows, body, 0, unroll=8)


def _wait_row_gather(src_hbm, n_rows, dst, sem):
    pltpu.make_async_copy(src_hbm.at[pl.ds(0, n_rows), :], dst, sem).wait()


def _moe_ffn_kernel(ce_ref, tok_ref, x_hbm, wg_ref, wu_ref, wd_ref, y_ref, xbuf, sem, wgb, wub, wdb):
    c = pl.program_id(0)
    n_c = pl.num_programs(0)
    slot = c % 2

    def start(chunk, slot_):
        _start_row_gather(x_hbm, lambda r: tok_ref[chunk * MOE_CHUNK + r], MOE_CHUNK, xbuf.at[slot_], sem.at[slot_])

    @pl.when(c == 0)
    def _():
        start(0, 0)

    @pl.when(c + 1 < n_c)
    def _():
        start(c + 1, 1 - slot)

    @pl.when((c == 0) | (ce_ref[c] != ce_ref[jnp.maximum(c - 1, 0)]))
    def _():
        wgb[...] = wg_ref[...].astype(BF16)
        wub[...] = wu_ref[...].astype(BF16)
        wdb[...] = wd_ref[...].astype(BF16)

    _wait_row_gather(x_hbm, MOE_CHUNK, xbuf.at[slot], sem.at[slot])
    xb = xbuf[slot].astype(BF16)
    a = jnp.dot(xb, wgb[...], preferred_element_type=F32)
    u = jnp.dot(xb, wub[...], preferred_element_type=F32)
    h = (a / (1.0 + jnp.exp(-a))) * u
    y_ref[...] = jnp.dot(h.astype(BF16), wdb[...], preferred_element_type=F32)


def _moe_ffn(x, chunk_expert, tok_of_row, w_gate, w_up, w_down):
    n_chunks = chunk_expert.shape[0]
    grid_spec = pltpu.PrefetchScalarGridSpec(
        num_scalar_prefetch=2,
        grid=(n_chunks,),
        in_specs=[
            pl.BlockSpec(memory_space=pl.ANY),
            pl.BlockSpec((None, D_MODEL, D_EXPERT), lambda c, ce, tok: (ce[c], 0, 0)),
            pl.BlockSpec((None, D_MODEL, D_EXPERT), lambda c, ce, tok: (ce[c], 0, 0)),
            pl.BlockSpec((None, D_EXPERT, D_MODEL), lambda c, ce, tok: (ce[c], 0, 0)),
        ],
        out_specs=pl.BlockSpec((MOE_CHUNK, D_MODEL), lambda c, ce, tok: (c, 0)),
        scratch_shapes=[
            pltpu.VMEM((2, MOE_CHUNK, D_MODEL), F32),
            pltpu.SemaphoreType.DMA((2,)),
            pltpu.VMEM((D_MODEL, D_EXPERT), BF16),
            pltpu.VMEM((D_MODEL, D_EXPERT), BF16),
            pltpu.VMEM((D_EXPERT, D_MODEL), BF16),
        ],
    )
    return pl.pallas_call(
        _moe_ffn_kernel,
        grid_spec=grid_spec,
        out_shape=jax.ShapeDtypeStruct((n_chunks * MOE_CHUNK, D_MODEL), F32),
        compiler_params=_params(1),
        name="moe_ffn",
    )(chunk_expert, tok_of_row, x, w_gate, w_up, w_down)


def _combine_ln_kernel(dest_ref, y_hbm, x_ref, g1_ref, g2_ref, gain_ref, bias_ref, out_ref, ybuf, sem):
    i = pl.program_id(0)
    n_i = pl.num_programs(0)
    tm = x_ref.shape[0]
    slot = i % 2

    def start(tile, slot_):
        row_of = lambda r: dest_ref[(r // tm) * (n_i * tm) + tile * tm + r % tm]
        _start_row_gather(y_hbm, row_of, 2 * tm, ybuf.at[slot_], sem.at[slot_])

    @pl.when(i == 0)
    def _():
        start(0, 0)

    @pl.when(i + 1 < n_i)
    def _():
        start(i + 1, 1 - slot)

    _wait_row_gather(y_hbm, 2 * tm, ybuf.at[slot], sem.at[slot])
    f = g1_ref[...] * ybuf[slot, 0:tm, :] + g2_ref[...] * ybuf[slot, tm:2 * tm, :]
    out_ref[...] = _layer_norm(DEEPNORM_ALPHA * x_ref[...] + f, gain_ref[...], bias_ref[...])


def _combine_ln(x, ybuf, dest_flat, g1, g2, gain, bias):
    m = x.shape[0]
    tm = _row_tile(m, 128, 8)
    row = pl.BlockSpec((tm, D_MODEL), lambda i, d: (i, 0))
    col = pl.BlockSpec((tm, 1), lambda i, d: (i, 0))
    vec = pl.BlockSpec((1, D_MODEL), lambda i, d: (0, 0))
    grid_spec = pltpu.PrefetchScalarGridSpec(
        num_scalar_prefetch=1,
        grid=(m // tm,),
        in_specs=[pl.BlockSpec(memory_space=pl.ANY), row, col, col, vec, vec],
        out_specs=row,
        scratch_shapes=[pltpu.VMEM((2, 2 * tm, D_MODEL), F32), pltpu.SemaphoreType.DMA((2,))],
    )
    return pl.pallas_call(
        _combine_ln_kernel,
        grid_spec=grid_spec,
        out_shape=jax.ShapeDtypeStruct((m, D_MODEL), F32),
        compiler_params=_params(1),
        name="moe_combine_ln",
    )(dest_flat, ybuf, x, g1, g2, gain, bias)


def _moe_layer(x, rwt_bf16, rbias_col, w_gate, w_up, w_down, gain, bias):
    m = x.shape[0]
    idx, gates, counts = _router(x, rwt_bf16, rbias_col)
    experts, ranks = idx[0:2], idx[2:4]
    cnt = counts[:, 0].astype(I32)
    padded = (cnt + MOE_CHUNK - 1) // MOE_CHUNK * MOE_CHUNK
    pend = jnp.cumsum(padded)
    pstart = pend - padded
    eid = jnp.arange(N_EXPERTS, dtype=I32)[:, None, None]
    dest = ranks + jnp.sum(jnp.where(experts[None] == eid, pstart[:, None, None], 0), axis=0)
    n_chunks = -(-2 * m // MOE_CHUNK) + N_EXPERTS
    chunk_start = jnp.arange(n_chunks, dtype=I32) * MOE_CHUNK
    chunk_expert = jnp.minimum(jnp.sum((pend[None, :] <= chunk_start[:, None]).astype(I32), axis=1), N_EXPERTS - 1)
    tok = jnp.tile(jnp.arange(m, dtype=I32), 2)
    tok_of_row = jnp.zeros((n_chunks * MOE_CHUNK,), I32).at[dest.reshape(-1)].set(tok)
    ybuf = _moe_ffn(x, chunk_expert, tok_of_row, w_gate, w_up, w_down)
    return _combine_ln(x, ybuf, dest.reshape(-1), gates[0].reshape(m, 1), gates[1].reshape(m, 1), gain, bias)


def _block_diag_queries(q_rows, n_b, n_tok):
    q4 = q_rows.reshape(n_b, n_tok, N_HEADS, HEAD_DIM)
    tiled = jnp.tile(q4, (1, 1, 1, N_KV_HEADS))
    head = jnp.arange(N_HEADS)[:, None] // GQA
    lane_kv = jnp.arange(KV_COLS)[None, :] // HEAD_DIM
    return jnp.where(head == lane_kv, tiled, jnp.zeros((), tiled.dtype)).reshape(n_b, n_tok * N_HEADS, KV_COLS)


def _sample_rows_grouped(o, n_b, n_tok):
    o = o.reshape(n_b, n_tok, N_KV_HEADS, GQA, LANES)
    return o.transpose(0, 1, 3, 2, 4)


def _sample_out(o, n_b, n_tok):
    o = _sample_rows_grouped(o, n_b, n_tok)
    return (o[..., :HEAD_DIM] + o[..., HEAD_DIM:]).reshape(n_b * n_tok, Q_COLS)


def _sample_lse(lse, n_b, n_tok):
    return _sample_rows_grouped(lse, n_b, n_tok)[..., :HEAD_DIM].reshape(n_b * n_tok, Q_COLS)


def _new_kv_rows(kv_s, n_b, n_tok):
    return jnp.pad(kv_s.reshape(n_b, n_tok, 2 * KV_COLS), ((0, 0), (0, 8 - n_tok), (0, 0)))


def _position_minor(cache):
    lead = cache.shape[:-4]
    n = len(lead)
    t = cache.transpose(tuple(range(n)) + (n + 1, n + 2, n + 3, n))
    return t.reshape(lead + (2, KV_COLS, cache.shape[-4]))


def _position_major(kvt):
    lead = kvt.shape[:-2]
    n = len(lead)
    t = kvt.reshape(lead + (2, N_KV_HEADS, HEAD_DIM, kvt.shape[-1]))
    return t.transpose(tuple(range(n)) + (n + 3, n, n + 1, n + 2))


def _window_state(kvt, kv_s, cache, n_b, n_tok, window):
    seq = kvt.shape[-1]
    state_p = _position_major(kvt[..., seq - min(window, seq):])
    kvs = kv_s.reshape(n_b, n_tok, 2, N_KV_HEADS, HEAD_DIM)
    state_s = jnp.concatenate([cache, kvs], axis=1)[:, -cache.shape[1]:]
    return state_p, state_s


def kernel(x_prompt, x_sample, cache_swa, cache_dil0, cache_dil1, cache_dil2, cache_moba, page_table,
           w_qkv_swa, w_o_swa, sinks_swa, w_qkv_dil, w_o_dil, w_qkv_moba, w_o_moba,
           ln_gain, ln_bias, router_w, router_bias, w_gate_e, w_up_e, w_down_e):
    n_batch, seq, d = x_prompt.shape
    n_b, n_tok = x_sample.shape[:2]
    past_len = page_table.shape[1] * cache_moba.shape[2]
    mp, ms = n_batch * seq, n_b * n_tok
    m = mp + ms
    assert d == D_MODEL and cache_moba.shape[2] == PAGE_SIZE and n_tok <= 4 and ms == Q_TILE and mp % ms == 0
    x = jnp.concatenate([x_prompt.reshape(mp, d), x_sample.reshape(ms, d)], axis=0)
    pos = jnp.concatenate([jnp.tile(jnp.arange(seq), n_batch), jnp.tile(past_len + jnp.arange(n_tok), n_b)])
    cos, sin = _rope_tables(pos)
    rwt = router_w.T.astype(BF16)
    rbias = router_bias.astype(F32).reshape(N_EXPERTS, 1)
    dil_caches = (cache_dil0, cache_dil1, cache_dil2)

    def project(w, n_groups):
        w = w.astype(BF16)
        q, kvb, kvt = _qkv_rope(x, w, cos, sin, n_groups, 0, mp, seq)
        q_s, kv_s = _qkv_rope(x, w, cos, sin, n_groups, mp, ms, None)
        return q, kvb, kvt, q_s, kv_s

    def sample_local(q_s, kv_s, cache, dil, sink_col, want_lse):
        outs = _sample_attn(_block_diag_queries(q_s, n_b, n_tok), _position_minor(cache),
                            _new_kv_rows(kv_s, n_b, n_tok), dil, sink_col, want_lse)
        return _sample_out(outs[0], n_b, n_tok), (_sample_lse(outs[1], n_b, n_tok) if want_lse else None)

    swa_p, swa_s, moba_p, moba_s = [], [], [], []
    dil_p, dil_s = ([], [], []), ([], [], [])
    for i in range(DEPTH):
        kind, j = i % 3, i // 3
        gain1, bias1 = ln_gain[i, 0].reshape(1, d), ln_bias[i, 0].reshape(1, d)
        gain2, bias2 = ln_gain[i, 1].reshape(1, d), ln_bias[i, 1].reshape(1, d)
        if kind == 0:
            q, kvb, kvt, q_s, kv_s = project(w_qkv_swa[j], 1)
            sinks = sinks_swa[j].astype(F32)
            sink_rows = jnp.repeat(sinks, Q_TILE).reshape(N_KV_HEADS, GQA * Q_TILE)
            (o,) = _local_attn_prompt(q, kvb, m, 0, 1, n_batch, seq, sink_rows, False)
            sink_col = jnp.tile(sinks, n_tok).reshape(n_tok * N_HEADS, 1)
            o_s, _ = sample_local(q_s[0], kv_s[0], cache_swa[j], 1, sink_col, False)
            o = lax.dynamic_update_slice(o, o_s.astype(BF16), (mp, 0))
            sp, ss = _window_state(kvt[0], kv_s[0], cache_swa[j], n_b, n_tok, SWA_WINDOW)
            swa_p.append(sp)
            swa_s.append(ss)
            x = _oproj_ln(x, [o], None, _grouped_head_order(w_o_swa[j]).astype(BF16), gain1, bias1)
        elif kind == 1:
            q, kvb, kvt, q_s, kv_s = project(w_qkv_dil[j], len(DIL_PAIRS))
            os_, lses = [], []
            for g, (win, dil) in enumerate(DIL_PAIRS):
                o, lse = _local_attn_prompt(q, kvb, m, g, dil, n_batch, seq, None, True)
                o_s, lse_s = sample_local(q_s[g], kv_s[g], dil_caches[g][j], dil, None, True)
                os_.append(lax.dynamic_update_slice(o, o_s.astype(BF16), (mp, 0)))
                lses.append(lax.dynamic_update_slice(lse, lse_s, (mp, 0)))
                sp, ss = _window_state(kvt[g], kv_s[g], dil_caches[g][j], n_b, n_tok, win)
                dil_p[g].append(sp)
                dil_s[g].append(ss)
            x = _oproj_ln(x, os_, lses, _grouped_head_order(w_o_dil[j]).astype(BF16), gain1, bias1)
        else:
            q, kvb, kvt, q_s, kv_s = project(w_qkv_moba[j], 1)
            o = _moba_prompt(q[0], kvb[0], _block_means(kvt[0]), m, n_batch, seq)
            o_s = _moba_sample(_block_diag_queries(q_s[0], n_b, n_tok), _new_kv_rows(kv_s[0], n_b, n_tok),
                               _position_minor(cache_moba[j]), page_table)
            o = lax.dynamic_update_slice(o, _sample_out(o_s, n_b, n_tok).astype(BF16), (mp, 0))
            moba_p.append(_position_major(kvt[0]))
            moba_s.append(kv_s[0].reshape(n_b, n_tok, 2, N_KV_HEADS, HEAD_DIM))
            x = _oproj_ln(x, [o], None, _grouped_head_order(w_o_moba[j]).astype(BF16), gain1, bias1)
        x = _moe_layer(x, rwt, rbias, w_gate_e[i], w_up_e[i], w_down_e[i], gain2, bias2)
    y_prompt = x[:mp].reshape(n_batch, seq, d)
    y_sample = x[mp:].reshape(n_b, n_tok, d)
    return (y_prompt, y_sample, jnp.stack(swa_p), jnp.stack(swa_s),
            jnp.stack(dil_p[0]), jnp.stack(dil_s[0]), jnp.stack(dil_p[1]), jnp.stack(dil_s[1]),
            jnp.stack(dil_p[2]), jnp.stack(dil_s[2]), jnp.stack(moba_p), jnp.stack(moba_s))
```

```python
import functools
import math

import jax
import jax.numpy as jnp
from jax import lax
from jax.experimental import pallas as pl
from jax.experimental.pallas import tpu as pltpu

F32, BF16, I32 = jnp.float32, jnp.bfloat16, jnp.int32

D_MODEL = 2048
HEAD_DIM = 64
N_HEADS = 32
N_KV_HEADS = 8
GQA = N_HEADS // N_KV_HEADS
Q_COLS = N_HEADS * HEAD_DIM
KV_COLS = N_KV_HEADS * HEAD_DIM
QKV_COLS = Q_COLS + 2 * KV_COLS
ATTN_SCALE = HEAD_DIM ** -0.5
ROPE_THETA = 10000.0
SWA_WINDOW = 128
LOCAL_WINDOW = 128
DIL_PAIRS = ((128, 1), (512, 4), (2048, 16))
MOBA_BLOCK = 256
MOBA_TOPK = 3
PAGE_SIZE = 128
N_EXPERTS = 16
N_EXPERT_GROUPS = 4
EXPERTS_PER_GROUP = 4
D_EXPERT = D_MODEL // 4
DEPTH = 4
DEEPNORM_ALPHA = (2 * DEPTH) ** 0.25
LN_EPS = 1e-5
NEG = -1e30

LANES = 128
Q_TILE = 128
QKV_COL_TILE = 512
QKV_ROW_TILE = 512
MOE_CHUNK = 256
PAGES_PER_STEP = 8
VMEM_LIMIT = 56 * 1024 * 1024


def _params(n_axes, vmem=VMEM_LIMIT):
    return pltpu.CompilerParams(dimension_semantics=("arbitrary",) * n_axes, vmem_limit_bytes=vmem)


def _nt_dot(a, b):
    return lax.dot_general(a, b, (((1,), (1,)), ((), ())), preferred_element_type=F32)


def _row_tile(m, pref, align):
    best = m
    for t in range(align, min(pref, m) + 1, align):
        if m % t == 0:
            best = t
    return best


def _rope_tables(pos):
    half = HEAD_DIM // 2
    inv_freq = jnp.exp(jnp.arange(half, dtype=F32) * (-2.0 * math.log(ROPE_THETA) / HEAD_DIM))
    ang = pos.astype(F32)[:, None] * inv_freq[None, :]
    cos, sin = jnp.cos(ang), jnp.sin(ang)
    return jnp.tile(cos, (1, 4)), jnp.concatenate([-sin, sin, -sin, sin], axis=1)


def _qkv_rope_kernel(x_ref, w_ref, cos_ref, sin_ref, q_ref, kv_ref, *kvt_ref):
    j = pl.program_id(2)
    z = jnp.dot(x_ref[...].astype(BF16), w_ref[...], preferred_element_type=F32)
    tm = z.shape[0]

    def rope(z):
        cos, sin = cos_ref[...], sin_ref[...]
        lane = lax.broadcasted_iota(I32, (tm, LANES), 1)
        first_half = (lane & (HEAD_DIM - 1)) < HEAD_DIM // 2
        parts = []
        for c in range(z.shape[1] // LANES):
            zc = z[:, c * LANES:(c + 1) * LANES]
            other = jnp.where(first_half, pltpu.roll(zc, LANES - HEAD_DIM // 2, 1), pltpu.roll(zc, HEAD_DIM // 2, 1))
            parts.append(zc * cos + other * sin)
        return jnp.concatenate(parts, axis=1)

    n_q = Q_COLS // QKV_COL_TILE

    @pl.when(j < n_q)
    def _():
        q_ref[...] = (rope(z) * ATTN_SCALE).astype(BF16)

    def put_kv(val):
        kv_ref[...] = val.astype(kv_ref.dtype)
        if kvt_ref:
            kvt_ref[0][...] = val.T

    @pl.when(j == n_q)
    def _():
        put_kv(rope(z))

    @pl.when(j == n_q + 1)
    def _():
        put_kv(z)


def _qkv_rope(x, w_bf16, cos, sin, g0, n_groups, row0, n_rows, seq):
    tm = QKV_ROW_TILE if seq is not None else n_rows
    assert n_rows % tm == 0 and row0 % tm == 0 and (seq is None or seq % tm == 0)
    blk0 = row0 // tm
    n_q = Q_COLS // QKV_COL_TILE
    n_j = QKV_COLS // QKV_COL_TILE
    kv_col = lambda j: jnp.maximum(j - n_q, 0)
    out_specs = [
        pl.BlockSpec((None, tm, QKV_COL_TILE), lambda g, i, j: (g, i, jnp.minimum(j, n_q - 1))),
        pl.BlockSpec((None, tm, QKV_COL_TILE), lambda g, i, j: (g, i, kv_col(j))),
    ]
    out_shape = [
        jax.ShapeDtypeStruct((n_groups, n_rows, Q_COLS), BF16),
        jax.ShapeDtypeStruct((n_groups, n_rows, 2 * KV_COLS), BF16 if seq is not None else F32),
    ]
    if seq is not None:
        per = seq // tm
        out_specs.append(pl.BlockSpec((None, None, QKV_COL_TILE, tm), lambda g, i, j: (g, i // per, kv_col(j), i % per)))
        out_shape.append(jax.ShapeDtypeStruct((n_groups, n_rows // seq, 2 * KV_COLS, seq), F32))
    return pl.pallas_call(
        _qkv_rope_kernel,
        grid=(n_groups, n_rows // tm, n_j),
        in_specs=[
            pl.BlockSpec((tm, D_MODEL), lambda g, i, j: (blk0 + i, 0)),
            pl.BlockSpec((D_MODEL, QKV_COL_TILE), lambda g, i, j: (0, (g0 + g) * n_j + j)),
            pl.BlockSpec((tm, LANES), lambda g, i, j: (blk0 + i, 0)),
            pl.BlockSpec((tm, LANES), lambda g, i, j: (blk0 + i, 0)),
        ],
        out_specs=out_specs,
        out_shape=out_shape,
        compiler_params=_params(3),
        name="qkv_rope_prompt" if seq is not None else "qkv_rope_sample",
    )(x, w_bf16, cos, sin)


def _stack_q_heads(q_ref, kv):
    qa = q_ref[:, (2 * kv) * LANES:(2 * kv + 1) * LANES]
    qb = q_ref[:, (2 * kv + 1) * LANES:(2 * kv + 2) * LANES]
    return jnp.concatenate([qa[:, :HEAD_DIM], qa[:, HEAD_DIM:], qb[:, :HEAD_DIM], qb[:, HEAD_DIM:]], axis=0)


def _store_transposed_heads(ref, t_scr):
    for g in range(GQA):
        ref[:, g * KV_COLS:(g + 1) * KV_COLS] = jnp.transpose(t_scr[:, g * Q_TILE:(g + 1) * Q_TILE]).astype(ref.dtype)


def _grouped_head_order(w_o):
    return w_o.reshape(N_KV_HEADS, GQA, HEAD_DIM, -1).transpose(1, 0, 2, 3).reshape(Q_COLS, -1)


def _local_attn_kernel(*refs, has_sink, want_lse, nb, steps):
    q_ref, kc_ref, kp_ref = refs[:3]
    pos = 3
    sink_ref = None
    if has_sink:
        sink_ref = refs[pos]
        pos += 1
    o_ref = refs[pos]
    pos += 1
    lse_ref = None
    if want_lse:
        lse_ref = refs[pos]
        pos += 1
    ot_scr = refs[pos]
    lt_scr = refs[pos + 1] if want_lse else None
    t = pl.program_id(0)

    @pl.when(t >= steps)
    def _():
        o_ref[...] = jnp.zeros(o_ref.shape, o_ref.dtype)
        if want_lse:
            lse_ref[...] = jnp.zeros(lse_ref.shape, lse_ref.dtype)

    @pl.when(t < steps)
    def _():
        _local_attn_step(q_ref, kc_ref, kp_ref, sink_ref, o_ref, lse_ref, ot_scr, lt_scr, t % nb)


def _local_attn_step(q_ref, kc_ref, kp_ref, sink_ref, o_ref, lse_ref, ot_scr, lt_scr, i):
    w = LOCAL_WINDOW
    cols = GQA * Q_TILE
    ci = lax.broadcasted_iota(I32, (2 * w, cols), 0)
    qi = lax.broadcasted_iota(I32, (2 * w, cols), 1) & (Q_TILE - 1)
    delta = qi + w - ci
    mask = (delta >= 0) & (delta <= w) & ((ci >= w) | (i > 0))
    for c in range(N_KV_HEADS // 2):
        ksl = slice(c * LANES, (c + 1) * LANES)
        vsl = slice(KV_COLS + c * LANES, KV_COLS + (c + 1) * LANES)
        kpair = jnp.concatenate([kp_ref[:, ksl], kc_ref[:, ksl]], axis=0)
        vpair = jnp.concatenate([kp_ref[:, vsl], kc_ref[:, vsl]], axis=0)
        vpair_t = jnp.transpose(vpair.astype(F32)).astype(BF16)
        for half in range(2):
            kv = 2 * c + half
            hs = slice(half * HEAD_DIM, (half + 1) * HEAD_DIM)
            st = jnp.where(mask, _nt_dot(kpair[:, hs], _stack_q_heads(q_ref, kv)), NEG)
            m = jnp.max(st, axis=0, keepdims=True)
            if sink_ref is not None:
                sink = sink_ref[kv:kv + 1, :]
                m = jnp.maximum(m, sink)
            pt = jnp.exp(st - m)
            l = jnp.sum(pt, axis=0, keepdims=True)
            if sink_ref is not None:
                l = l + jnp.exp(sink - m)
            acc = jnp.dot(vpair_t[hs, :], pt.astype(BF16), preferred_element_type=F32)
            rows = slice(kv * HEAD_DIM, (kv + 1) * HEAD_DIM)
            ot_scr[rows, :] = acc / l
            if lt_scr is not None:
                lt_scr[rows, :] = jnp.broadcast_to(m + jnp.log(l), (HEAD_DIM, cols))
    _store_transposed_heads(o_ref, ot_scr)
    if lt_scr is not None:
        _store_transposed_heads(lse_ref, lt_scr)


def _local_attn_prompt(q, kvb, m, g, dil, n_batch, seq, sink_rows, want_lse):
    n_g, mp, _ = q.shape
    assert m % dil == 0 and seq % (dil * Q_TILE) == 0 and (m - mp) // dil <= Q_TILE
    nb = seq // dil // Q_TILE
    qv = q.reshape(n_g, mp // dil, dil * Q_COLS)
    kvv = kvb.reshape(n_g, mp // dil, dil * 2 * KV_COLS)
    has_sink = sink_rows is not None
    steps = n_batch * dil * nb

    def decode(t):
        tt = jnp.minimum(t, steps - 1)
        return tt // (dil * nb), (tt // nb) % dil, tt % nb

    def cur_map(t):
        n, r, i = decode(t)
        return (g, n * nb + i, r)

    def prev_map(t):
        n, r, i = decode(t)
        return (g, n * nb + jnp.maximum(i - 1, 0), r)

    def out_map(t):
        n, r, i = decode(t)
        tail = t >= steps
        return (jnp.where(tail, n_batch * nb, n * nb + i), jnp.where(tail, t - steps, r))

    in_specs = [
        pl.BlockSpec((None, Q_TILE, Q_COLS), cur_map),
        pl.BlockSpec((None, Q_TILE, 2 * KV_COLS), cur_map),
        pl.BlockSpec((None, Q_TILE, 2 * KV_COLS), prev_map),
    ]
    args = [qv, kvv, kvv]
    if has_sink:
        in_specs.append(pl.BlockSpec((N_KV_HEADS, GQA * Q_TILE), lambda t: (0, 0)))
        args.append(sink_rows)
    out_spec = pl.BlockSpec((Q_TILE, Q_COLS), out_map)
    out_specs = [out_spec]
    out_shape = [jax.ShapeDtypeStruct((m // dil, dil * Q_COLS), BF16)]
    scratch = [pltpu.VMEM((KV_COLS, GQA * Q_TILE), F32)]
    if want_lse:
        out_specs.append(out_spec)
        out_shape.append(jax.ShapeDtypeStruct((m // dil, dil * Q_COLS), F32))
        scratch.append(pltpu.VMEM((KV_COLS, GQA * Q_TILE), F32))
    outs = pl.pallas_call(
        functools.partial(_local_attn_kernel, has_sink=has_sink, want_lse=want_lse, nb=nb, steps=steps),
        grid=(steps + dil,),
        in_specs=in_specs,
        out_specs=out_specs,
        out_shape=out_shape,
        scratch_shapes=scratch,
        compiler_params=_params(1),
        name=f"local_attn_prompt_d{dil}",
    )(*args)
    return [o.reshape(m, Q_COLS) for o in outs]


def _head_block_mask(rows):
    head = lax.broadcasted_iota(I32, (rows, KV_COLS), 0) & (N_HEADS - 1)
    lane = lax.broadcasted_iota(I32, (rows, KV_COLS), 1)
    return (lane // HEAD_DIM) == (head // GQA)


def _fold_lanes(x):
    return x[:, 0:LANES] + x[:, LANES:2 * LANES] + x[:, 2 * LANES:3 * LANES] + x[:, 3 * LANES:4 * LANES]


def _padded_new_rows(n_ref):
    pad = jnp.zeros((LANES - n_ref.shape[1], KV_COLS), F32)
    knew = jnp.concatenate([n_ref[0, :, 0:KV_COLS], pad], axis=0).astype(BF16)
    vnew = jnp.concatenate([n_ref[0, :, KV_COLS:2 * KV_COLS], pad], axis=0).astype(BF16)
    return knew, vnew


def _sample_attn_kernel(*refs, dil, n_tok, has_sink, want_lse):
    q_ref, c_ref, n_ref = refs[:3]
    pos = 3
    sink_ref = None
    if has_sink:
        sink_ref = refs[pos]
        pos += 1
    o_ref = refs[pos]
    lse_ref = refs[pos + 1] if want_lse else None
    rows = n_tok * N_HEADS
    win = c_ref.shape[-1]
    q = q_ref[0]
    knew, vnew = _padded_new_rows(n_ref)
    tok_c = lax.broadcasted_iota(I32, (rows, win), 0) // N_HEADS
    pos_c = lax.broadcasted_iota(I32, (rows, win), 1)
    valid_c = (((pos_c - tok_c) & (dil - 1)) == 0) & (pos_c >= tok_c)
    tok_n = lax.broadcasted_iota(I32, (rows, LANES), 0) // N_HEADS
    pos_n = lax.broadcasted_iota(I32, (rows, LANES), 1)
    valid_n = (((tok_n - pos_n) & (dil - 1)) == 0) & (pos_n <= tok_n)
    s_c = jnp.where(valid_c, jnp.dot(q, c_ref[0, 0].astype(BF16), preferred_element_type=F32), NEG)
    s_n = jnp.where(valid_n, _nt_dot(q, knew), NEG)
    m = jnp.maximum(jnp.max(s_c, axis=1, keepdims=True), jnp.max(s_n, axis=1, keepdims=True))
    if has_sink:
        sink = sink_ref[...]
        m = jnp.maximum(m, sink)
    p_c = jnp.exp(s_c - m)
    p_n = jnp.exp(s_n - m)
    l = jnp.sum(p_c, axis=1, keepdims=True) + jnp.sum(p_n, axis=1, keepdims=True)
    if has_sink:
        l = l + jnp.exp(sink - m)
    of = (_nt_dot(p_c.astype(BF16), c_ref[0, 1].astype(BF16))
          + jnp.dot(p_n.astype(BF16), vnew, preferred_element_type=F32))
    o_ref[0] = _fold_lanes(jnp.where(_head_block_mask(rows), of, 0.0)) / l
    if want_lse:
        lse_ref[0] = jnp.broadcast_to(m + jnp.log(l), (rows, LANES))


def _sample_attn(qbd, cache_t, kv_new, dil, sink_col, want_lse):
    n_b, rows, _ = qbd.shape
    n_tok = rows // N_HEADS
    win = cache_t.shape[-1]
    assert win == LOCAL_WINDOW * dil and (dil & (dil - 1)) == 0 and n_tok <= 8
    has_sink = sink_col is not None
    in_specs = [
        pl.BlockSpec((1, rows, KV_COLS), lambda b: (b, 0, 0)),
        pl.BlockSpec((1, 2, KV_COLS, win), lambda b: (b, 0, 0, 0)),
        pl.BlockSpec((1, 8, 2 * KV_COLS), lambda b: (b, 0, 0)),
    ]
    args = [qbd, cache_t, kv_new]
    if has_sink:
        in_specs.append(pl.BlockSpec((rows, 1), lambda b: (0, 0)))
        args.append(sink_col)
    out_spec = pl.BlockSpec((1, rows, LANES), lambda b: (b, 0, 0))
    out_specs = [out_spec]
    out_shape = [jax.ShapeDtypeStruct((n_b, rows, LANES), F32)]
    if want_lse:
        out_specs.append(out_spec)
        out_shape.append(jax.ShapeDtypeStruct((n_b, rows, LANES), F32))
    return pl.pallas_call(
        functools.partial(_sample_attn_kernel, dil=dil, n_tok=n_tok, has_sink=has_sink, want_lse=want_lse),
        grid=(n_b,),
        in_specs=in_specs,
        out_specs=out_specs,
        out_shape=out_shape,
        compiler_params=_params(1),
        name=f"local_attn_sample_d{dil}",
    )(*args)


def _block_mean_kernel(k_ref, o_ref):
    n_blk = k_ref.shape[1] // MOBA_BLOCK
    cols = [jnp.mean(k_ref[:, b * MOBA_BLOCK:(b + 1) * MOBA_BLOCK], axis=1, keepdims=True) for b in range(n_blk)]
    lane = lax.broadcasted_iota(I32, (k_ref.shape[0], LANES), 1)
    out = jnp.zeros((k_ref.shape[0], LANES), F32)
    for b in range(n_blk):
        out = jnp.where(lane == b, cols[b], out)
    o_ref[...] = out


def _block_means(kvt):
    n_batch, _, seq = kvt.shape
    assert seq // MOBA_BLOCK <= LANES
    return pl.pallas_call(
        _block_mean_kernel,
        grid=(n_batch,),
        in_specs=[pl.BlockSpec((None, KV_COLS, seq), lambda n: (n, 0, 0))],
        out_specs=pl.BlockSpec((None, KV_COLS, LANES), lambda n: (n, 0, 0)),
        out_shape=jax.ShapeDtypeStruct((n_batch, KV_COLS, LANES), F32),
        compiler_params=_params(1),
        name="moba_block_means",
    )(kvt)


def _moba_prompt_kernel(q_ref, kv_ref, km_ref, o_ref, *scratch, nq, steps, n_blk):
    t = pl.program_id(0)

    @pl.when(t >= steps)
    def _():
        o_ref[...] = jnp.zeros(o_ref.shape, o_ref.dtype)

    @pl.when(t < steps)
    def _():
        _moba_prompt_step(q_ref, kv_ref, km_ref, o_ref, *scratch, t % nq, n_blk)


def _moba_prompt_step(q_ref, kv_ref, km_ref, o_ref, vt_scr, q4_scr, bias_scr, m_scr, l_scr, acc_scr, i, n_blk):
    own = (i * Q_TILE) // MOBA_BLOCK
    cols = GQA * Q_TILE

    @pl.when(i == 0)
    def _():
        for b in range(n_blk):
            for c in range(KV_COLS // LANES):
                v = kv_ref[b * MOBA_BLOCK:(b + 1) * MOBA_BLOCK, KV_COLS + c * LANES:KV_COLS + (c + 1) * LANES]
                vt_scr[c * LANES:(c + 1) * LANES, b * MOBA_BLOCK:(b + 1) * MOBA_BLOCK] = (
                    jnp.transpose(v.astype(F32)).astype(BF16))

    blk_rows = bias_scr.shape[1]
    bid = lax.broadcasted_iota(I32, (blk_rows, cols), 0)
    q_off = (i * Q_TILE) % MOBA_BLOCK + (lax.broadcasted_iota(I32, (MOBA_BLOCK, cols), 1) & (Q_TILE - 1))
    own_bias = jnp.where(lax.broadcasted_iota(I32, (MOBA_BLOCK, cols), 0) <= q_off, 0.0, NEG)
    km_t = jnp.transpose(km_ref[...])[0:blk_rows, :].astype(BF16)
    for kv in range(N_KV_HEADS):
        q4 = _stack_q_heads(q_ref, kv)
        q4_scr[kv] = q4
        gate = _nt_dot(km_t[:, kv * HEAD_DIM:(kv + 1) * HEAD_DIM], q4)
        gate = jnp.where(bid < own, gate, NEG)
        rank = jnp.zeros((blk_rows, cols), F32)
        for b in range(n_blk):
            gb = gate[b:b + 1, :]
            rank = rank + jnp.where((gb > gate) | ((gb == gate) & (b < bid)), 1.0, 0.0)
        bias_scr[kv] = jnp.where((rank < min(MOBA_TOPK, n_blk)) & (bid < own), 0.0, NEG)
    m_scr[...] = jnp.full(m_scr.shape, NEG, F32)
    l_scr[...] = jnp.zeros(l_scr.shape, F32)
    acc_scr[...] = jnp.zeros(acc_scr.shape, F32)

    def update(kv, start, bias):
        c = kv // 2
        hs = slice((kv % 2) * HEAD_DIM, (kv % 2 + 1) * HEAD_DIM)
        rows = slice(kv * HEAD_DIM, (kv + 1) * HEAD_DIM)
        k = kv_ref[pl.ds(start, MOBA_BLOCK), c * LANES:(c + 1) * LANES][:, hs]
        st = _nt_dot(k, q4_scr[kv]) + bias
        m_old = m_scr[kv, 0:1, :]
        m_new = jnp.maximum(m_old, jnp.max(st, axis=0, keepdims=True))
        pt = jnp.exp(st - m_new)
        alpha = jnp.exp(m_old - m_new)
        pv = jnp.dot(vt_scr[rows, pl.ds(start, MOBA_BLOCK)], pt.astype(BF16), preferred_element_type=F32)
        acc_scr[rows, :] = alpha * acc_scr[rows, :] + pv
        l_scr[kv, 0:1, :] = alpha * l_scr[kv, 0:1, :] + jnp.sum(pt, axis=0, keepdims=True)
        m_scr[kv, 0:1, :] = m_new

    def body(j, carry):
        start = pl.multiple_of(j * MOBA_BLOCK, MOBA_BLOCK)
        for kv in range(N_KV_HEADS):
            update(kv, start, bias_scr[kv, pl.ds(j, 1), :])
        return carry

    lax.fori_loop(0, own, body, 0)
    for kv in range(N_KV_HEADS):
        update(kv, pl.multiple_of(own * MOBA_BLOCK, MOBA_BLOCK), own_bias)
    for kv in range(N_KV_HEADS):
        rows = slice(kv * HEAD_DIM, (kv + 1) * HEAD_DIM)
        acc_scr[rows, :] = acc_scr[rows, :] / l_scr[kv, 0:1, :]
    _store_transposed_heads(o_ref, acc_scr)


def _moba_prompt(q, kvb, kmean, m, n_batch, seq):
    nq = seq // Q_TILE
    n_blk = seq // MOBA_BLOCK
    steps = n_batch * nq
    assert m - steps * Q_TILE <= Q_TILE
    batch_of = lambda t: jnp.minimum(t, steps - 1) // nq
    return pl.pallas_call(
        functools.partial(_moba_prompt_kernel, nq=nq, steps=steps, n_blk=n_blk),
        grid=(steps + 1,),
        in_specs=[
            pl.BlockSpec((Q_TILE, Q_COLS), lambda t: (jnp.minimum(t, steps - 1), 0)),
            pl.BlockSpec((seq, 2 * KV_COLS), lambda t: (batch_of(t), 0)),
            pl.BlockSpec((None, KV_COLS, LANES), lambda t: (batch_of(t), 0, 0)),
        ],
        out_specs=pl.BlockSpec((Q_TILE, Q_COLS), lambda t: (t, 0)),
        out_shape=jax.ShapeDtypeStruct((m, Q_COLS), BF16),
        scratch_shapes=[
            pltpu.VMEM((KV_COLS, seq), BF16),
            pltpu.VMEM((N_KV_HEADS, GQA * Q_TILE, HEAD_DIM), BF16),
            pltpu.VMEM((N_KV_HEADS, -(-n_blk // 8) * 8, GQA * Q_TILE), F32),
            pltpu.VMEM((N_KV_HEADS, 8, GQA * Q_TILE), F32),
            pltpu.VMEM((N_KV_HEADS, 8, GQA * Q_TILE), F32),
            pltpu.VMEM((KV_COLS, GQA * Q_TILE), F32),
        ],
        compiler_params=_params(1),
        name="moba_prompt",
    )(q, kvb, kmean)


def _top_blocks(gate, lane, k):
    chosen = jnp.zeros(gate.shape, F32)
    for _ in range(k):
        mx = jnp.max(gate, axis=1, keepdims=True)
        idx = jnp.min(jnp.where(gate == mx, lane, LANES), axis=1, keepdims=True)
        hit = lane == idx
        chosen = jnp.where(hit & (mx > 0.5 * NEG), 1.0, chosen)
        gate = jnp.where(hit, NEG, gate)
    return chosen


def _moba_sample_kernel(pt_ref, q_ref, n_ref, *refs, n_chunks, n_tok):
    k_refs = refs[:PAGES_PER_STEP]
    v_refs = refs[PAGES_PER_STEP:2 * PAGES_PER_STEP]
    o_ref, s_scr, acc_scr, l_scr, m_scr, ch_scr = refs[2 * PAGES_PER_STEP:]
    c = pl.program_id(1)
    rows = q_ref.shape[1]
    step_keys = PAGES_PER_STEP * PAGE_SIZE
    blk_per_step = step_keys // MOBA_BLOCK
    n_blk = n_chunks * blk_per_step
    lane = lax.broadcasted_iota(I32, (rows, LANES), 1)

    @pl.when(c < n_chunks)
    def _scores():
        kt = jnp.concatenate([r[...] for r in k_refs], axis=1).astype(BF16)
        s_scr[:, pl.ds(pl.multiple_of(c * step_keys, step_keys), step_keys)] = (
            jnp.dot(q_ref[0], kt, preferred_element_type=F32))

    @pl.when(c == n_chunks)
    def _select():
        gate = jnp.full((rows, LANES), NEG, F32)
        bmax = jnp.full((rows, LANES), NEG, F32)
        for b in range(n_blk):
            sb = s_scr[:, b * MOBA_BLOCK:(b + 1) * MOBA_BLOCK]
            gate = jnp.where(lane == b, jnp.sum(sb, axis=1, keepdims=True), gate)
            bmax = jnp.where(lane == b, jnp.max(sb, axis=1, keepdims=True), bmax)
        chosen = _top_blocks(gate, lane, min(MOBA_TOPK, n_blk))
        knew, vnew = _padded_new_rows(n_ref)
        tok = lax.broadcasted_iota(I32, (rows, LANES), 0) // N_HEADS
        valid = (lane <= tok) & (lane < n_tok)
        s_new = jnp.where(valid, _nt_dot(q_ref[0], knew), NEG)
        m = jnp.maximum(jnp.max(jnp.where(chosen > 0.5, bmax, NEG), axis=1, keepdims=True),
                        jnp.max(s_new, axis=1, keepdims=True))
        p_new = jnp.where(valid, jnp.exp(s_new - m), 0.0)
        acc_scr[...] = jnp.dot(p_new.astype(BF16), vnew, preferred_element_type=F32)
        l_scr[...] = jnp.sum(p_new, axis=1, keepdims=True)
        m_scr[...] = m
        ch_scr[...] = chosen

    @pl.when(c >= n_chunks)
    def _accumulate():
        cc = c - n_chunks
        vt = jnp.concatenate([r[...] for r in v_refs], axis=1).astype(BF16)
        s = s_scr[:, pl.ds(pl.multiple_of(cc * step_keys, step_keys), step_keys)]
        chosen = ch_scr[...]
        keep = jnp.concatenate(
            [jnp.broadcast_to(jnp.max(jnp.where(lane == cc * blk_per_step + u, chosen, 0.0), axis=1, keepdims=True) > 0.5,
                              (rows, MOBA_BLOCK)) for u in range(blk_per_step)], axis=1)
        p = jnp.where(keep, jnp.exp(s - m_scr[...]), 0.0)
        l_scr[...] += jnp.sum(p, axis=1, keepdims=True)
        acc_scr[...] += _nt_dot(p.astype(BF16), vt)

    @pl.when(c == 2 * n_chunks - 1)
    def _finish():
        o_ref[0] = _fold_lanes(jnp.where(_head_block_mask(rows), acc_scr[...], 0.0)) / l_scr[...]


def _moba_sample(qbd, kv_new, cache_t, page_table):
    n_b, rows, _ = qbd.shape
    n_tok = rows // N_HEADS
    n_pages = page_table.shape[1]
    assert n_pages % PAGES_PER_STEP == 0 and (n_pages * PAGE_SIZE) % MOBA_BLOCK == 0
    assert n_pages * PAGE_SIZE // MOBA_BLOCK <= LANES
    n_chunks = n_pages // PAGES_PER_STEP

    def k_map(u):
        return lambda b, c, pt: (pt[b, jnp.minimum(c, n_chunks - 1) * PAGES_PER_STEP + u], 0, 0, 0)

    def v_map(u):
        return lambda b, c, pt: (pt[b, jnp.maximum(c - n_chunks, 0) * PAGES_PER_STEP + u], 1, 0, 0)

    page_specs = ([pl.BlockSpec((None, None, KV_COLS, PAGE_SIZE), k_map(u)) for u in range(PAGES_PER_STEP)]
                  + [pl.BlockSpec((None, None, KV_COLS, PAGE_SIZE), v_map(u)) for u in range(PAGES_PER_STEP)])
    grid_spec = pltpu.PrefetchScalarGridSpec(
        num_scalar_prefetch=1,
        grid=(n_b, 2 * n_chunks),
        in_specs=[
            pl.BlockSpec((1, rows, KV_COLS), lambda b, c, pt: (b, 0, 0)),
            pl.BlockSpec((1, 8, 2 * KV_COLS), lambda b, c, pt: (b, 0, 0)),
        ] + page_specs,
        out_specs=pl.BlockSpec((1, rows, LANES), lambda b, c, pt: (b, 0, 0)),
        scratch_shapes=[
            pltpu.VMEM((rows, n_pages * PAGE_SIZE), F32),
            pltpu.VMEM((rows, KV_COLS), F32),
            pltpu.VMEM((rows, 1), F32),
            pltpu.VMEM((rows, 1), F32),
            pltpu.VMEM((rows, LANES), F32),
        ],
    )
    return pl.pallas_call(
        functools.partial(_moba_sample_kernel, n_chunks=n_chunks, n_tok=n_tok),
        grid_spec=grid_spec,
        out_shape=jax.ShapeDtypeStruct((n_b, rows, LANES), F32),
        compiler_params=_params(2),
        name="moba_sample",
    )(page_table, qbd, kv_new, *([cache_t] * (2 * PAGES_PER_STEP)))


def _layer_norm(h, gain, bias):
    mu = jnp.mean(h, axis=-1, keepdims=True)
    d = h - mu
    var = jnp.mean(d * d, axis=-1, keepdims=True)
    return d * lax.rsqrt(var + LN_EPS) * gain + bias


def _oproj_ln_kernel(*refs, n_merge):
    x_ref = refs[0]
    o_refs = refs[1:1 + n_merge]
    w_ref, g_ref, b_ref, out_ref = refs[-4:]
    if n_merge == 1:
        o = o_refs[0][...]
    else:
        lses = [r[...] for r in refs[1 + n_merge:1 + 2 * n_merge]]
        mx = functools.reduce(jnp.maximum, lses)
        es = [jnp.exp(l - mx) for l in lses]
        num = sum(e * r[...].astype(F32) for r, e in zip(o_refs, es))
        o = (num / sum(es)).astype(BF16)
    y = jnp.dot(o, w_ref[...], preferred_element_type=F32)
    out_ref[...] = _layer_norm(DEEPNORM_ALPHA * x_ref[...] + y, g_ref[...], b_ref[...])


def _oproj_ln(x, o, lse, w_bf16, gain, bias):
    m = x.shape[0]
    n_merge = len(o)
    tm = _row_tile(m, 320 if n_merge == 1 else 160, 16)
    row = pl.BlockSpec((tm, D_MODEL), lambda i: (i, 0))
    vec = pl.BlockSpec((1, D_MODEL), lambda i: (0, 0))
    wspec = pl.BlockSpec((Q_COLS, D_MODEL), lambda i: (0, 0))
    groups = list(o) + (list(lse) if n_merge > 1 else [])
    in_specs = [row] + [pl.BlockSpec((tm, Q_COLS), lambda i: (i, 0))] * len(groups) + [wspec, vec, vec]
    args = [x] + groups + [w_bf16, gain, bias]
    return pl.pallas_call(
        functools.partial(_oproj_ln_kernel, n_merge=n_merge),
        grid=(m // tm,),
        in_specs=in_specs,
        out_specs=row,
        out_shape=jax.ShapeDtypeStruct((m, D_MODEL), F32),
        compiler_params=_params(1),
        name=f"oproj_ln_m{n_merge}",
    )(*args)


def _router_kernel(x_ref, rwt_ref, bias_ref, idx_ref, gate_ref, cnt_ref, carry_scr):
    i = pl.program_id(0)
    tm = x_ref.shape[0]

    @pl.when(i == 0)
    def _():
        carry_scr[...] = jnp.zeros(carry_scr.shape, F32)

    logits = _nt_dot(rwt_ref[...], x_ref[...].astype(BF16))
    scores = 1.0 / (1.0 + jnp.exp(-logits))
    biased = scores + bias_ref[...]
    srow = [scores[e:e + 1, :] for e in range(N_EXPERTS)]
    brow = [biased[e:e + 1, :] for e in range(N_EXPERTS)]

    gscore = []
    for g in range(N_EXPERT_GROUPS):
        a0, a1, a2, a3 = brow[4 * g:4 * g + 4]
        hi01, lo01 = jnp.maximum(a0, a1), jnp.minimum(a0, a1)
        hi23, lo23 = jnp.maximum(a2, a3), jnp.minimum(a2, a3)
        gscore.append(jnp.maximum(hi01, hi23) + jnp.maximum(jnp.minimum(hi01, hi23), jnp.maximum(lo01, lo23)))
    gsel = jnp.zeros((1, tm), I32)
    best = gscore[0]
    for g in range(1, N_EXPERT_GROUPS):
        better = gscore[g] > best
        gsel = jnp.where(better, g, gsel)
        best = jnp.where(better, gscore[g], best)

    def in_group(rows_, j):
        v = rows_[j]
        for g in range(1, N_EXPERT_GROUPS):
            v = jnp.where(gsel == g, rows_[4 * g + j], v)
        return v

    ab = [in_group(brow, j) for j in range(EXPERTS_PER_GROUP)]
    au = [in_group(srow, j) for j in range(EXPERTS_PER_GROUP)]
    i1 = jnp.zeros((1, tm), I32)
    v1 = ab[0]
    for j in range(1, EXPERTS_PER_GROUP):
        better = ab[j] > v1
        i1 = jnp.where(better, j, i1)
        v1 = jnp.where(better, ab[j], v1)
    i2 = jnp.full((1, tm), -1, I32)
    v2 = jnp.full((1, tm), -jnp.inf, F32)
    for j in range(EXPERTS_PER_GROUP):
        better = (i1 != j) & ((ab[j] > v2) | (i2 < 0))
        i2 = jnp.where(better, j, i2)
        v2 = jnp.where(better, ab[j], v2)

    def pick(vals, idx):
        v = vals[0]
        for j in range(1, EXPERTS_PER_GROUP):
            v = jnp.where(idx == j, vals[j], v)
        return v

    s1, s2 = pick(au, i1), pick(au, i2)
    den = s1 + s2
    e1 = gsel * EXPERTS_PER_GROUP + i1
    e2 = gsel * EXPERTS_PER_GROUP + i2

    eid = lax.broadcasted_iota(I32, (N_EXPERTS, tm), 0)
    hit1, hit2 = eid == e1, eid == e2
    onehot = jnp.where(hit1 | hit2, 1.0, 0.0)
    before = lax.broadcasted_iota(I32, (tm, tm), 0) < lax.broadcasted_iota(I32, (tm, tm), 1)
    prefix = jnp.dot(onehot.astype(BF16), jnp.where(before, 1.0, 0.0).astype(BF16), preferred_element_type=F32)
    offset = carry_scr[...][:, 0:1] + prefix
    r1 = jnp.sum(jnp.where(hit1, offset, 0.0), axis=0, keepdims=True).astype(I32)
    r2 = jnp.sum(jnp.where(hit2, offset, 0.0), axis=0, keepdims=True).astype(I32)
    carry_scr[...] = carry_scr[...] + jnp.sum(onehot, axis=1, keepdims=True)

    row8 = lax.broadcasted_iota(I32, (8, tm), 0)
    idx_ref[...] = jnp.where(row8 == 0, e1, jnp.where(row8 == 1, e2, jnp.where(row8 == 2, r1, jnp.where(row8 == 3, r2, 0))))
    gate_ref[...] = jnp.where(row8 == 0, s1 / den, jnp.where(row8 == 1, s2 / den, 0.0))
    cnt_ref[...] = carry_scr[...]


def _router(x, rwt_bf16, bias_col):
    m = x.shape[0]
    tm = _row_tile(m, 640, LANES)
    return pl.pallas_call(
        _router_kernel,
        grid=(m // tm,),
        in_specs=[
            pl.BlockSpec((tm, D_MODEL), lambda i: (i, 0)),
            pl.BlockSpec((N_EXPERTS, D_MODEL), lambda i: (0, 0)),
            pl.BlockSpec((N_EXPERTS, 1), lambda i: (0, 0)),
        ],
        out_specs=[
            pl.BlockSpec((8, tm), lambda i: (0, i)),
            pl.BlockSpec((8, tm), lambda i: (0, i)),
            pl.BlockSpec((N_EXPERTS, LANES), lambda i: (0, 0)),
        ],
        out_shape=[
            jax.ShapeDtypeStruct((8, m), I32),
            jax.ShapeDtypeStruct((8, m), F32),
            jax.ShapeDtypeStruct((N_EXPERTS, LANES), F32),
        ],
        scratch_shapes=[pltpu.VMEM((N_EXPERTS, LANES), F32)],
        compiler_params=_params(1),
        name="router",
    )(x, rwt_bf16, bias_col)


def _start_row_gather(src_hbm, row_of, n_rows, dst, sem):
    def body(r, carry):
        pltpu.make_async_copy(src_hbm.at[pl.ds(row_of(r), 1), :], dst.at[pl.ds(r, 1), :], sem).start()
        return carry
    lax.fori_loop(0, n_rows, body, 0, unroll=8)


def _wait_row_gather(src_hbm, n_rows, dst, sem):
    pltpu.make_async_copy(src_hbm.at[pl.ds(0, n_rows), :], dst, sem).wait()


def _start_row_gather_inline(src_hbm, idx_ref, base, n_rows, dst, sem):
    for r in range(n_rows):
        pltpu.make_async_copy(src_hbm.at[pl.ds(idx_ref[base + r], 1), :], dst.at[pl.ds(r, 1), :], sem).start()


def _moe_ffn_kernel(ce_ref, tok_ref, x_hbm, wg_ref, wu_ref, wd_ref, y_ref, xbuf0, xbuf1, sem, wgb, wub, wdb, *,
                    n_chunks):
    c = pl.program_id(0)
    bufs = (xbuf0, xbuf1)

    @pl.when(c == 0)
    def _():
        _start_row_gather(x_hbm, lambda r: tok_ref[r], MOE_CHUNK, xbuf0, sem.at[0])

    @pl.when((c == 0) | (ce_ref[c] != ce_ref[jnp.maximum(c - 1, 0)]))
    def _():
        wgb[...] = wg_ref[...].astype(BF16)
        wub[...] = wu_ref[...].astype(BF16)
        wdb[...] = wd_ref[...].astype(BF16)

    def run(cur):
        _wait_row_gather(x_hbm, MOE_CHUNK, bufs[cur], sem.at[cur])
        _start_row_gather_inline(x_hbm, tok_ref, (c + 1) * MOE_CHUNK, MOE_CHUNK, bufs[1 - cur], sem.at[1 - cur])
        xb = bufs[cur][...].astype(BF16)
        a = jnp.dot(xb, wgb[...], preferred_element_type=F32)
        u = jnp.dot(xb, wub[...], preferred_element_type=F32)
        h = (a / (1.0 + jnp.exp(-a))) * u
        y_ref[...] = jnp.dot(h.astype(BF16), wdb[...], preferred_element_type=F32)

    @pl.when(c % 2 == 0)
    def _():
        run(0)

    @pl.when(c % 2 == 1)
    def _():
        run(1)

    @pl.when(c == n_chunks - 1)
    def _():
        _wait_row_gather(x_hbm, MOE_CHUNK, bufs[n_chunks % 2], sem.at[n_chunks % 2])


def _moe_ffn(x, chunk_expert, tok_of_row, w_gate, w_up, w_down, layer):
    n_chunks = chunk_expert.shape[0]
    assert tok_of_row.shape[0] == (n_chunks + 1) * MOE_CHUNK
    w_in = pl.BlockSpec((None, None, D_MODEL, D_EXPERT), lambda c, ce, tok: (layer, ce[c], 0, 0))
    grid_spec = pltpu.PrefetchScalarGridSpec(
        num_scalar_prefetch=2,
        grid=(n_chunks,),
        in_specs=[
            pl.BlockSpec(memory_space=pl.ANY),
            w_in,
            w_in,
            pl.BlockSpec((None, None, D_EXPERT, D_MODEL), lambda c, ce, tok: (layer, ce[c], 0, 0)),
        ],
        out_specs=pl.BlockSpec((MOE_CHUNK, D_MODEL), lambda c, ce, tok: (c, 0)),
        scratch_shapes=[
            pltpu.VMEM((MOE_CHUNK, D_MODEL), F32),
            pltpu.VMEM((MOE_CHUNK, D_MODEL), F32),
            pltpu.SemaphoreType.DMA((2,)),
            pltpu.VMEM((D_MODEL, D_EXPERT), BF16),
            pltpu.VMEM((D_MODEL, D_EXPERT), BF16),
            pltpu.VMEM((D_EXPERT, D_MODEL), BF16),
        ],
    )
    return pl.pallas_call(
        functools.partial(_moe_ffn_kernel, n_chunks=n_chunks),
        grid_spec=grid_spec,
        out_shape=jax.ShapeDtypeStruct((n_chunks * MOE_CHUNK, D_MODEL), F32),
        compiler_params=_params(1),
        name="moe_ffn",
    )(chunk_expert, tok_of_row, x, w_gate, w_up, w_down)


def _combine_ln_kernel(dest_ref, y_hbm, x_ref, g1_ref, g2_ref, gain_ref, bias_ref, out_ref, ybuf0, ybuf1, sem, *,
                       n_tiles):
    i = pl.program_id(0)
    tm = x_ref.shape[0]
    bufs = (ybuf0, ybuf1)

    @pl.when(i == 0)
    def _():
        _start_row_gather(y_hbm, lambda r: dest_ref[r], 2 * tm, ybuf0, sem.at[0])

    def run(cur):
        _wait_row_gather(y_hbm, 2 * tm, bufs[cur], sem.at[cur])
        _start_row_gather_inline(y_hbm, dest_ref, (i + 1) * 2 * tm, 2 * tm, bufs[1 - cur], sem.at[1 - cur])
        f = g1_ref[...] * bufs[cur][0:tm, :] + g2_ref[...] * bufs[cur][tm:2 * tm, :]
        out_ref[...] = _layer_norm(DEEPNORM_ALPHA * x_ref[...] + f, gain_ref[...], bias_ref[...])

    @pl.when(i % 2 == 0)
    def _():
        run(0)

    @pl.when(i % 2 == 1)
    def _():
        run(1)

    @pl.when(i == n_tiles - 1)
    def _():
        _wait_row_gather(y_hbm, 2 * tm, bufs[n_tiles % 2], sem.at[n_tiles % 2])


def _combine_ln(x, ybuf, dest, g1, g2, gain, bias):
    m = x.shape[0]
    tm = _row_tile(m, 128, 8)
    n_tiles = m // tm
    dest_tiles = dest.reshape(2, n_tiles, tm).transpose(1, 0, 2).reshape(-1)
    dest_tiles = jnp.concatenate([dest_tiles, jnp.zeros((2 * tm,), I32)])
    row = pl.BlockSpec((tm, D_MODEL), lambda i, d: (i, 0))
    col = pl.BlockSpec((tm, 1), lambda i, d: (i, 0))
    vec = pl.BlockSpec((1, D_MODEL), lambda i, d: (0, 0))
    grid_spec = pltpu.PrefetchScalarGridSpec(
        num_scalar_prefetch=1,
        grid=(n_tiles,),
        in_specs=[pl.BlockSpec(memory_space=pl.ANY), row, col, col, vec, vec],
        out_specs=row,
        scratch_shapes=[pltpu.VMEM((2 * tm, D_MODEL), F32), pltpu.VMEM((2 * tm, D_MODEL), F32),
                        pltpu.SemaphoreType.DMA((2,))],
    )
    return pl.pallas_call(
        functools.partial(_combine_ln_kernel, n_tiles=n_tiles),
        grid_spec=grid_spec,
        out_shape=jax.ShapeDtypeStruct((m, D_MODEL), F32),
        compiler_params=_params(1),
        name="moe_combine_ln",
    )(dest_tiles, ybuf, x, g1, g2, gain, bias)


def _moe_layer(x, rwt_bf16, rbias_col, w_gate, w_up, w_down, layer, gain, bias):
    m = x.shape[0]
    idx, gates, counts = _router(x, rwt_bf16, rbias_col)
    experts, ranks = idx[0:2], idx[2:4]
    cnt = counts[:, 0].astype(I32)
    padded = (cnt + MOE_CHUNK - 1) // MOE_CHUNK * MOE_CHUNK
    pend = jnp.cumsum(padded)
    pstart = pend - padded
    eid = jnp.arange(N_EXPERTS, dtype=I32)[:, None, None]
    dest = ranks + jnp.sum(jnp.where(experts[None] == eid, pstart[:, None, None], 0), axis=0)
    n_chunks = -(-2 * m // MOE_CHUNK) + N_EXPERTS
    chunk_start = jnp.arange(n_chunks, dtype=I32) * MOE_CHUNK
    chunk_expert = jnp.minimum(jnp.sum((pend[None, :] <= chunk_start[:, None]).astype(I32), axis=1), N_EXPERTS - 1)
    tok = jnp.tile(jnp.arange(m, dtype=I32), 2)
    tok_of_row = jnp.zeros(((n_chunks + 1) * MOE_CHUNK,), I32).at[dest.reshape(-1)].set(tok)
    ybuf = _moe_ffn(x, chunk_expert, tok_of_row, w_gate, w_up, w_down, layer)
    return _combine_ln(x, ybuf, dest, gates[0].reshape(m, 1), gates[1].reshape(m, 1), gain, bias)


def _block_diag_queries(q_rows, n_b, n_tok):
    q4 = q_rows.reshape(n_b, n_tok, N_HEADS, HEAD_DIM)
    tiled = jnp.tile(q4, (1, 1, 1, N_KV_HEADS))
    head = jnp.arange(N_HEADS)[:, None] // GQA
    lane_kv = jnp.arange(KV_COLS)[None, :] // HEAD_DIM
    return jnp.where(head == lane_kv, tiled, jnp.zeros((), tiled.dtype)).reshape(n_b, n_tok * N_HEADS, KV_COLS)


def _sample_rows_grouped(o, n_b, n_tok):
    o = o.reshape(n_b, n_tok, N_KV_HEADS, GQA, LANES)
    return o.transpose(0, 1, 3, 2, 4)


def _sample_out(o, n_b, n_tok):
    o = _sample_rows_grouped(o, n_b, n_tok)
    return (o[..., :HEAD_DIM] + o[..., HEAD_DIM:]).reshape(n_b * n_tok, Q_COLS)


def _sample_lse(lse, n_b, n_tok):
    return _sample_rows_grouped(lse, n_b, n_tok)[..., :HEAD_DIM].reshape(n_b * n_tok, Q_COLS)


def _new_kv_rows(kv_s, n_b, n_tok):
    return jnp.pad(kv_s.reshape(n_b, n_tok, 2 * KV_COLS), ((0, 0), (0, 8 - n_tok), (0, 0)))


def _position_minor(cache):
    lead = cache.shape[:-4]
    n = len(lead)
    t = cache.transpose(tuple(range(n)) + (n + 1, n + 2, n + 3, n))
    return t.reshape(lead + (2, KV_COLS, cache.shape[-4]))


def _position_major(kvt):
    lead = kvt.shape[:-2]
    n = len(lead)
    t = kvt.reshape(lead + (2, N_KV_HEADS, HEAD_DIM, kvt.shape[-1]))
    return t.transpose(tuple(range(n)) + (n + 3, n, n + 1, n + 2))


def _window_state(kvt, kv_s, cache, n_b, n_tok, window):
    seq = kvt.shape[-1]
    state_p = _position_major(kvt[..., seq - min(window, seq):])
    kvs = kv_s.reshape(n_b, n_tok, 2, N_KV_HEADS, HEAD_DIM)
    state_s = jnp.concatenate([cache, kvs], axis=1)[:, -cache.shape[1]:]
    return state_p, state_s


def kernel(x_prompt, x_sample, cache_swa, cache_dil0, cache_dil1, cache_dil2, cache_moba, page_table,
           w_qkv_swa, w_o_swa, sinks_swa, w_qkv_dil, w_o_dil, w_qkv_moba, w_o_moba,
           ln_gain, ln_bias, router_w, router_bias, w_gate_e, w_up_e, w_down_e):
    n_batch, seq, d = x_prompt.shape
    n_b, n_tok = x_sample.shape[:2]
    past_len = page_table.shape[1] * cache_moba.shape[2]
    mp, ms = n_batch * seq, n_b * n_tok
    m = mp + ms
    assert d == D_MODEL and cache_moba.shape[2] == PAGE_SIZE and n_tok <= 4 and ms == Q_TILE and mp % ms == 0
    x = jnp.concatenate([x_prompt.reshape(mp, d), x_sample.reshape(ms, d)], axis=0)
    pos = jnp.concatenate([jnp.tile(jnp.arange(seq), n_batch), jnp.tile(past_len + jnp.arange(n_tok), n_b)])
    cos, sin = _rope_tables(pos)
    rwt = router_w.T.astype(BF16)
    rbias = router_bias.astype(F32).reshape(N_EXPERTS, 1)
    dil_caches = (cache_dil0, cache_dil1, cache_dil2)

    def project(w_bf16, g):
        q, kvb, kvt = _qkv_rope(x, w_bf16, cos, sin, g, 1, 0, mp, seq)
        q_s, kv_s = _qkv_rope(x, w_bf16, cos, sin, g, 1, mp, ms, None)
        return q, kvb, kvt, q_s, kv_s

    def sample_local(q_s, kv_s, cache, dil, sink_col, want_lse):
        outs = _sample_attn(_block_diag_queries(q_s, n_b, n_tok), _position_minor(cache),
                            _new_kv_rows(kv_s, n_b, n_tok), dil, sink_col, want_lse)
        return _sample_out(outs[0], n_b, n_tok), (_sample_lse(outs[1], n_b, n_tok) if want_lse else None)

    swa_p, swa_s, moba_p, moba_s = [], [], [], []
    dil_p, dil_s = ([], [], []), ([], [], [])
    for i in range(DEPTH):
        kind, j = i % 3, i // 3
        gain1, bias1 = ln_gain[i, 0].reshape(1, d), ln_bias[i, 0].reshape(1, d)
        gain2, bias2 = ln_gain[i, 1].reshape(1, d), ln_bias[i, 1].reshape(1, d)
        if kind == 0:
            q, kvb, kvt, q_s, kv_s = project(w_qkv_swa[j].astype(BF16), 0)
            sinks = sinks_swa[j].astype(F32)
            sink_rows = jnp.repeat(sinks, Q_TILE).reshape(N_KV_HEADS, GQA * Q_TILE)
            (o,) = _local_attn_prompt(q, kvb, m, 0, 1, n_batch, seq, sink_rows, False)
            sink_col = jnp.tile(sinks, n_tok).reshape(n_tok * N_HEADS, 1)
            o_s, _ = sample_local(q_s[0], kv_s[0], cache_swa[j], 1, sink_col, False)
            o = lax.dynamic_update_slice(o, o_s.astype(BF16), (mp, 0))
            sp, ss = _window_state(kvt[0], kv_s[0], cache_swa[j], n_b, n_tok, SWA_WINDOW)
            swa_p.append(sp)
            swa_s.append(ss)
            x = _oproj_ln(x, [o], None, _grouped_head_order(w_o_swa[j]).astype(BF16), gain1, bias1)
        elif kind == 1:
            w_dil = w_qkv_dil[j].astype(BF16)
            os_, lses = [], []
            for g, (win, dil) in enumerate(DIL_PAIRS):
                q, kvb, kvt, q_s, kv_s = project(w_dil, g)
                o, lse = _local_attn_prompt(q, kvb, m, 0, dil, n_batch, seq, None, True)
                o_s, lse_s = sample_local(q_s[0], kv_s[0], dil_caches[g][j], dil, None, True)
                os_.append(lax.dynamic_update_slice(o, o_s.astype(BF16), (mp, 0)))
                lses.append(lax.dynamic_update_slice(lse, lse_s, (mp, 0)))
                sp, ss = _window_state(kvt[0], kv_s[0], dil_caches[g][j], n_b, n_tok, win)
                dil_p[g].append(sp)
                dil_s[g].append(ss)
            x = _oproj_ln(x, os_, lses, _grouped_head_order(w_o_dil[j]).astype(BF16), gain1, bias1)
        else:
            q, kvb, kvt, q_s, kv_s = project(w_qkv_moba[j].astype(BF16), 0)
            o = _moba_prompt(q[0], kvb[0], _block_means(kvt[0]), m, n_batch, seq)
            o_s = _moba_sample(_block_diag_queries(q_s[0], n_b, n_tok), _new_kv_rows(kv_s[0], n_b, n_tok),
                               _position_minor(cache_moba[j]), page_table)
            o = lax.dynamic_update_slice(o, _sample_out(o_s, n_b, n_tok).astype(BF16), (mp, 0))
            moba_p.append(_position_major(kvt[0]))
            moba_s.append(kv_s[0].reshape(n_b, n_tok, 2, N_KV_HEADS, HEAD_DIM))
            x = _oproj_ln(x, [o], None, _grouped_head_order(w_o_moba[j]).astype(BF16), gain1, bias1)
        x = _moe_layer(x, rwt, rbias, w_gate_e, w_up_e, w_down_e, i, gain2, bias2)
    y_prompt = x[:mp].reshape(n_batch, seq, d)
    y_sample = x[mp:].reshape(n_b, n_tok, d)
    return (y_prompt, y_sample, jnp.stack(swa_p), jnp.stack(swa_s),
            jnp.stack(dil_p[0]), jnp.stack(dil_s[0]), jnp.stack(dil_p[1]), jnp.stack(dil_s[1]),
            jnp.stack(dil_p[2]), jnp.stack(dil_s[2]), jnp.stack(moba_p), jnp.stack(moba_s))
```

```python
import functools
import math

import jax
import jax.numpy as jnp
from jax import lax
from jax.experimental import pallas as pl
from jax.experimental.pallas import tpu as pltpu

F32, BF16, I32 = jnp.float32, jnp.bfloat16, jnp.int32

D_MODEL = 2048
HEAD_DIM = 64
N_HEADS = 32
N_KV_HEADS = 8
GQA = N_HEADS // N_KV_HEADS
Q_COLS = N_HEADS * HEAD_DIM
KV_COLS = N_KV_HEADS * HEAD_DIM
QKV_COLS = Q_COLS + 2 * KV_COLS
ATTN_SCALE = HEAD_DIM ** -0.5
ROPE_THETA = 10000.0
SWA_WINDOW = 128
LOCAL_WINDOW = 128
DIL_PAIRS = ((128, 1), (512, 4), (2048, 16))
MOBA_BLOCK = 256
MOBA_TOPK = 3
PAGE_SIZE = 128
N_EXPERTS = 16
N_EXPERT_GROUPS = 4
EXPERTS_PER_GROUP = 4
D_EXPERT = D_MODEL // 4
DEPTH = 4
DEEPNORM_ALPHA = (2 * DEPTH) ** 0.25
LN_EPS = 1e-5
NEG = -1e30

LANES = 128
Q_TILE = 128
QKV_COL_TILE = 512
QKV_ROW_TILE = 512
MOE_CHUNK = 256
PAGES_PER_STEP = 16
VMEM_LIMIT = 56 * 1024 * 1024


def _params(n_axes, vmem=VMEM_LIMIT):
    return pltpu.CompilerParams(dimension_semantics=("arbitrary",) * n_axes, vmem_limit_bytes=vmem)


def _nt_dot(a, b):
    return lax.dot_general(a, b, (((1,), (1,)), ((), ())), preferred_element_type=F32)


def _row_tile(m, pref, align):
    best = m
    for t in range(align, min(pref, m) + 1, align):
        if m % t == 0:
            best = t
    return best


def _rope_tables(pos):
    half = HEAD_DIM // 2
    inv_freq = jnp.exp(jnp.arange(half, dtype=F32) * (-2.0 * math.log(ROPE_THETA) / HEAD_DIM))
    ang = pos.astype(F32)[:, None] * inv_freq[None, :]
    cos, sin = jnp.cos(ang), jnp.sin(ang)
    return jnp.tile(cos, (1, 4)), jnp.concatenate([-sin, sin, -sin, sin], axis=1)


def _qkv_rope_kernel(x_ref, w_ref, cos_ref, sin_ref, q_ref, kv_ref, *kvt_ref):
    xb = x_ref[...].astype(BF16)
    tm = xb.shape[0]
    lane = lax.broadcasted_iota(I32, (tm, LANES), 1)
    first_half = (lane & (HEAD_DIM - 1)) < HEAD_DIM // 2

    def rope(z):
        cos, sin = cos_ref[...], sin_ref[...]
        parts = []
        for c in range(z.shape[1] // LANES):
            zc = z[:, c * LANES:(c + 1) * LANES]
            other = jnp.where(first_half, pltpu.roll(zc, LANES - HEAD_DIM // 2, 1), pltpu.roll(zc, HEAD_DIM // 2, 1))
            parts.append(zc * cos + other * sin)
        return jnp.concatenate(parts, axis=1)

    n_q = Q_COLS // QKV_COL_TILE
    for j in range(QKV_COLS // QKV_COL_TILE):
        cols = slice(j * QKV_COL_TILE, (j + 1) * QKV_COL_TILE)
        z = jnp.dot(xb, w_ref[:, cols], preferred_element_type=F32)
        if j < n_q:
            q_ref[:, cols] = (rope(z) * ATTN_SCALE).astype(BF16)
        else:
            val = rope(z) if j == n_q else z
            kcols = slice((j - n_q) * QKV_COL_TILE, (j - n_q + 1) * QKV_COL_TILE)
            kv_ref[:, kcols] = val.astype(kv_ref.dtype)
            if kvt_ref:
                kvt_ref[0][kcols, :] = val.T


def _qkv_rope(x, w_bf16, cos, sin, g0, n_groups, row0, n_rows, seq):
    tm = QKV_ROW_TILE if seq is not None else n_rows
    assert n_rows % tm == 0 and row0 % tm == 0 and (seq is None or seq % tm == 0)
    blk0 = row0 // tm
    out_specs = [
        pl.BlockSpec((None, tm, Q_COLS), lambda g, i: (g, i, 0)),
        pl.BlockSpec((None, tm, 2 * KV_COLS), lambda g, i: (g, i, 0)),
    ]
    out_shape = [
        jax.ShapeDtypeStruct((n_groups, n_rows, Q_COLS), BF16),
        jax.ShapeDtypeStruct((n_groups, n_rows, 2 * KV_COLS), BF16 if seq is not None else F32),
    ]
    if seq is not None:
        per = seq // tm
        out_specs.append(pl.BlockSpec((None, None, 2 * KV_COLS, tm), lambda g, i: (g, i // per, 0, i % per)))
        out_shape.append(jax.ShapeDtypeStruct((n_groups, n_rows // seq, 2 * KV_COLS, seq), F32))
    return pl.pallas_call(
        _qkv_rope_kernel,
        grid=(n_groups, n_rows // tm),
        in_specs=[
            pl.BlockSpec((tm, D_MODEL), lambda g, i: (blk0 + i, 0)),
            pl.BlockSpec((D_MODEL, QKV_COLS), lambda g, i: (0, g0 + g)),
            pl.BlockSpec((tm, LANES), lambda g, i: (blk0 + i, 0)),
            pl.BlockSpec((tm, LANES), lambda g, i: (blk0 + i, 0)),
        ],
        out_specs=out_specs,
        out_shape=out_shape,
        compiler_params=_params(2),
        name="qkv_rope_prompt" if seq is not None else "qkv_rope_sample",
    )(x, w_bf16, cos, sin)


def _stack_q_heads(q_ref, kv):
    qa = q_ref[:, (2 * kv) * LANES:(2 * kv + 1) * LANES]
    qb = q_ref[:, (2 * kv + 1) * LANES:(2 * kv + 2) * LANES]
    return jnp.concatenate([qa[:, :HEAD_DIM], qa[:, HEAD_DIM:], qb[:, :HEAD_DIM], qb[:, HEAD_DIM:]], axis=0)


def _store_transposed_heads(ref, t_scr):
    for g in range(GQA):
        ref[:, g * KV_COLS:(g + 1) * KV_COLS] = jnp.transpose(t_scr[:, g * Q_TILE:(g + 1) * Q_TILE]).astype(ref.dtype)


def _grouped_head_order(w_o):
    return w_o.reshape(N_KV_HEADS, GQA, HEAD_DIM, -1).transpose(1, 0, 2, 3).reshape(Q_COLS, -1)


def _local_attn_kernel(*refs, has_sink, want_lse, nb, steps):
    q_ref, kc_ref, kp_ref = refs[:3]
    pos = 3
    sink_ref = None
    if has_sink:
        sink_ref = refs[pos]
        pos += 1
    o_ref = refs[pos]
    pos += 1
    lse_ref = None
    if want_lse:
        lse_ref = refs[pos]
        pos += 1
    ot_scr = refs[pos]
    lt_scr = refs[pos + 1] if want_lse else None
    t = pl.program_id(0)

    @pl.when(t >= steps)
    def _():
        o_ref[...] = jnp.zeros(o_ref.shape, o_ref.dtype)
        if want_lse:
            lse_ref[...] = jnp.zeros(lse_ref.shape, lse_ref.dtype)

    @pl.when(t < steps)
    def _():
        _local_attn_step(q_ref, kc_ref, kp_ref, sink_ref, o_ref, lse_ref, ot_scr, lt_scr, t % nb)


def _local_attn_step(q_ref, kc_ref, kp_ref, sink_ref, o_ref, lse_ref, ot_scr, lt_scr, i):
    w = LOCAL_WINDOW
    cols = GQA * Q_TILE
    ci = lax.broadcasted_iota(I32, (2 * w, cols), 0)
    qi = lax.broadcasted_iota(I32, (2 * w, cols), 1) & (Q_TILE - 1)
    delta = qi + w - ci
    mask = (delta >= 0) & (delta <= w) & ((ci >= w) | (i > 0))
    for c in range(N_KV_HEADS // 2):
        ksl = slice(c * LANES, (c + 1) * LANES)
        vsl = slice(KV_COLS + c * LANES, KV_COLS + (c + 1) * LANES)
        kpair = jnp.concatenate([kp_ref[:, ksl], kc_ref[:, ksl]], axis=0)
        vpair = jnp.concatenate([kp_ref[:, vsl], kc_ref[:, vsl]], axis=0)
        vpair_t = jnp.transpose(vpair.astype(F32)).astype(BF16)
        for half in range(2):
            kv = 2 * c + half
            hs = slice(half * HEAD_DIM, (half + 1) * HEAD_DIM)
            st = jnp.where(mask, _nt_dot(kpair[:, hs], _stack_q_heads(q_ref, kv)), NEG)
            m = jnp.max(st, axis=0, keepdims=True)
            if sink_ref is not None:
                sink = sink_ref[kv:kv + 1, :]
                m = jnp.maximum(m, sink)
            pt = jnp.exp(st - m)
            l = jnp.sum(pt, axis=0, keepdims=True)
            if sink_ref is not None:
                l = l + jnp.exp(sink - m)
            acc = jnp.dot(vpair_t[hs, :], pt.astype(BF16), preferred_element_type=F32)
            rows = slice(kv * HEAD_DIM, (kv + 1) * HEAD_DIM)
            ot_scr[rows, :] = acc / l
            if lt_scr is not None:
                lt_scr[rows, :] = jnp.broadcast_to(m + jnp.log(l), (HEAD_DIM, cols))
    _store_transposed_heads(o_ref, ot_scr)
    if lt_scr is not None:
        _store_transposed_heads(lse_ref, lt_scr)


def _local_attn_prompt(q, kvb, m, g, dil, n_batch, seq, sink_rows, want_lse):
    n_g, mp, _ = q.shape
    assert m % dil == 0 and seq % (dil * Q_TILE) == 0 and (m - mp) // dil <= Q_TILE
    nb = seq // dil // Q_TILE
    qv = q.reshape(n_g, mp // dil, dil * Q_COLS)
    kvv = kvb.reshape(n_g, mp // dil, dil * 2 * KV_COLS)
    has_sink = sink_rows is not None
    steps = n_batch * dil * nb

    def decode(t):
        tt = jnp.minimum(t, steps - 1)
        return tt // (dil * nb), (tt // nb) % dil, tt % nb

    def cur_map(t):
        n, r, i = decode(t)
        return (g, n * nb + i, r)

    def prev_map(t):
        n, r, i = decode(t)
        return (g, n * nb + jnp.maximum(i - 1, 0), r)

    def out_map(t):
        n, r, i = decode(t)
        tail = t >= steps
        return (jnp.where(tail, n_batch * nb, n * nb + i), jnp.where(tail, t - steps, r))

    in_specs = [
        pl.BlockSpec((None, Q_TILE, Q_COLS), cur_map),
        pl.BlockSpec((None, Q_TILE, 2 * KV_COLS), cur_map),
        pl.BlockSpec((None, Q_TILE, 2 * KV_COLS), prev_map),
    ]
    args = [qv, kvv, kvv]
    if has_sink:
        in_specs.append(pl.BlockSpec((N_KV_HEADS, GQA * Q_TILE), lambda t: (0, 0)))
        args.append(sink_rows)
    out_spec = pl.BlockSpec((Q_TILE, Q_COLS), out_map)
    out_specs = [out_spec]
    out_shape = [jax.ShapeDtypeStruct((m // dil, dil * Q_COLS), BF16)]
    scratch = [pltpu.VMEM((KV_COLS, GQA * Q_TILE), F32)]
    if want_lse:
        out_specs.append(out_spec)
        out_shape.append(jax.ShapeDtypeStruct((m // dil, dil * Q_COLS), F32))
        scratch.append(pltpu.VMEM((KV_COLS, GQA * Q_TILE), F32))
    outs = pl.pallas_call(
        functools.partial(_local_attn_kernel, has_sink=has_sink, want_lse=want_lse, nb=nb, steps=steps),
        grid=(steps + dil,),
        in_specs=in_specs,
        out_specs=out_specs,
        out_shape=out_shape,
        scratch_shapes=scratch,
        compiler_params=_params(1),
        name=f"local_attn_prompt_d{dil}",
    )(*args)
    return [o.reshape(m, Q_COLS) for o in outs]


def _head_block_mask(rows):
    head = lax.broadcasted_iota(I32, (rows, KV_COLS), 0) & (N_HEADS - 1)
    lane = lax.broadcasted_iota(I32, (rows, KV_COLS), 1)
    return (lane // HEAD_DIM) == (head // GQA)


def _fold_lanes(x):
    return x[:, 0:LANES] + x[:, LANES:2 * LANES] + x[:, 2 * LANES:3 * LANES] + x[:, 3 * LANES:4 * LANES]


def _padded_new_rows(n_ref):
    pad = jnp.zeros((LANES - n_ref.shape[1], KV_COLS), F32)
    knew = jnp.concatenate([n_ref[0, :, 0:KV_COLS], pad], axis=0).astype(BF16)
    vnew = jnp.concatenate([n_ref[0, :, KV_COLS:2 * KV_COLS], pad], axis=0).astype(BF16)
    return knew, vnew


def _sample_attn_kernel(*refs, dil, n_tok, has_sink, want_lse):
    q_ref, c_ref, n_ref = refs[:3]
    pos = 3
    sink_ref = None
    if has_sink:
        sink_ref = refs[pos]
        pos += 1
    o_ref = refs[pos]
    lse_ref = refs[pos + 1] if want_lse else None
    rows = n_tok * N_HEADS
    win = c_ref.shape[-1]
    q = q_ref[0]
    knew, vnew = _padded_new_rows(n_ref)
    tok_c = lax.broadcasted_iota(I32, (rows, win), 0) // N_HEADS
    pos_c = lax.broadcasted_iota(I32, (rows, win), 1)
    valid_c = (((pos_c - tok_c) & (dil - 1)) == 0) & (pos_c >= tok_c)
    tok_n = lax.broadcasted_iota(I32, (rows, LANES), 0) // N_HEADS
    pos_n = lax.broadcasted_iota(I32, (rows, LANES), 1)
    valid_n = (((tok_n - pos_n) & (dil - 1)) == 0) & (pos_n <= tok_n)
    s_c = jnp.where(valid_c, jnp.dot(q, c_ref[0, 0].astype(BF16), preferred_element_type=F32), NEG)
    s_n = jnp.where(valid_n, _nt_dot(q, knew), NEG)
    m = jnp.maximum(jnp.max(s_c, axis=1, keepdims=True), jnp.max(s_n, axis=1, keepdims=True))
    if has_sink:
        sink = sink_ref[...]
        m = jnp.maximum(m, sink)
    p_c = jnp.exp(s_c - m)
    p_n = jnp.exp(s_n - m)
    l = jnp.sum(p_c, axis=1, keepdims=True) + jnp.sum(p_n, axis=1, keepdims=True)
    if has_sink:
        l = l + jnp.exp(sink - m)
    of = (_nt_dot(p_c.astype(BF16), c_ref[0, 1].astype(BF16))
          + jnp.dot(p_n.astype(BF16), vnew, preferred_element_type=F32))
    o_ref[0] = _fold_lanes(jnp.where(_head_block_mask(rows), of, 0.0)) / l
    if want_lse:
        lse_ref[0] = jnp.broadcast_to(m + jnp.log(l), (rows, LANES))


def _sample_attn(qbd, cache_t, kv_new, dil, sink_col, want_lse):
    n_b, rows, _ = qbd.shape
    n_tok = rows // N_HEADS
    win = cache_t.shape[-1]
    assert win == LOCAL_WINDOW * dil and (dil & (dil - 1)) == 0 and n_tok <= 8
    has_sink = sink_col is not None
    in_specs = [
        pl.BlockSpec((1, rows, KV_COLS), lambda b: (b, 0, 0)),
        pl.BlockSpec((1, 2, KV_COLS, win), lambda b: (b, 0, 0, 0)),
        pl.BlockSpec((1, 8, 2 * KV_COLS), lambda b: (b, 0, 0)),
    ]
    args = [qbd, cache_t, kv_new]
    if has_sink:
        in_specs.append(pl.BlockSpec((rows, 1), lambda b: (0, 0)))
        args.append(sink_col)
    out_spec = pl.BlockSpec((1, rows, LANES), lambda b: (b, 0, 0))
    out_specs = [out_spec]
    out_shape = [jax.ShapeDtypeStruct((n_b, rows, LANES), F32)]
    if want_lse:
        out_specs.append(out_spec)
        out_shape.append(jax.ShapeDtypeStruct((n_b, rows, LANES), F32))
    return pl.pallas_call(
        functools.partial(_sample_attn_kernel, dil=dil, n_tok=n_tok, has_sink=has_sink, want_lse=want_lse),
        grid=(n_b,),
        in_specs=in_specs,
        out_specs=out_specs,
        out_shape=out_shape,
        compiler_params=_params(1),
        name=f"local_attn_sample_d{dil}",
    )(*args)


def _block_mean_kernel(k_ref, o_ref):
    n_blk = k_ref.shape[1] // MOBA_BLOCK
    cols = [jnp.mean(k_ref[:, b * MOBA_BLOCK:(b + 1) * MOBA_BLOCK], axis=1, keepdims=True) for b in range(n_blk)]
    lane = lax.broadcasted_iota(I32, (k_ref.shape[0], LANES), 1)
    out = jnp.zeros((k_ref.shape[0], LANES), F32)
    for b in range(n_blk):
        out = jnp.where(lane == b, cols[b], out)
    o_ref[...] = out


def _block_means(kvt):
    n_batch, _, seq = kvt.shape
    assert seq // MOBA_BLOCK <= LANES
    return pl.pallas_call(
        _block_mean_kernel,
        grid=(n_batch,),
        in_specs=[pl.BlockSpec((None, KV_COLS, seq), lambda n: (n, 0, 0))],
        out_specs=pl.BlockSpec((None, KV_COLS, LANES), lambda n: (n, 0, 0)),
        out_shape=jax.ShapeDtypeStruct((n_batch, KV_COLS, LANES), F32),
        compiler_params=_params(1),
        name="moba_block_means",
    )(kvt)


def _moba_prompt_kernel(q_ref, kv_ref, km_ref, o_ref, *scratch, nq, steps, n_blk):
    t = pl.program_id(0)

    @pl.when(t >= steps)
    def _():
        o_ref[...] = jnp.zeros(o_ref.shape, o_ref.dtype)

    @pl.when(t < steps)
    def _():
        _moba_prompt_step(q_ref, kv_ref, km_ref, o_ref, *scratch, t % nq, n_blk)


def _moba_prompt_step(q_ref, kv_ref, km_ref, o_ref, vt_scr, q4_scr, bias_scr, m_scr, l_scr, acc_scr, i, n_blk):
    own = (i * Q_TILE) // MOBA_BLOCK
    cols = GQA * Q_TILE

    @pl.when(i == 0)
    def _():
        for b in range(n_blk):
            for c in range(KV_COLS // LANES):
                v = kv_ref[b * MOBA_BLOCK:(b + 1) * MOBA_BLOCK, KV_COLS + c * LANES:KV_COLS + (c + 1) * LANES]
                vt_scr[c * LANES:(c + 1) * LANES, b * MOBA_BLOCK:(b + 1) * MOBA_BLOCK] = (
                    jnp.transpose(v.astype(F32)).astype(BF16))

    blk_rows = bias_scr.shape[1]
    bid = lax.broadcasted_iota(I32, (blk_rows, cols), 0)
    q_off = (i * Q_TILE) % MOBA_BLOCK + (lax.broadcasted_iota(I32, (MOBA_BLOCK, cols), 1) & (Q_TILE - 1))
    own_bias = jnp.where(lax.broadcasted_iota(I32, (MOBA_BLOCK, cols), 0) <= q_off, 0.0, NEG)
    km_t = jnp.transpose(km_ref[...])[0:blk_rows, :].astype(BF16)
    for kv in range(N_KV_HEADS):
        q4 = _stack_q_heads(q_ref, kv)
        q4_scr[kv] = q4
        gate = _nt_dot(km_t[:, kv * HEAD_DIM:(kv + 1) * HEAD_DIM], q4)
        gate = jnp.where(bid < own, gate, NEG)
        rank = jnp.zeros((blk_rows, cols), F32)
        for b in range(n_blk):
            gb = gate[b:b + 1, :]
            rank = rank + jnp.where((gb > gate) | ((gb == gate) & (b < bid)), 1.0, 0.0)
        bias_scr[kv] = jnp.where((rank < min(MOBA_TOPK, n_blk)) & (bid < own), 0.0, NEG)
    m_scr[...] = jnp.full(m_scr.shape, NEG, F32)
    l_scr[...] = jnp.zeros(l_scr.shape, F32)
    acc_scr[...] = jnp.zeros(acc_scr.shape, F32)

    def update(kv, start, bias):
        c = kv // 2
        hs = slice((kv % 2) * HEAD_DIM, (kv % 2 + 1) * HEAD_DIM)
        rows = slice(kv * HEAD_DIM, (kv + 1) * HEAD_DIM)
        k = kv_ref[pl.ds(start, MOBA_BLOCK), c * LANES:(c + 1) * LANES][:, hs]
        st = _nt_dot(k, q4_scr[kv]) + bias
        m_old = m_scr[kv, 0:1, :]
        m_new = jnp.maximum(m_old, jnp.max(st, axis=0, keepdims=True))
        pt = jnp.exp(st - m_new)
        alpha = jnp.exp(m_old - m_new)
        pv = jnp.dot(vt_scr[rows, pl.ds(start, MOBA_BLOCK)], pt.astype(BF16), preferred_element_type=F32)
        acc_scr[rows, :] = alpha * acc_scr[rows, :] + pv
        l_scr[kv, 0:1, :] = alpha * l_scr[kv, 0:1, :] + jnp.sum(pt, axis=0, keepdims=True)
        m_scr[kv, 0:1, :] = m_new

    def body(j, carry):
        start = pl.multiple_of(j * MOBA_BLOCK, MOBA_BLOCK)
        for kv in range(N_KV_HEADS):
            update(kv, start, bias_scr[kv, pl.ds(j, 1), :])
        return carry

    lax.fori_loop(0, own, body, 0)
    for kv in range(N_KV_HEADS):
        update(kv, pl.multiple_of(own * MOBA_BLOCK, MOBA_BLOCK), own_bias)
    for kv in range(N_KV_HEADS):
        rows = slice(kv * HEAD_DIM, (kv + 1) * HEAD_DIM)
        acc_scr[rows, :] = acc_scr[rows, :] / l_scr[kv, 0:1, :]
    _store_transposed_heads(o_ref, acc_scr)


def _moba_prompt(q, kvb, kmean, m, n_batch, seq):
    nq = seq // Q_TILE
    n_blk = seq // MOBA_BLOCK
    steps = n_batch * nq
    assert m - steps * Q_TILE <= Q_TILE
    batch_of = lambda t: jnp.minimum(t, steps - 1) // nq
    return pl.pallas_call(
        functools.partial(_moba_prompt_kernel, nq=nq, steps=steps, n_blk=n_blk),
        grid=(steps + 1,),
        in_specs=[
            pl.BlockSpec((Q_TILE, Q_COLS), lambda t: (jnp.minimum(t, steps - 1), 0)),
            pl.BlockSpec((seq, 2 * KV_COLS), lambda t: (batch_of(t), 0)),
            pl.BlockSpec((None, KV_COLS, LANES), lambda t: (batch_of(t), 0, 0)),
        ],
        out_specs=pl.BlockSpec((Q_TILE, Q_COLS), lambda t: (t, 0)),
        out_shape=jax.ShapeDtypeStruct((m, Q_COLS), BF16),
        scratch_shapes=[
            pltpu.VMEM((KV_COLS, seq), BF16),
            pltpu.VMEM((N_KV_HEADS, GQA * Q_TILE, HEAD_DIM), BF16),
            pltpu.VMEM((N_KV_HEADS, -(-n_blk // 8) * 8, GQA * Q_TILE), F32),
            pltpu.VMEM((N_KV_HEADS, 8, GQA * Q_TILE), F32),
            pltpu.VMEM((N_KV_HEADS, 8, GQA * Q_TILE), F32),
            pltpu.VMEM((KV_COLS, GQA * Q_TILE), F32),
        ],
        compiler_params=_params(1),
        name="moba_prompt",
    )(q, kvb, kmean)


def _top_blocks(gate, lane, k):
    chosen = jnp.zeros(gate.shape, F32)
    for _ in range(k):
        mx = jnp.max(gate, axis=1, keepdims=True)
        idx = jnp.min(jnp.where(gate == mx, lane, LANES), axis=1, keepdims=True)
        hit = lane == idx
        chosen = jnp.where(hit & (mx > 0.5 * NEG), 1.0, chosen)
        gate = jnp.where(hit, NEG, gate)
    return chosen


def _moba_sample_kernel(pt_ref, q_ref, n_ref, *refs, n_chunks, n_tok):
    page_refs = refs[:PAGES_PER_STEP]
    o_ref, s_scr, acc_scr, l_scr, m_scr, ch_scr = refs[PAGES_PER_STEP:]
    c = pl.program_id(1)
    rows = q_ref.shape[1]
    step_keys = PAGES_PER_STEP * PAGE_SIZE
    blk_per_step = step_keys // MOBA_BLOCK
    n_blk = n_chunks * blk_per_step
    lane = lax.broadcasted_iota(I32, (rows, LANES), 1)

    @pl.when(c < n_chunks)
    def _scores():
        kt = jnp.concatenate([r[...] for r in page_refs], axis=1).astype(BF16)
        s_scr[:, pl.ds(pl.multiple_of(c * step_keys, step_keys), step_keys)] = (
            jnp.dot(q_ref[0], kt, preferred_element_type=F32))

    @pl.when(c == n_chunks)
    def _select():
        gate = jnp.full((rows, LANES), NEG, F32)
        bmax = jnp.full((rows, LANES), NEG, F32)
        for b in range(n_blk):
            sb = s_scr[:, b * MOBA_BLOCK:(b + 1) * MOBA_BLOCK]
            gate = jnp.where(lane == b, jnp.sum(sb, axis=1, keepdims=True), gate)
            bmax = jnp.where(lane == b, jnp.max(sb, axis=1, keepdims=True), bmax)
        chosen = _top_blocks(gate, lane, min(MOBA_TOPK, n_blk))
        knew, vnew = _padded_new_rows(n_ref)
        tok = lax.broadcasted_iota(I32, (rows, LANES), 0) // N_HEADS
        valid = (lane <= tok) & (lane < n_tok)
        s_new = jnp.where(valid, _nt_dot(q_ref[0], knew), NEG)
        m = jnp.maximum(jnp.max(jnp.where(chosen > 0.5, bmax, NEG), axis=1, keepdims=True),
                        jnp.max(s_new, axis=1, keepdims=True))
        p_new = jnp.where(valid, jnp.exp(s_new - m), 0.0)
        acc_scr[...] = jnp.dot(p_new.astype(BF16), vnew, preferred_element_type=F32)
        l_scr[...] = jnp.sum(p_new, axis=1, keepdims=True)
        m_scr[...] = m
        ch_scr[...] = chosen

    @pl.when(c >= n_chunks)
    def _accumulate():
        cc = c - n_chunks
        vt = jnp.concatenate([r[...] for r in page_refs], axis=1).astype(BF16)
        s = s_scr[:, pl.ds(pl.multiple_of(cc * step_keys, step_keys), step_keys)]
        chosen = ch_scr[...]
        keep = jnp.concatenate(
            [jnp.broadcast_to(jnp.max(jnp.where(lane == cc * blk_per_step + u, chosen, 0.0), axis=1, keepdims=True) > 0.5,
                              (rows, MOBA_BLOCK)) for u in range(blk_per_step)], axis=1)
        p = jnp.where(keep, jnp.exp(s - m_scr[...]), 0.0)
        l_scr[...] += jnp.sum(p, axis=1, keepdims=True)
        acc_scr[...] += _nt_dot(p.astype(BF16), vt)

    @pl.when(c == 2 * n_chunks - 1)
    def _finish():
        o_ref[0] = _fold_lanes(jnp.where(_head_block_mask(rows), acc_scr[...], 0.0)) / l_scr[...]


def _moba_sample(qbd, kv_new, cache_t, page_table):
    n_b, rows, _ = qbd.shape
    n_tok = rows // N_HEADS
    n_pages = page_table.shape[1]
    assert n_pages % PAGES_PER_STEP == 0 and (n_pages * PAGE_SIZE) % MOBA_BLOCK == 0
    assert n_pages * PAGE_SIZE // MOBA_BLOCK <= LANES
    n_chunks = n_pages // PAGES_PER_STEP

    def page_map(u):
        return lambda b, c, pt: (pt[b, (c % n_chunks) * PAGES_PER_STEP + u], c // n_chunks, 0, 0)

    page_specs = [pl.BlockSpec((None, None, KV_COLS, PAGE_SIZE), page_map(u)) for u in range(PAGES_PER_STEP)]
    grid_spec = pltpu.PrefetchScalarGridSpec(
        num_scalar_prefetch=1,
        grid=(n_b, 2 * n_chunks),
        in_specs=[
            pl.BlockSpec((1, rows, KV_COLS), lambda b, c, pt: (b, 0, 0)),
            pl.BlockSpec((1, 8, 2 * KV_COLS), lambda b, c, pt: (b, 0, 0)),
        ] + page_specs,
        out_specs=pl.BlockSpec((1, rows, LANES), lambda b, c, pt: (b, 0, 0)),
        scratch_shapes=[
            pltpu.VMEM((rows, n_pages * PAGE_SIZE), F32),
            pltpu.VMEM((rows, KV_COLS), F32),
            pltpu.VMEM((rows, 1), F32),
            pltpu.VMEM((rows, 1), F32),
            pltpu.VMEM((rows, LANES), F32),
        ],
    )
    return pl.pallas_call(
        functools.partial(_moba_sample_kernel, n_chunks=n_chunks, n_tok=n_tok),
        grid_spec=grid_spec,
        out_shape=jax.ShapeDtypeStruct((n_b, rows, LANES), F32),
        compiler_params=_params(2),
        name="moba_sample",
    )(page_table, qbd, kv_new, *([cache_t] * PAGES_PER_STEP))


def _layer_norm(h, gain, bias):
    mu = jnp.mean(h, axis=-1, keepdims=True)
    d = h - mu
    var = jnp.mean(d * d, axis=-1, keepdims=True)
    return d * lax.rsqrt(var + LN_EPS) * gain + bias


def _oproj_ln_kernel(*refs, n_merge):
    x_ref = refs[0]
    o_refs = refs[1:1 + n_merge]
    w_ref, g_ref, b_ref, out_ref = refs[-4:]
    if n_merge == 1:
        o = o_refs[0][...]
    else:
        lses = [r[...] for r in refs[1 + n_merge:1 + 2 * n_merge]]
        mx = functools.reduce(jnp.maximum, lses)
        es = [jnp.exp(l - mx) for l in lses]
        num = sum(e * r[...].astype(F32) for r, e in zip(o_refs, es))
        o = (num / sum(es)).astype(BF16)
    y = jnp.dot(o, w_ref[...], preferred_element_type=F32)
    out_ref[...] = _layer_norm(DEEPNORM_ALPHA * x_ref[...] + y, g_ref[...], b_ref[...])


def _oproj_ln(x, o, lse, w_bf16, gain, bias):
    m = x.shape[0]
    n_merge = len(o)
    tm = _row_tile(m, 320 if n_merge == 1 else 160, 16)
    row = pl.BlockSpec((tm, D_MODEL), lambda i: (i, 0))
    vec = pl.BlockSpec((1, D_MODEL), lambda i: (0, 0))
    wspec = pl.BlockSpec((Q_COLS, D_MODEL), lambda i: (0, 0))
    groups = list(o) + (list(lse) if n_merge > 1 else [])
    in_specs = [row] + [pl.BlockSpec((tm, Q_COLS), lambda i: (i, 0))] * len(groups) + [wspec, vec, vec]
    args = [x] + groups + [w_bf16, gain, bias]
    return pl.pallas_call(
        functools.partial(_oproj_ln_kernel, n_merge=n_merge),
        grid=(m // tm,),
        in_specs=in_specs,
        out_specs=row,
        out_shape=jax.ShapeDtypeStruct((m, D_MODEL), F32),
        compiler_params=_params(1),
        name=f"oproj_ln_m{n_merge}",
    )(*args)


def _router_kernel(x_ref, rwt_ref, bias_ref, idx_ref, gate_ref, cnt_ref, carry_scr):
    i = pl.program_id(0)
    tm = x_ref.shape[0]

    @pl.when(i == 0)
    def _():
        carry_scr[...] = jnp.zeros(carry_scr.shape, F32)

    logits = _nt_dot(rwt_ref[...], x_ref[...].astype(BF16))
    scores = 1.0 / (1.0 + jnp.exp(-logits))
    biased = scores + bias_ref[...]
    srow = [scores[e:e + 1, :] for e in range(N_EXPERTS)]
    brow = [biased[e:e + 1, :] for e in range(N_EXPERTS)]

    gscore = []
    for g in range(N_EXPERT_GROUPS):
        a0, a1, a2, a3 = brow[4 * g:4 * g + 4]
        hi01, lo01 = jnp.maximum(a0, a1), jnp.minimum(a0, a1)
        hi23, lo23 = jnp.maximum(a2, a3), jnp.minimum(a2, a3)
        gscore.append(jnp.maximum(hi01, hi23) + jnp.maximum(jnp.minimum(hi01, hi23), jnp.maximum(lo01, lo23)))
    gsel = jnp.zeros((1, tm), I32)
    best = gscore[0]
    for g in range(1, N_EXPERT_GROUPS):
        better = gscore[g] > best
        gsel = jnp.where(better, g, gsel)
        best = jnp.where(better, gscore[g], best)

    def in_group(rows_, j):
        v = rows_[j]
        for g in range(1, N_EXPERT_GROUPS):
            v = jnp.where(gsel == g, rows_[4 * g + j], v)
        return v

    ab = [in_group(brow, j) for j in range(EXPERTS_PER_GROUP)]
    au = [in_group(srow, j) for j in range(EXPERTS_PER_GROUP)]
    i1 = jnp.zeros((1, tm), I32)
    v1 = ab[0]
    for j in range(1, EXPERTS_PER_GROUP):
        better = ab[j] > v1
        i1 = jnp.where(better, j, i1)
        v1 = jnp.where(better, ab[j], v1)
    i2 = jnp.full((1, tm), -1, I32)
    v2 = jnp.full((1, tm), -jnp.inf, F32)
    for j in range(EXPERTS_PER_GROUP):
        better = (i1 != j) & ((ab[j] > v2) | (i2 < 0))
        i2 = jnp.where(better, j, i2)
        v2 = jnp.where(better, ab[j], v2)

    def pick(vals, idx):
        v = vals[0]
        for j in range(1, EXPERTS_PER_GROUP):
            v = jnp.where(idx == j, vals[j], v)
        return v

    s1, s2 = pick(au, i1), pick(au, i2)
    den = s1 + s2
    e1 = gsel * EXPERTS_PER_GROUP + i1
    e2 = gsel * EXPERTS_PER_GROUP + i2

    eid = lax.broadcasted_iota(I32, (N_EXPERTS, tm), 0)
    hit1, hit2 = eid == e1, eid == e2
    onehot = jnp.where(hit1 | hit2, 1.0, 0.0)
    before = lax.broadcasted_iota(I32, (tm, tm), 0) < lax.broadcasted_iota(I32, (tm, tm), 1)
    prefix = jnp.dot(onehot.astype(BF16), jnp.where(before, 1.0, 0.0).astype(BF16), preferred_element_type=F32)
    offset = carry_scr[...][:, 0:1] + prefix
    r1 = jnp.sum(jnp.where(hit1, offset, 0.0), axis=0, keepdims=True).astype(I32)
    r2 = jnp.sum(jnp.where(hit2, offset, 0.0), axis=0, keepdims=True).astype(I32)
    carry_scr[...] = carry_scr[...] + jnp.sum(onehot, axis=1, keepdims=True)

    row8 = lax.broadcasted_iota(I32, (8, tm), 0)
    idx_ref[...] = jnp.where(row8 == 0, e1, jnp.where(row8 == 1, e2, jnp.where(row8 == 2, r1, jnp.where(row8 == 3, r2, 0))))
    gate_ref[...] = jnp.where(row8 == 0, s1 / den, jnp.where(row8 == 1, s2 / den, 0.0))
    cnt_ref[...] = carry_scr[...]


def _router(x, rwt_bf16, bias_col):
    m = x.shape[0]
    tm = _row_tile(m, 640, LANES)
    return pl.pallas_call(
        _router_kernel,
        grid=(m // tm,),
        in_specs=[
            pl.BlockSpec((tm, D_MODEL), lambda i: (i, 0)),
            pl.BlockSpec((N_EXPERTS, D_MODEL), lambda i: (0, 0)),
            pl.BlockSpec((N_EXPERTS, 1), lambda i: (0, 0)),
        ],
        out_specs=[
            pl.BlockSpec((8, tm), lambda i: (0, i)),
            pl.BlockSpec((8, tm), lambda i: (0, i)),
            pl.BlockSpec((N_EXPERTS, LANES), lambda i: (0, 0)),
        ],
        out_shape=[
            jax.ShapeDtypeStruct((8, m), I32),
            jax.ShapeDtypeStruct((8, m), F32),
            jax.ShapeDtypeStruct((N_EXPERTS, LANES), F32),
        ],
        scratch_shapes=[pltpu.VMEM((N_EXPERTS, LANES), F32)],
        compiler_params=_params(1),
        name="router",
    )(x, rwt_bf16, bias_col)


def _start_row_gather(src_hbm, row_of, n_rows, dst, sem):
    def body(r, carry):
        pltpu.make_async_copy(src_hbm.at[pl.ds(row_of(r), 1), :], dst.at[pl.ds(r, 1), :], sem).start()
        return carry
    lax.fori_loop(0, n_rows, body, 0, unroll=8)


def _wait_row_gather(src_hbm, n_rows, dst, sem):
    pltpu.make_async_copy(src_hbm.at[pl.ds(0, n_rows), :], dst, sem).wait()


def _start_row_gather_inline(src_hbm, idx_ref, base, n_rows, dst, sem):
    for r in range(n_rows):
        pltpu.make_async_copy(src_hbm.at[pl.ds(idx_ref[base + r], 1), :], dst.at[pl.ds(r, 1), :], sem).start()


def _moe_ffn_kernel(ce_ref, tok_ref, x_hbm, wg_ref, wu_ref, wd_ref, y_ref, xbuf0, xbuf1, sem, wgb, wub, wdb, *,
                    n_chunks):
    c = pl.program_id(0)
    bufs = (xbuf0, xbuf1)

    @pl.when(c == 0)
    def _():
        _start_row_gather(x_hbm, lambda r: tok_ref[r], MOE_CHUNK, xbuf0, sem.at[0])

    @pl.when((c == 0) | (ce_ref[c] != ce_ref[jnp.maximum(c - 1, 0)]))
    def _():
        wgb[...] = wg_ref[...].astype(BF16)
        wub[...] = wu_ref[...].astype(BF16)
        wdb[...] = wd_ref[...].astype(BF16)

    def run(cur):
        _wait_row_gather(x_hbm, MOE_CHUNK, bufs[cur], sem.at[cur])
        _start_row_gather_inline(x_hbm, tok_ref, (c + 1) * MOE_CHUNK, MOE_CHUNK, bufs[1 - cur], sem.at[1 - cur])
        xb = bufs[cur][...].astype(BF16)
        a = jnp.dot(xb, wgb[...], preferred_element_type=F32)
        u = jnp.dot(xb, wub[...], preferred_element_type=F32)
        h = (a / (1.0 + jnp.exp(-a))) * u
        y_ref[...] = jnp.dot(h.astype(BF16), wdb[...], preferred_element_type=F32)

    @pl.when(c % 2 == 0)
    def _():
        run(0)

    @pl.when(c % 2 == 1)
    def _():
        run(1)

    @pl.when(c == n_chunks - 1)
    def _():
        _wait_row_gather(x_hbm, MOE_CHUNK, bufs[n_chunks % 2], sem.at[n_chunks % 2])


def _moe_ffn(x, chunk_expert, tok_of_row, w_gate, w_up, w_down, layer):
    n_chunks = chunk_expert.shape[0]
    assert tok_of_row.shape[0] == (n_chunks + 1) * MOE_CHUNK
    w_in = pl.BlockSpec((None, None, D_MODEL, D_EXPERT), lambda c, ce, tok: (layer, ce[c], 0, 0))
    grid_spec = pltpu.PrefetchScalarGridSpec(
        num_scalar_prefetch=2,
        grid=(n_chunks,),
        in_specs=[
            pl.BlockSpec(memory_space=pl.ANY),
            w_in,
            w_in,
            pl.BlockSpec((None, None, D_EXPERT, D_MODEL), lambda c, ce, tok: (layer, ce[c], 0, 0)),
        ],
        out_specs=pl.BlockSpec((MOE_CHUNK, D_MODEL), lambda c, ce, tok: (c, 0)),
        scratch_shapes=[
            pltpu.VMEM((MOE_CHUNK, D_MODEL), F32),
            pltpu.VMEM((MOE_CHUNK, D_MODEL), F32),
            pltpu.SemaphoreType.DMA((2,)),
            pltpu.VMEM((D_MODEL, D_EXPERT), BF16),
            pltpu.VMEM((D_MODEL, D_EXPERT), BF16),
            pltpu.VMEM((D_EXPERT, D_MODEL), BF16),
        ],
    )
    return pl.pallas_call(
        functools.partial(_moe_ffn_kernel, n_chunks=n_chunks),
        grid_spec=grid_spec,
        out_shape=jax.ShapeDtypeStruct((n_chunks * MOE_CHUNK, D_MODEL), F32),
        compiler_params=_params(1),
        name="moe_ffn",
    )(chunk_expert, tok_of_row, x, w_gate, w_up, w_down)


def _combine_ln_kernel(dest_ref, y_hbm, x_ref, g1_ref, g2_ref, gain_ref, bias_ref, out_ref, ybuf0, ybuf1, sem, *,
                       n_tiles):
    i = pl.program_id(0)
    tm = x_ref.shape[0]
    bufs = (ybuf0, ybuf1)

    @pl.when(i == 0)
    def _():
        _start_row_gather(y_hbm, lambda r: dest_ref[r], 2 * tm, ybuf0, sem.at[0])

    def run(cur):
        _wait_row_gather(y_hbm, 2 * tm, bufs[cur], sem.at[cur])
        _start_row_gather_inline(y_hbm, dest_ref, (i + 1) * 2 * tm, 2 * tm, bufs[1 - cur], sem.at[1 - cur])
        f = g1_ref[...] * bufs[cur][0:tm, :] + g2_ref[...] * bufs[cur][tm:2 * tm, :]
        out_ref[...] = _layer_norm(DEEPNORM_ALPHA * x_ref[...] + f, gain_ref[...], bias_ref[...])

    @pl.when(i % 2 == 0)
    def _():
        run(0)

    @pl.when(i % 2 == 1)
    def _():
        run(1)

    @pl.when(i == n_tiles - 1)
    def _():
        _wait_row_gather(y_hbm, 2 * tm, bufs[n_tiles % 2], sem.at[n_tiles % 2])


def _combine_ln(x, ybuf, dest, g1, g2, gain, bias):
    m = x.shape[0]
    tm = _row_tile(m, 128, 8)
    n_tiles = m // tm
    dest_tiles = dest.reshape(2, n_tiles, tm).transpose(1, 0, 2).reshape(-1)
    dest_tiles = jnp.concatenate([dest_tiles, jnp.zeros((2 * tm,), I32)])
    row = pl.BlockSpec((tm, D_MODEL), lambda i, d: (i, 0))
    col = pl.BlockSpec((tm, 1), lambda i, d: (i, 0))
    vec = pl.BlockSpec((1, D_MODEL), lambda i, d: (0, 0))
    grid_spec = pltpu.PrefetchScalarGridSpec(
        num_scalar_prefetch=1,
        grid=(n_tiles,),
        in_specs=[pl.BlockSpec(memory_space=pl.ANY), row, col, col, vec, vec],
        out_specs=row,
        scratch_shapes=[pltpu.VMEM((2 * tm, D_MODEL), F32), pltpu.VMEM((2 * tm, D_MODEL), F32),
                        pltpu.SemaphoreType.DMA((2,))],
    )
    return pl.pallas_call(
        functools.partial(_combine_ln_kernel, n_tiles=n_tiles),
        grid_spec=grid_spec,
        out_shape=jax.ShapeDtypeStruct((m, D_MODEL), F32),
        compiler_params=_params(1),
        name="moe_combine_ln",
    )(dest_tiles, ybuf, x, g1, g2, gain, bias)


def _moe_layer(x, rwt_bf16, rbias_col, w_gate, w_up, w_down, layer, gain, bias):
    m = x.shape[0]
    idx, gates, counts = _router(x, rwt_bf16, rbias_col)
    experts, ranks = idx[0:2], idx[2:4]
    cnt = counts[:, 0].astype(I32)
    padded = (cnt + MOE_CHUNK - 1) // MOE_CHUNK * MOE_CHUNK
    pend = jnp.cumsum(padded)
    pstart = pend - padded
    eid = jnp.arange(N_EXPERTS, dtype=I32)[:, None, None]
    dest = ranks + jnp.sum(jnp.where(experts[None] == eid, pstart[:, None, None], 0), axis=0)
    n_chunks = -(-2 * m // MOE_CHUNK) + N_EXPERTS
    chunk_start = jnp.arange(n_chunks, dtype=I32) * MOE_CHUNK
    chunk_expert = jnp.minimum(jnp.sum((pend[None, :] <= chunk_start[:, None]).astype(I32), axis=1), N_EXPERTS - 1)
    tok = jnp.tile(jnp.arange(m, dtype=I32), 2)
    tok_of_row = jnp.zeros(((n_chunks + 1) * MOE_CHUNK,), I32).at[dest.reshape(-1)].set(tok)
    ybuf = _moe_ffn(x, chunk_expert, tok_of_row, w_gate, w_up, w_down, layer)
    return _combine_ln(x, ybuf, dest, gates[0].reshape(m, 1), gates[1].reshape(m, 1), gain, bias)


def _block_diag_queries(q_rows, n_b, n_tok):
    q4 = q_rows.reshape(n_b, n_tok, N_HEADS, HEAD_DIM)
    tiled = jnp.tile(q4, (1, 1, 1, N_KV_HEADS))
    head = jnp.arange(N_HEADS)[:, None] // GQA
    lane_kv = jnp.arange(KV_COLS)[None, :] // HEAD_DIM
    return jnp.where(head == lane_kv, tiled, jnp.zeros((), tiled.dtype)).reshape(n_b, n_tok * N_HEADS, KV_COLS)


def _sample_rows_grouped(o, n_b, n_tok):
    o = o.reshape(n_b, n_tok, N_KV_HEADS, GQA, LANES)
    return o.transpose(0, 1, 3, 2, 4)


def _sample_out(o, n_b, n_tok):
    o = _sample_rows_grouped(o, n_b, n_tok)
    return (o[..., :HEAD_DIM] + o[..., HEAD_DIM:]).reshape(n_b * n_tok, Q_COLS)


def _sample_lse(lse, n_b, n_tok):
    return _sample_rows_grouped(lse, n_b, n_tok)[..., :HEAD_DIM].reshape(n_b * n_tok, Q_COLS)


def _new_kv_rows(kv_s, n_b, n_tok):
    return jnp.pad(kv_s.reshape(n_b, n_tok, 2 * KV_COLS), ((0, 0), (0, 8 - n_tok), (0, 0)))


def _position_minor(cache):
    lead = cache.shape[:-4]
    n = len(lead)
    t = cache.transpose(tuple(range(n)) + (n + 1, n + 2, n + 3, n))
    return t.reshape(lead + (2, KV_COLS, cache.shape[-4]))


def _position_major(kvt):
    lead = kvt.shape[:-2]
    n = len(lead)
    t = kvt.reshape(lead + (2, N_KV_HEADS, HEAD_DIM, kvt.shape[-1]))
    return t.transpose(tuple(range(n)) + (n + 3, n, n + 1, n + 2))


def _window_state(kvt, kv_s, cache, n_b, n_tok, window):
    seq = kvt.shape[-1]
    state_p = _position_major(kvt[..., seq - min(window, seq):])
    kvs = kv_s.reshape(n_b, n_tok, 2, N_KV_HEADS, HEAD_DIM)
    state_s = jnp.concatenate([cache, kvs], axis=1)[:, -cache.shape[1]:]
    return state_p, state_s


def kernel(x_prompt, x_sample, cache_swa, cache_dil0, cache_dil1, cache_dil2, cache_moba, page_table,
           w_qkv_swa, w_o_swa, sinks_swa, w_qkv_dil, w_o_dil, w_qkv_moba, w_o_moba,
           ln_gain, ln_bias, router_w, router_bias, w_gate_e, w_up_e, w_down_e):
    n_batch, seq, d = x_prompt.shape
    n_b, n_tok = x_sample.shape[:2]
    past_len = page_table.shape[1] * cache_moba.shape[2]
    mp, ms = n_batch * seq, n_b * n_tok
    m = mp + ms
    assert d == D_MODEL and cache_moba.shape[2] == PAGE_SIZE and n_tok <= 4 and ms == Q_TILE and mp % ms == 0
    x = jnp.concatenate([x_prompt.reshape(mp, d), x_sample.reshape(ms, d)], axis=0)
    pos = jnp.concatenate([jnp.tile(jnp.arange(seq), n_batch), jnp.tile(past_len + jnp.arange(n_tok), n_b)])
    cos, sin = _rope_tables(pos)
    rwt = router_w.T.astype(BF16)
    rbias = router_bias.astype(F32).reshape(N_EXPERTS, 1)
    dil_caches = (cache_dil0, cache_dil1, cache_dil2)

    def project(w_bf16, g):
        q, kvb, kvt = _qkv_rope(x, w_bf16, cos, sin, g, 1, 0, mp, seq)
        q_s, kv_s = _qkv_rope(x, w_bf16, cos, sin, g, 1, mp, ms, None)
        return q, kvb, kvt, q_s, kv_s

    def sample_local(q_s, kv_s, cache, dil, sink_col, want_lse):
        outs = _sample_attn(_block_diag_queries(q_s, n_b, n_tok), _position_minor(cache),
                            _new_kv_rows(kv_s, n_b, n_tok), dil, sink_col, want_lse)
        return _sample_out(outs[0], n_b, n_tok), (_sample_lse(outs[1], n_b, n_tok) if want_lse else None)

    swa_p, swa_s, moba_p, moba_s = [], [], [], []
    dil_p, dil_s = ([], [], []), ([], [], [])
    for i in range(DEPTH):
        kind, j = i % 3, i // 3
        gain1, bias1 = ln_gain[i, 0].reshape(1, d), ln_bias[i, 0].reshape(1, d)
        gain2, bias2 = ln_gain[i, 1].reshape(1, d), ln_bias[i, 1].reshape(1, d)
        if kind == 0:
            q, kvb, kvt, q_s, kv_s = project(w_qkv_swa[j].astype(BF16), 0)
            sinks = sinks_swa[j].astype(F32)
            sink_rows = jnp.repeat(sinks, Q_TILE).reshape(N_KV_HEADS, GQA * Q_TILE)
            (o,) = _local_attn_prompt(q, kvb, m, 0, 1, n_batch, seq, sink_rows, False)
            sink_col = jnp.tile(sinks, n_tok).reshape(n_tok * N_HEADS, 1)
            o_s, _ = sample_local(q_s[0], kv_s[0], cache_swa[j], 1, sink_col, False)
            o = lax.dynamic_update_slice(o, o_s.astype(BF16), (mp, 0))
            sp, ss = _window_state(kvt[0], kv_s[0], cache_swa[j], n_b, n_tok, SWA_WINDOW)
            swa_p.append(sp)
            swa_s.append(ss)
            x = _oproj_ln(x, [o], None, _grouped_head_order(w_o_swa[j]).astype(BF16), gain1, bias1)
        elif kind == 1:
            w_dil = w_qkv_dil[j].astype(BF16)
            os_, lses = [], []
            for g, (win, dil) in enumerate(DIL_PAIRS):
                q, kvb, kvt, q_s, kv_s = project(w_dil, g)
                o, lse = _local_attn_prompt(q, kvb, m, 0, dil, n_batch, seq, None, True)
                o_s, lse_s = sample_local(q_s[0], kv_s[0], dil_caches[g][j], dil, None, True)
                os_.append(lax.dynamic_update_slice(o, o_s.astype(BF16), (mp, 0)))
                lses.append(lax.dynamic_update_slice(lse, lse_s, (mp, 0)))
                sp, ss = _window_state(kvt[0], kv_s[0], dil_caches[g][j], n_b, n_tok, win)
                dil_p[g].append(sp)
                dil_s[g].append(ss)
            x = _oproj_ln(x, os_, lses, _grouped_head_order(w_o_dil[j]).astype(BF16), gain1, bias1)
        else:
            q, kvb, kvt, q_s, kv_s = project(w_qkv_moba[j].astype(BF16), 0)
            o = _moba_prompt(q[0], kvb[0], _block_means(kvt[0]), m, n_batch, seq)
            o_s = _moba_sample(_block_diag_queries(q_s[0], n_b, n_tok), _new_kv_rows(kv_s[0], n_b, n_tok),
                               _position_minor(cache_moba[j]), page_table)
            o = lax.dynamic_update_slice(o, _sample_out(o_s, n_b, n_tok).astype(BF16), (mp, 0))
            moba_p.append(_position_major(kvt[0]))
            moba_s.append(kv_s[0].reshape(n_b, n_tok, 2, N_KV_HEADS, HEAD_DIM))
            x = _oproj_ln(x, [o], None, _grouped_head_order(w_o_moba[j]).astype(BF16), gain1, bias1)
        x = _moe_layer(x, rwt, rbias, w_gate_e, w_up_e, w_down_e, i, gain2, bias2)
    y_prompt = x[:mp].reshape(n_batch, seq, d)
    y_sample = x[mp:].reshape(n_b, n_tok, d)
    return (y_prompt, y_sample, jnp.stack(swa_p), jnp.stack(swa_s),
            jnp.stack(dil_p[0]), jnp.stack(dil_s[0]), jnp.stack(dil_p[1]), jnp.stack(dil_s[1]),
            jnp.stack(dil_p[2]), jnp.stack(dil_s[2]), jnp.stack(moba_p), jnp.stack(moba_s))
```

```python
import functools
import math

import jax
import jax.numpy as jnp
from jax import lax
from jax.experimental import pallas as pl
from jax.experimental.pallas import tpu as pltpu

F32, BF16, I32 = jnp.float32, jnp.bfloat16, jnp.int32

D_MODEL = 2048
HEAD_DIM = 64
N_HEADS = 32
N_KV_HEADS = 8
GQA = N_HEADS // N_KV_HEADS
Q_COLS = N_HEADS * HEAD_DIM
KV_COLS = N_KV_HEADS * HEAD_DIM
QKV_COLS = Q_COLS + 2 * KV_COLS
ATTN_SCALE = HEAD_DIM ** -0.5
ROPE_THETA = 10000.0
SWA_WINDOW = 128
LOCAL_WINDOW = 128
DIL_PAIRS = ((128, 1), (512, 4), (2048, 16))
MOBA_BLOCK = 256
MOBA_TOPK = 3
PAGE_SIZE = 128
N_EXPERTS = 16
N_EXPERT_GROUPS = 4
EXPERTS_PER_GROUP = 4
D_EXPERT = D_MODEL // 4
DEPTH = 4
DEEPNORM_ALPHA = (2 * DEPTH) ** 0.25
LN_EPS = 1e-5
NEG = -1e30

LANES = 128
Q_TILE = 128
QKV_COL_TILE = 512
QKV_ROW_TILE = 512
MOE_CHUNK = 256
PAGES_PER_STEP = 16
VMEM_LIMIT = 56 * 1024 * 1024


def _params(n_axes, vmem=VMEM_LIMIT):
    return pltpu.CompilerParams(dimension_semantics=("arbitrary",) * n_axes, vmem_limit_bytes=vmem)


def _nt_dot(a, b):
    return lax.dot_general(a, b, (((1,), (1,)), ((), ())), preferred_element_type=F32)


def _row_tile(m, pref, align):
    best = m
    for t in range(align, min(pref, m) + 1, align):
        if m % t == 0:
            best = t
    return best


def _rope_tables(pos):
    half = HEAD_DIM // 2
    inv_freq = jnp.exp(jnp.arange(half, dtype=F32) * (-2.0 * math.log(ROPE_THETA) / HEAD_DIM))
    ang = pos.astype(F32)[:, None] * inv_freq[None, :]
    cos, sin = jnp.cos(ang), jnp.sin(ang)
    return jnp.tile(cos, (1, 4)), jnp.concatenate([-sin, sin, -sin, sin], axis=1)


def _qkv_rope_kernel(x_ref, w_ref, cos_ref, sin_ref, q_ref, kv_ref, *kvt_ref):
    xb = x_ref[...].astype(BF16)
    tm = xb.shape[0]
    lane = lax.broadcasted_iota(I32, (tm, LANES), 1)
    first_half = (lane & (HEAD_DIM - 1)) < HEAD_DIM // 2

    def rope(z):
        cos, sin = cos_ref[...], sin_ref[...]
        parts = []
        for c in range(z.shape[1] // LANES):
            zc = z[:, c * LANES:(c + 1) * LANES]
            other = jnp.where(first_half, pltpu.roll(zc, LANES - HEAD_DIM // 2, 1), pltpu.roll(zc, HEAD_DIM // 2, 1))
            parts.append(zc * cos + other * sin)
        return jnp.concatenate(parts, axis=1)

    n_q = Q_COLS // QKV_COL_TILE
    for j in range(QKV_COLS // QKV_COL_TILE):
        cols = slice(j * QKV_COL_TILE, (j + 1) * QKV_COL_TILE)
        z = jnp.dot(xb, w_ref[:, cols], preferred_element_type=F32)
        if j < n_q:
            q_ref[:, cols] = (rope(z) * ATTN_SCALE).astype(BF16)
        else:
            val = rope(z) if j == n_q else z
            kcols = slice((j - n_q) * QKV_COL_TILE, (j - n_q + 1) * QKV_COL_TILE)
            kv_ref[:, kcols] = val.astype(kv_ref.dtype)
            if kvt_ref:
                kvt_ref[0][kcols, :] = val.T


def _qkv_rope(x, w_bf16, cos, sin, g0, n_groups, row0, n_rows, seq):
    tm = QKV_ROW_TILE if seq is not None else n_rows
    assert n_rows % tm == 0 and row0 % tm == 0 and (seq is None or seq % tm == 0)
    blk0 = row0 // tm
    out_specs = [
        pl.BlockSpec((None, tm, Q_COLS), lambda g, i: (g, i, 0)),
        pl.BlockSpec((None, tm, 2 * KV_COLS), lambda g, i: (g, i, 0)),
    ]
    out_shape = [
        jax.ShapeDtypeStruct((n_groups, n_rows, Q_COLS), BF16),
        jax.ShapeDtypeStruct((n_groups, n_rows, 2 * KV_COLS), BF16 if seq is not None else F32),
    ]
    if seq is not None:
        per = seq // tm
        out_specs.append(pl.BlockSpec((None, None, 2 * KV_COLS, tm), lambda g, i: (g, i // per, 0, i % per)))
        out_shape.append(jax.ShapeDtypeStruct((n_groups, n_rows // seq, 2 * KV_COLS, seq), F32))
    return pl.pallas_call(
        _qkv_rope_kernel,
        grid=(n_groups, n_rows // tm),
        in_specs=[
            pl.BlockSpec((tm, D_MODEL), lambda g, i: (blk0 + i, 0)),
            pl.BlockSpec((D_MODEL, QKV_COLS), lambda g, i: (0, g0 + g)),
            pl.BlockSpec((tm, LANES), lambda g, i: (blk0 + i, 0)),
            pl.BlockSpec((tm, LANES), lambda g, i: (blk0 + i, 0)),
        ],
        out_specs=out_specs,
        out_shape=out_shape,
        compiler_params=_params(2),
        name="qkv_rope_prompt" if seq is not None else "qkv_rope_sample",
    )(x, w_bf16, cos, sin)


def _stack_q_heads(q_ref, kv):
    qa = q_ref[:, (2 * kv) * LANES:(2 * kv + 1) * LANES]
    qb = q_ref[:, (2 * kv + 1) * LANES:(2 * kv + 2) * LANES]
    return jnp.concatenate([qa[:, :HEAD_DIM], qa[:, HEAD_DIM:], qb[:, :HEAD_DIM], qb[:, HEAD_DIM:]], axis=0)


def _store_transposed_heads(ref, t_scr):
    for g in range(GQA):
        ref[:, g * KV_COLS:(g + 1) * KV_COLS] = jnp.transpose(t_scr[:, g * Q_TILE:(g + 1) * Q_TILE]).astype(ref.dtype)


def _grouped_head_order(w_o):
    return w_o.reshape(N_KV_HEADS, GQA, HEAD_DIM, -1).transpose(1, 0, 2, 3).reshape(Q_COLS, -1)


def _local_attn_kernel(*refs, has_sink, want_lse, nb, steps):
    q_ref, kc_ref, kp_ref = refs[:3]
    pos = 3
    sink_ref = None
    if has_sink:
        sink_ref = refs[pos]
        pos += 1
    o_ref = refs[pos]
    pos += 1
    lse_ref = None
    if want_lse:
        lse_ref = refs[pos]
        pos += 1
    ot_scr = refs[pos]
    lt_scr = refs[pos + 1] if want_lse else None
    t = pl.program_id(0)

    @pl.when(t >= steps)
    def _():
        o_ref[...] = jnp.zeros(o_ref.shape, o_ref.dtype)
        if want_lse:
            lse_ref[...] = jnp.zeros(lse_ref.shape, lse_ref.dtype)

    @pl.when(t < steps)
    def _():
        _local_attn_step(q_ref, kc_ref, kp_ref, sink_ref, o_ref, lse_ref, ot_scr, lt_scr, t % nb)


def _local_attn_step(q_ref, kc_ref, kp_ref, sink_ref, o_ref, lse_ref, ot_scr, lt_scr, i):
    w = LOCAL_WINDOW
    cols = GQA * Q_TILE
    ci = lax.broadcasted_iota(I32, (2 * w, cols), 0)
    qi = lax.broadcasted_iota(I32, (2 * w, cols), 1) & (Q_TILE - 1)
    delta = qi + w - ci
    mask = (delta >= 0) & (delta <= w) & ((ci >= w) | (i > 0))
    for c in range(N_KV_HEADS // 2):
        ksl = slice(c * LANES, (c + 1) * LANES)
        vsl = slice(KV_COLS + c * LANES, KV_COLS + (c + 1) * LANES)
        kpair = jnp.concatenate([kp_ref[:, ksl], kc_ref[:, ksl]], axis=0)
        vpair = jnp.concatenate([kp_ref[:, vsl], kc_ref[:, vsl]], axis=0)
        vpair_t = jnp.transpose(vpair.astype(F32)).astype(BF16)
        for half in range(2):
            kv = 2 * c + half
            hs = slice(half * HEAD_DIM, (half + 1) * HEAD_DIM)
            st = jnp.where(mask, _nt_dot(kpair[:, hs], _stack_q_heads(q_ref, kv)), NEG)
            m = jnp.max(st, axis=0, keepdims=True)
            if sink_ref is not None:
                sink = sink_ref[kv:kv + 1, :]
                m = jnp.maximum(m, sink)
            pt = jnp.exp(st - m)
            l = jnp.sum(pt, axis=0, keepdims=True)
            if sink_ref is not None:
                l = l + jnp.exp(sink - m)
            acc = jnp.dot(vpair_t[hs, :], pt.astype(BF16), preferred_element_type=F32)
            rows = slice(kv * HEAD_DIM, (kv + 1) * HEAD_DIM)
            ot_scr[rows, :] = acc / l
            if lt_scr is not None:
                lt_scr[rows, :] = jnp.broadcast_to(m + jnp.log(l), (HEAD_DIM, cols))
    _store_transposed_heads(o_ref, ot_scr)
    if lt_scr is not None:
        _store_transposed_heads(lse_ref, lt_scr)


def _local_attn_prompt(q, kvb, m, g, dil, n_batch, seq, sink_rows, want_lse):
    n_g, mp, _ = q.shape
    assert m % dil == 0 and seq % (dil * Q_TILE) == 0 and (m - mp) // dil <= Q_TILE
    nb = seq // dil // Q_TILE
    qv = q.reshape(n_g, mp // dil, dil * Q_COLS)
    kvv = kvb.reshape(n_g, mp // dil, dil * 2 * KV_COLS)
    has_sink = sink_rows is not None
    steps = n_batch * dil * nb

    def decode(t):
        tt = jnp.minimum(t, steps - 1)
        return tt // (dil * nb), (tt // nb) % dil, tt % nb

    def cur_map(t):
        n, r, i = decode(t)
        return (g, n * nb + i, r)

    def prev_map(t):
        n, r, i = decode(t)
        return (g, n * nb + jnp.maximum(i - 1, 0), r)

    def out_map(t):
        n, r, i = decode(t)
        tail = t >= steps
        return (jnp.where(tail, n_batch * nb, n * nb + i), jnp.where(tail, t - steps, r))

    in_specs = [
        pl.BlockSpec((None, Q_TILE, Q_COLS), cur_map),
        pl.BlockSpec((None, Q_TILE, 2 * KV_COLS), cur_map),
        pl.BlockSpec((None, Q_TILE, 2 * KV_COLS), prev_map),
    ]
    args = [qv, kvv, kvv]
    if has_sink:
        in_specs.append(pl.BlockSpec((N_KV_HEADS, GQA * Q_TILE), lambda t: (0, 0)))
        args.append(sink_rows)
    out_spec = pl.BlockSpec((Q_TILE, Q_COLS), out_map)
    out_specs = [out_spec]
    out_shape = [jax.ShapeDtypeStruct((m // dil, dil * Q_COLS), BF16)]
    scratch = [pltpu.VMEM((KV_COLS, GQA * Q_TILE), F32)]
    if want_lse:
        out_specs.append(out_spec)
        out_shape.append(jax.ShapeDtypeStruct((m // dil, dil * Q_COLS), F32))
        scratch.append(pltpu.VMEM((KV_COLS, GQA * Q_TILE), F32))
    outs = pl.pallas_call(
        functools.partial(_local_attn_kernel, has_sink=has_sink, want_lse=want_lse, nb=nb, steps=steps),
        grid=(steps + dil,),
        in_specs=in_specs,
        out_specs=out_specs,
        out_shape=out_shape,
        scratch_shapes=scratch,
        compiler_params=_params(1),
        name=f"local_attn_prompt_d{dil}",
    )(*args)
    return [o.reshape(m, Q_COLS) for o in outs]


def _head_block_mask(rows):
    head = lax.broadcasted_iota(I32, (rows, KV_COLS), 0) & (N_HEADS - 1)
    lane = lax.broadcasted_iota(I32, (rows, KV_COLS), 1)
    return (lane // HEAD_DIM) == (head // GQA)


def _fold_lanes(x):
    return x[:, 0:LANES] + x[:, LANES:2 * LANES] + x[:, 2 * LANES:3 * LANES] + x[:, 3 * LANES:4 * LANES]


def _padded_new_rows(n_ref):
    pad = jnp.zeros((LANES - n_ref.shape[1], KV_COLS), F32)
    knew = jnp.concatenate([n_ref[0, :, 0:KV_COLS], pad], axis=0).astype(BF16)
    vnew = jnp.concatenate([n_ref[0, :, KV_COLS:2 * KV_COLS], pad], axis=0).astype(BF16)
    return knew, vnew


def _sample_attn_kernel(*refs, dil, n_tok, has_sink, want_lse):
    q_ref, c_ref, n_ref = refs[:3]
    pos = 3
    sink_ref = None
    if has_sink:
        sink_ref = refs[pos]
        pos += 1
    o_ref = refs[pos]
    lse_ref = refs[pos + 1] if want_lse else None
    rows = n_tok * N_HEADS
    win = c_ref.shape[-1]
    q = q_ref[0]
    knew, vnew = _padded_new_rows(n_ref)
    tok_c = lax.broadcasted_iota(I32, (rows, win), 0) // N_HEADS
    pos_c = lax.broadcasted_iota(I32, (rows, win), 1)
    valid_c = (((pos_c - tok_c) & (dil - 1)) == 0) & (pos_c >= tok_c)
    tok_n = lax.broadcasted_iota(I32, (rows, LANES), 0) // N_HEADS
    pos_n = lax.broadcasted_iota(I32, (rows, LANES), 1)
    valid_n = (((tok_n - pos_n) & (dil - 1)) == 0) & (pos_n <= tok_n)
    s_c = jnp.where(valid_c, jnp.dot(q, c_ref[0, 0].astype(BF16), preferred_element_type=F32), NEG)
    s_n = jnp.where(valid_n, _nt_dot(q, knew), NEG)
    m = jnp.maximum(jnp.max(s_c, axis=1, keepdims=True), jnp.max(s_n, axis=1, keepdims=True))
    if has_sink:
        sink = sink_ref[...]
        m = jnp.maximum(m, sink)
    p_c = jnp.exp(s_c - m)
    p_n = jnp.exp(s_n - m)
    l = jnp.sum(p_c, axis=1, keepdims=True) + jnp.sum(p_n, axis=1, keepdims=True)
    if has_sink:
        l = l + jnp.exp(sink - m)
    of = (_nt_dot(p_c.astype(BF16), c_ref[0, 1].astype(BF16))
          + jnp.dot(p_n.astype(BF16), vnew, preferred_element_type=F32))
    o_ref[0] = _fold_lanes(jnp.where(_head_block_mask(rows), of, 0.0)) / l
    if want_lse:
        lse_ref[0] = jnp.broadcast_to(m + jnp.log(l), (rows, LANES))


def _sample_attn(qbd, cache_t, kv_new, dil, sink_col, want_lse):
    n_b, rows, _ = qbd.shape
    n_tok = rows // N_HEADS
    win = cache_t.shape[-1]
    assert win == LOCAL_WINDOW * dil and (dil & (dil - 1)) == 0 and n_tok <= 8
    has_sink = sink_col is not None
    in_specs = [
        pl.BlockSpec((1, rows, KV_COLS), lambda b: (b, 0, 0)),
        pl.BlockSpec((1, 2, KV_COLS, win), lambda b: (b, 0, 0, 0)),
        pl.BlockSpec((1, 8, 2 * KV_COLS), lambda b: (b, 0, 0)),
    ]
    args = [qbd, cache_t, kv_new]
    if has_sink:
        in_specs.append(pl.BlockSpec((rows, 1), lambda b: (0, 0)))
        args.append(sink_col)
    out_spec = pl.BlockSpec((1, rows, LANES), lambda b: (b, 0, 0))
    out_specs = [out_spec]
    out_shape = [jax.ShapeDtypeStruct((n_b, rows, LANES), F32)]
    if want_lse:
        out_specs.append(out_spec)
        out_shape.append(jax.ShapeDtypeStruct((n_b, rows, LANES), F32))
    return pl.pallas_call(
        functools.partial(_sample_attn_kernel, dil=dil, n_tok=n_tok, has_sink=has_sink, want_lse=want_lse),
        grid=(n_b,),
        in_specs=in_specs,
        out_specs=out_specs,
        out_shape=out_shape,
        compiler_params=_params(1),
        name=f"local_attn_sample_d{dil}",
    )(*args)


def _block_mean_kernel(k_ref, o_ref):
    n_blk = k_ref.shape[1] // MOBA_BLOCK
    cols = [jnp.mean(k_ref[:, b * MOBA_BLOCK:(b + 1) * MOBA_BLOCK], axis=1, keepdims=True) for b in range(n_blk)]
    lane = lax.broadcasted_iota(I32, (k_ref.shape[0], LANES), 1)
    out = jnp.zeros((k_ref.shape[0], LANES), F32)
    for b in range(n_blk):
        out = jnp.where(lane == b, cols[b], out)
    o_ref[...] = out


def _block_means(kvt):
    n_batch, _, seq = kvt.shape
    assert seq // MOBA_BLOCK <= LANES
    return pl.pallas_call(
        _block_mean_kernel,
        grid=(n_batch,),
        in_specs=[pl.BlockSpec((None, KV_COLS, seq), lambda n: (n, 0, 0))],
        out_specs=pl.BlockSpec((None, KV_COLS, LANES), lambda n: (n, 0, 0)),
        out_shape=jax.ShapeDtypeStruct((n_batch, KV_COLS, LANES), F32),
        compiler_params=_params(1),
        name="moba_block_means",
    )(kvt)


def _moba_prompt_kernel(q_ref, kv_ref, km_ref, o_ref, *scratch, nq, steps, n_blk):
    t = pl.program_id(0)

    @pl.when(t >= steps)
    def _():
        o_ref[...] = jnp.zeros(o_ref.shape, o_ref.dtype)

    @pl.when(t < steps)
    def _():
        _moba_prompt_step(q_ref, kv_ref, km_ref, o_ref, *scratch, t % nq, n_blk)


def _moba_prompt_step(q_ref, kv_ref, km_ref, o_ref, vt_scr, q4_scr, bias_scr, m_scr, l_scr, acc_scr, i, n_blk):
    own = (i * Q_TILE) // MOBA_BLOCK
    cols = GQA * Q_TILE

    @pl.when(i == 0)
    def _():
        for b in range(n_blk):
            for c in range(KV_COLS // LANES):
                v = kv_ref[b * MOBA_BLOCK:(b + 1) * MOBA_BLOCK, KV_COLS + c * LANES:KV_COLS + (c + 1) * LANES]
                vt_scr[c * LANES:(c + 1) * LANES, b * MOBA_BLOCK:(b + 1) * MOBA_BLOCK] = (
                    jnp.transpose(v.astype(F32)).astype(BF16))

    blk_rows = bias_scr.shape[1]
    bid = lax.broadcasted_iota(I32, (blk_rows, cols), 0)
    q_off = (i * Q_TILE) % MOBA_BLOCK + (lax.broadcasted_iota(I32, (MOBA_BLOCK, cols), 1) & (Q_TILE - 1))
    own_bias = jnp.where(lax.broadcasted_iota(I32, (MOBA_BLOCK, cols), 0) <= q_off, 0.0, NEG)
    km_t = jnp.transpose(km_ref[...])[0:blk_rows, :].astype(BF16)
    for kv in range(N_KV_HEADS):
        q4 = _stack_q_heads(q_ref, kv)
        q4_scr[kv] = q4
        gate = _nt_dot(km_t[:, kv * HEAD_DIM:(kv + 1) * HEAD_DIM], q4)
        gate = jnp.where(bid < own, gate, NEG)
        rank = jnp.zeros((blk_rows, cols), F32)
        for b in range(n_blk):
            gb = gate[b:b + 1, :]
            rank = rank + jnp.where((gb > gate) | ((gb == gate) & (b < bid)), 1.0, 0.0)
        bias_scr[kv] = jnp.where((rank < min(MOBA_TOPK, n_blk)) & (bid < own), 0.0, NEG)
    m_scr[...] = jnp.full(m_scr.shape, NEG, F32)
    l_scr[...] = jnp.zeros(l_scr.shape, F32)
    acc_scr[...] = jnp.zeros(acc_scr.shape, F32)

    def update(kv, start, bias):
        c = kv // 2
        hs = slice((kv % 2) * HEAD_DIM, (kv % 2 + 1) * HEAD_DIM)
        rows = slice(kv * HEAD_DIM, (kv + 1) * HEAD_DIM)
        k = kv_ref[pl.ds(start, MOBA_BLOCK), c * LANES:(c + 1) * LANES][:, hs]
        st = _nt_dot(k, q4_scr[kv]) + bias
        m_old = m_scr[kv, 0:1, :]
        m_new = jnp.maximum(m_old, jnp.max(st, axis=0, keepdims=True))
        pt = jnp.exp(st - m_new)
        alpha = jnp.exp(m_old - m_new)
        pv = jnp.dot(vt_scr[rows, pl.ds(start, MOBA_BLOCK)], pt.astype(BF16), preferred_element_type=F32)
        acc_scr[rows, :] = alpha * acc_scr[rows, :] + pv
        l_scr[kv, 0:1, :] = alpha * l_scr[kv, 0:1, :] + jnp.sum(pt, axis=0, keepdims=True)
        m_scr[kv, 0:1, :] = m_new

    def body(j, carry):
        start = pl.multiple_of(j * MOBA_BLOCK, MOBA_BLOCK)
        for kv in range(N_KV_HEADS):
            update(kv, start, bias_scr[kv, pl.ds(j, 1), :])
        return carry

    lax.fori_loop(0, own, body, 0)
    for kv in range(N_KV_HEADS):
        update(kv, pl.multiple_of(own * MOBA_BLOCK, MOBA_BLOCK), own_bias)
    for kv in range(N_KV_HEADS):
        rows = slice(kv * HEAD_DIM, (kv + 1) * HEAD_DIM)
        acc_scr[rows, :] = acc_scr[rows, :] / l_scr[kv, 0:1, :]
    _store_transposed_heads(o_ref, acc_scr)


def _moba_prompt(q, kvb, kmean, m, n_batch, seq):
    nq = seq // Q_TILE
    n_blk = seq // MOBA_BLOCK
    steps = n_batch * nq
    assert m - steps * Q_TILE <= Q_TILE
    batch_of = lambda t: jnp.minimum(t, steps - 1) // nq
    return pl.pallas_call(
        functools.partial(_moba_prompt_kernel, nq=nq, steps=steps, n_blk=n_blk),
        grid=(steps + 1,),
        in_specs=[
            pl.BlockSpec((Q_TILE, Q_COLS), lambda t: (jnp.minimum(t, steps - 1), 0)),
            pl.BlockSpec((seq, 2 * KV_COLS), lambda t: (batch_of(t), 0)),
            pl.BlockSpec((None, KV_COLS, LANES), lambda t: (batch_of(t), 0, 0)),
        ],
        out_specs=pl.BlockSpec((Q_TILE, Q_COLS), lambda t: (t, 0)),
        out_shape=jax.ShapeDtypeStruct((m, Q_COLS), BF16),
        scratch_shapes=[
            pltpu.VMEM((KV_COLS, seq), BF16),
            pltpu.VMEM((N_KV_HEADS, GQA * Q_TILE, HEAD_DIM), BF16),
            pltpu.VMEM((N_KV_HEADS, -(-n_blk // 8) * 8, GQA * Q_TILE), F32),
            pltpu.VMEM((N_KV_HEADS, 8, GQA * Q_TILE), F32),
            pltpu.VMEM((N_KV_HEADS, 8, GQA * Q_TILE), F32),
            pltpu.VMEM((KV_COLS, GQA * Q_TILE), F32),
        ],
        compiler_params=_params(1),
        name="moba_prompt",
    )(q, kvb, kmean)


def _top_blocks(gate, lane, k):
    chosen = jnp.zeros(gate.shape, F32)
    for _ in range(k):
        mx = jnp.max(gate, axis=1, keepdims=True)
        idx = jnp.min(jnp.where(gate == mx, lane, LANES), axis=1, keepdims=True)
        hit = lane == idx
        chosen = jnp.where(hit & (mx > 0.5 * NEG), 1.0, chosen)
        gate = jnp.where(hit, NEG, gate)
    return chosen


def _moba_sample_kernel(pt_ref, q_ref, n_ref, *refs, n_chunks, n_tok):
    page_refs = refs[:PAGES_PER_STEP]
    o_ref, s_scr, acc_scr, l_scr, m_scr, ch_scr = refs[PAGES_PER_STEP:]
    c = pl.program_id(1)
    rows = q_ref.shape[1]
    step_keys = PAGES_PER_STEP * PAGE_SIZE
    blk_per_step = step_keys // MOBA_BLOCK
    n_blk = n_chunks * blk_per_step
    lane = lax.broadcasted_iota(I32, (rows, LANES), 1)

    @pl.when(c < n_chunks)
    def _scores():
        kt = jnp.concatenate([r[...] for r in page_refs], axis=1).astype(BF16)
        s_scr[:, pl.ds(pl.multiple_of(c * step_keys, step_keys), step_keys)] = (
            jnp.dot(q_ref[0], kt, preferred_element_type=F32))

    @pl.when(c == n_chunks)
    def _select():
        gate = jnp.full((rows, LANES), NEG, F32)
        bmax = jnp.full((rows, LANES), NEG, F32)
        for b in range(n_blk):
            sb = s_scr[:, b * MOBA_BLOCK:(b + 1) * MOBA_BLOCK]
            gate = jnp.where(lane == b, jnp.sum(sb, axis=1, keepdims=True), gate)
            bmax = jnp.where(lane == b, jnp.max(sb, axis=1, keepdims=True), bmax)
        chosen = _top_blocks(gate, lane, min(MOBA_TOPK, n_blk))
        knew, vnew = _padded_new_rows(n_ref)
        tok = lax.broadcasted_iota(I32, (rows, LANES), 0) // N_HEADS
        valid = (lane <= tok) & (lane < n_tok)
        s_new = jnp.where(valid, _nt_dot(q_ref[0], knew), NEG)
        m = jnp.maximum(jnp.max(jnp.where(chosen > 0.5, bmax, NEG), axis=1, keepdims=True),
                        jnp.max(s_new, axis=1, keepdims=True))
        p_new = jnp.where(valid, jnp.exp(s_new - m), 0.0)
        acc_scr[...] = jnp.dot(p_new.astype(BF16), vnew, preferred_element_type=F32)
        l_scr[...] = jnp.sum(p_new, axis=1, keepdims=True)
        m_scr[...] = m
        ch_scr[...] = chosen

    @pl.when(c >= n_chunks)
    def _accumulate():
        cc = c - n_chunks
        vt = jnp.concatenate([r[...] for r in page_refs], axis=1).astype(BF16)
        s = s_scr[:, pl.ds(pl.multiple_of(cc * step_keys, step_keys), step_keys)]
        chosen = ch_scr[...]
        keep = jnp.concatenate(
            [jnp.broadcast_to(jnp.max(jnp.where(lane == cc * blk_per_step + u, chosen, 0.0), axis=1, keepdims=True) > 0.5,
                              (rows, MOBA_BLOCK)) for u in range(blk_per_step)], axis=1)
        p = jnp.where(keep, jnp.exp(s - m_scr[...]), 0.0)
        l_scr[...] += jnp.sum(p, axis=1, keepdims=True)
        acc_scr[...] += _nt_dot(p.astype(BF16), vt)

    @pl.when(c == 2 * n_chunks - 1)
    def _finish():
        o_ref[0] = _fold_lanes(jnp.where(_head_block_mask(rows), acc_scr[...], 0.0)) / l_scr[...]


def _moba_sample(qbd, kv_new, cache_t, page_table):
    n_b, rows, _ = qbd.shape
    n_tok = rows // N_HEADS
    n_pages = page_table.shape[1]
    assert n_pages % PAGES_PER_STEP == 0 and (n_pages * PAGE_SIZE) % MOBA_BLOCK == 0
    assert n_pages * PAGE_SIZE // MOBA_BLOCK <= LANES
    n_chunks = n_pages // PAGES_PER_STEP

    def page_map(u):
        return lambda b, c, pt: (pt[b, (c % n_chunks) * PAGES_PER_STEP + u], c // n_chunks, 0, 0)

    page_specs = [pl.BlockSpec((None, None, KV_COLS, PAGE_SIZE), page_map(u)) for u in range(PAGES_PER_STEP)]
    grid_spec = pltpu.PrefetchScalarGridSpec(
        num_scalar_prefetch=1,
        grid=(n_b, 2 * n_chunks),
        in_specs=[
            pl.BlockSpec((1, rows, KV_COLS), lambda b, c, pt: (b, 0, 0)),
            pl.BlockSpec((1, 8, 2 * KV_COLS), lambda b, c, pt: (b, 0, 0)),
        ] + page_specs,
        out_specs=pl.BlockSpec((1, rows, LANES), lambda b, c, pt: (b, 0, 0)),
        scratch_shapes=[
            pltpu.VMEM((rows, n_pages * PAGE_SIZE), F32),
            pltpu.VMEM((rows, KV_COLS), F32),
            pltpu.VMEM((rows, 1), F32),
            pltpu.VMEM((rows, 1), F32),
            pltpu.VMEM((rows, LANES), F32),
        ],
    )
    return pl.pallas_call(
        functools.partial(_moba_sample_kernel, n_chunks=n_chunks, n_tok=n_tok),
        grid_spec=grid_spec,
        out_shape=jax.ShapeDtypeStruct((n_b, rows, LANES), F32),
        compiler_params=_params(2),
        name="moba_sample",
    )(page_table, qbd, kv_new, *([cache_t] * PAGES_PER_STEP))


def _layer_norm(h, gain, bias):
    mu = jnp.mean(h, axis=-1, keepdims=True)
    d = h - mu
    var = jnp.mean(d * d, axis=-1, keepdims=True)
    return d * lax.rsqrt(var + LN_EPS) * gain + bias


def _oproj_ln_kernel(*refs, n_merge):
    x_ref = refs[0]
    o_refs = refs[1:1 + n_merge]
    w_ref, g_ref, b_ref, out_ref, rows_ref = refs[-5:]
    if n_merge == 1:
        o = o_refs[0][...]
    else:
        lses = [r[...] for r in refs[1 + n_merge:1 + 2 * n_merge]]
        mx = functools.reduce(jnp.maximum, lses)
        es = [jnp.exp(l - mx) for l in lses]
        num = sum(e * r[...].astype(F32) for r, e in zip(o_refs, es))
        o = (num / sum(es)).astype(BF16)
    y = jnp.dot(o, w_ref[...], preferred_element_type=F32)
    h = _layer_norm(DEEPNORM_ALPHA * x_ref[...] + y, g_ref[...], b_ref[...])
    out_ref[...] = h
    for j in range(D_MODEL // LANES):
        rows_ref[:, j, :] = h[:, j * LANES:(j + 1) * LANES]


def _oproj_ln(x, o, lse, w_bf16, gain, bias):
    m = x.shape[0]
    n_merge = len(o)
    tm = _row_tile(m, 320 if n_merge == 1 else 160, 16)
    row = pl.BlockSpec((tm, D_MODEL), lambda i: (i, 0))
    vec = pl.BlockSpec((1, D_MODEL), lambda i: (0, 0))
    wspec = pl.BlockSpec((Q_COLS, D_MODEL), lambda i: (0, 0))
    groups = list(o) + (list(lse) if n_merge > 1 else [])
    in_specs = [row] + [pl.BlockSpec((tm, Q_COLS), lambda i: (i, 0))] * len(groups) + [wspec, vec, vec]
    args = [x] + groups + [w_bf16, gain, bias]
    return pl.pallas_call(
        functools.partial(_oproj_ln_kernel, n_merge=n_merge),
        grid=(m // tm,),
        in_specs=in_specs,
        out_specs=[row, pl.BlockSpec((tm, D_MODEL // LANES, LANES), lambda i: (i, 0, 0))],
        out_shape=[jax.ShapeDtypeStruct((m, D_MODEL), F32),
                   jax.ShapeDtypeStruct((m, D_MODEL // LANES, LANES), F32)],
        compiler_params=_params(1),
        name=f"oproj_ln_m{n_merge}",
    )(*args)


def _router_kernel(x_ref, rwt_ref, bias_ref, idx_ref, gate_ref, cnt_ref, carry_scr):
    i = pl.program_id(0)
    tm = x_ref.shape[0]

    @pl.when(i == 0)
    def _():
        carry_scr[...] = jnp.zeros(carry_scr.shape, F32)

    logits = _nt_dot(rwt_ref[...], x_ref[...].astype(BF16))
    scores = 1.0 / (1.0 + jnp.exp(-logits))
    biased = scores + bias_ref[...]
    srow = [scores[e:e + 1, :] for e in range(N_EXPERTS)]
    brow = [biased[e:e + 1, :] for e in range(N_EXPERTS)]

    gscore = []
    for g in range(N_EXPERT_GROUPS):
        a0, a1, a2, a3 = brow[4 * g:4 * g + 4]
        hi01, lo01 = jnp.maximum(a0, a1), jnp.minimum(a0, a1)
        hi23, lo23 = jnp.maximum(a2, a3), jnp.minimum(a2, a3)
        gscore.append(jnp.maximum(hi01, hi23) + jnp.maximum(jnp.minimum(hi01, hi23), jnp.maximum(lo01, lo23)))
    gsel = jnp.zeros((1, tm), I32)
    best = gscore[0]
    for g in range(1, N_EXPERT_GROUPS):
        better = gscore[g] > best
        gsel = jnp.where(better, g, gsel)
        best = jnp.where(better, gscore[g], best)

    def in_group(rows_, j):
        v = rows_[j]
        for g in range(1, N_EXPERT_GROUPS):
            v = jnp.where(gsel == g, rows_[4 * g + j], v)
        return v

    ab = [in_group(brow, j) for j in range(EXPERTS_PER_GROUP)]
    au = [in_group(srow, j) for j in range(EXPERTS_PER_GROUP)]
    i1 = jnp.zeros((1, tm), I32)
    v1 = ab[0]
    for j in range(1, EXPERTS_PER_GROUP):
        better = ab[j] > v1
        i1 = jnp.where(better, j, i1)
        v1 = jnp.where(better, ab[j], v1)
    i2 = jnp.full((1, tm), -1, I32)
    v2 = jnp.full((1, tm), -jnp.inf, F32)
    for j in range(EXPERTS_PER_GROUP):
        better = (i1 != j) & ((ab[j] > v2) | (i2 < 0))
        i2 = jnp.where(better, j, i2)
        v2 = jnp.where(better, ab[j], v2)

    def pick(vals, idx):
        v = vals[0]
        for j in range(1, EXPERTS_PER_GROUP):
            v = jnp.where(idx == j, vals[j], v)
        return v

    s1, s2 = pick(au, i1), pick(au, i2)
    den = s1 + s2
    e1 = gsel * EXPERTS_PER_GROUP + i1
    e2 = gsel * EXPERTS_PER_GROUP + i2

    eid = lax.broadcasted_iota(I32, (N_EXPERTS, tm), 0)
    hit1, hit2 = eid == e1, eid == e2
    onehot = jnp.where(hit1 | hit2, 1.0, 0.0)
    before = lax.broadcasted_iota(I32, (tm, tm), 0) < lax.broadcasted_iota(I32, (tm, tm), 1)
    prefix = jnp.dot(onehot.astype(BF16), jnp.where(before, 1.0, 0.0).astype(BF16), preferred_element_type=F32)
    offset = carry_scr[...][:, 0:1] + prefix
    r1 = jnp.sum(jnp.where(hit1, offset, 0.0), axis=0, keepdims=True).astype(I32)
    r2 = jnp.sum(jnp.where(hit2, offset, 0.0), axis=0, keepdims=True).astype(I32)
    carry_scr[...] = carry_scr[...] + jnp.sum(onehot, axis=1, keepdims=True)

    row8 = lax.broadcasted_iota(I32, (8, tm), 0)
    idx_ref[...] = jnp.where(row8 == 0, e1, jnp.where(row8 == 1, e2, jnp.where(row8 == 2, r1, jnp.where(row8 == 3, r2, 0))))
    gate_ref[...] = jnp.where(row8 == 0, s1 / den, jnp.where(row8 == 1, s2 / den, 0.0))
    cnt_ref[...] = carry_scr[...]


def _router(x, rwt_bf16, bias_col):
    m = x.shape[0]
    tm = _row_tile(m, 640, LANES)
    return pl.pallas_call(
        _router_kernel,
        grid=(m // tm,),
        in_specs=[
            pl.BlockSpec((tm, D_MODEL), lambda i: (i, 0)),
            pl.BlockSpec((N_EXPERTS, D_MODEL), lambda i: (0, 0)),
            pl.BlockSpec((N_EXPERTS, 1), lambda i: (0, 0)),
        ],
        out_specs=[
            pl.BlockSpec((8, tm), lambda i: (0, i)),
            pl.BlockSpec((8, tm), lambda i: (0, i)),
            pl.BlockSpec((N_EXPERTS, LANES), lambda i: (0, 0)),
        ],
        out_shape=[
            jax.ShapeDtypeStruct((8, m), I32),
            jax.ShapeDtypeStruct((8, m), F32),
            jax.ShapeDtypeStruct((N_EXPERTS, LANES), F32),
        ],
        scratch_shapes=[pltpu.VMEM((N_EXPERTS, LANES), F32)],
        compiler_params=_params(1),
        name="router",
    )(x, rwt_bf16, bias_col)


def _start_row_gather(src_hbm, row_of, n_rows, dst, sem):
    def body(r, carry):
        pltpu.make_async_copy(src_hbm.at[pl.ds(row_of(r), 1)], dst.at[pl.ds(r, 1)], sem).start()
        return carry
    lax.fori_loop(0, n_rows, body, 0, unroll=8)


def _wait_row_gather(src_hbm, n_rows, dst, sem):
    pltpu.make_async_copy(src_hbm.at[pl.ds(0, n_rows)], dst, sem).wait()


def _start_row_gather_inline(src_hbm, idx_ref, base, n_rows, dst, sem):
    for r in range(n_rows):
        pltpu.make_async_copy(src_hbm.at[pl.ds(idx_ref[base + r], 1)], dst.at[pl.ds(r, 1)], sem).start()


def _moe_ffn_kernel(ce_ref, tok_ref, x_hbm, wg_ref, wu_ref, wd_ref, y_ref, xbuf0, xbuf1, sem, wgb, wub, wdb, *,
                    n_chunks):
    c = pl.program_id(0)
    bufs = (xbuf0, xbuf1)

    @pl.when(c == 0)
    def _():
        _start_row_gather(x_hbm, lambda r: tok_ref[r], MOE_CHUNK, xbuf0, sem.at[0])

    @pl.when((c == 0) | (ce_ref[c] != ce_ref[jnp.maximum(c - 1, 0)]))
    def _():
        wgb[...] = wg_ref[...].astype(BF16)
        wub[...] = wu_ref[...].astype(BF16)
        wdb[...] = wd_ref[...].astype(BF16)

    def run(cur):
        _wait_row_gather(x_hbm, MOE_CHUNK, bufs[cur], sem.at[cur])
        _start_row_gather_inline(x_hbm, tok_ref, (c + 1) * MOE_CHUNK, MOE_CHUNK, bufs[1 - cur], sem.at[1 - cur])
        xb = jnp.concatenate([bufs[cur][:, j, :] for j in range(D_MODEL // LANES)], axis=1).astype(BF16)
        a = jnp.dot(xb, wgb[...], preferred_element_type=F32)
        u = jnp.dot(xb, wub[...], preferred_element_type=F32)
        h = (a / (1.0 + jnp.exp(-a))) * u
        y_ref[...] = jnp.dot(h.astype(BF16), wdb[...], preferred_element_type=F32)

    @pl.when(c % 2 == 0)
    def _():
        run(0)

    @pl.when(c % 2 == 1)
    def _():
        run(1)

    @pl.when(c == n_chunks - 1)
    def _():
        _wait_row_gather(x_hbm, MOE_CHUNK, bufs[n_chunks % 2], sem.at[n_chunks % 2])


def _moe_ffn(x, chunk_expert, tok_of_row, w_gate, w_up, w_down, layer):
    n_chunks = chunk_expert.shape[0]
    assert tok_of_row.shape[0] == (n_chunks + 1) * MOE_CHUNK
    w_in = pl.BlockSpec((None, None, D_MODEL, D_EXPERT), lambda c, ce, tok: (layer, ce[c], 0, 0))
    grid_spec = pltpu.PrefetchScalarGridSpec(
        num_scalar_prefetch=2,
        grid=(n_chunks,),
        in_specs=[
            pl.BlockSpec(memory_space=pl.ANY),
            w_in,
            w_in,
            pl.BlockSpec((None, None, D_EXPERT, D_MODEL), lambda c, ce, tok: (layer, ce[c], 0, 0)),
        ],
        out_specs=pl.BlockSpec((MOE_CHUNK, D_MODEL), lambda c, ce, tok: (c, 0)),
        scratch_shapes=[
            pltpu.VMEM((MOE_CHUNK, D_MODEL // LANES, LANES), F32),
            pltpu.VMEM((MOE_CHUNK, D_MODEL // LANES, LANES), F32),
            pltpu.SemaphoreType.DMA((2,)),
            pltpu.VMEM((D_MODEL, D_EXPERT), BF16),
            pltpu.VMEM((D_MODEL, D_EXPERT), BF16),
            pltpu.VMEM((D_EXPERT, D_MODEL), BF16),
        ],
    )
    return pl.pallas_call(
        functools.partial(_moe_ffn_kernel, n_chunks=n_chunks),
        grid_spec=grid_spec,
        out_shape=jax.ShapeDtypeStruct((n_chunks * MOE_CHUNK, D_MODEL), F32),
        compiler_params=_params(1),
        name="moe_ffn",
    )(chunk_expert, tok_of_row, x, w_gate, w_up, w_down)


def _combine_ln_kernel(dest_ref, y_hbm, x_ref, g1_ref, g2_ref, gain_ref, bias_ref, out_ref, ybuf0, ybuf1, sem, *,
                       n_tiles):
    i = pl.program_id(0)
    tm = x_ref.shape[0]
    bufs = (ybuf0, ybuf1)

    @pl.when(i == 0)
    def _():
        _start_row_gather(y_hbm, lambda r: dest_ref[r], 2 * tm, ybuf0, sem.at[0])

    def run(cur):
        _wait_row_gather(y_hbm, 2 * tm, bufs[cur], sem.at[cur])
        _start_row_gather_inline(y_hbm, dest_ref, (i + 1) * 2 * tm, 2 * tm, bufs[1 - cur], sem.at[1 - cur])
        f = g1_ref[...] * bufs[cur][0:tm, :] + g2_ref[...] * bufs[cur][tm:2 * tm, :]
        out_ref[...] = _layer_norm(DEEPNORM_ALPHA * x_ref[...] + f, gain_ref[...], bias_ref[...])

    @pl.when(i % 2 == 0)
    def _():
        run(0)

    @pl.when(i % 2 == 1)
    def _():
        run(1)

    @pl.when(i == n_tiles - 1)
    def _():
        _wait_row_gather(y_hbm, 2 * tm, bufs[n_tiles % 2], sem.at[n_tiles % 2])


def _combine_ln(x, ybuf, dest, g1, g2, gain, bias):
    m = x.shape[0]
    tm = _row_tile(m, 128, 8)
    n_tiles = m // tm
    dest_tiles = dest.reshape(2, n_tiles, tm).transpose(1, 0, 2).reshape(-1)
    dest_tiles = jnp.concatenate([dest_tiles, jnp.zeros((2 * tm,), I32)])
    row = pl.BlockSpec((tm, D_MODEL), lambda i, d: (i, 0))
    col = pl.BlockSpec((tm, 1), lambda i, d: (i, 0))
    vec = pl.BlockSpec((1, D_MODEL), lambda i, d: (0, 0))
    grid_spec = pltpu.PrefetchScalarGridSpec(
        num_scalar_prefetch=1,
        grid=(n_tiles,),
        in_specs=[pl.BlockSpec(memory_space=pl.ANY), row, col, col, vec, vec],
        out_specs=row,
        scratch_shapes=[pltpu.VMEM((2 * tm, D_MODEL), F32), pltpu.VMEM((2 * tm, D_MODEL), F32),
                        pltpu.SemaphoreType.DMA((2,))],
    )
    return pl.pallas_call(
        functools.partial(_combine_ln_kernel, n_tiles=n_tiles),
        grid_spec=grid_spec,
        out_shape=jax.ShapeDtypeStruct((m, D_MODEL), F32),
        compiler_params=_params(1),
        name="moe_combine_ln",
    )(dest_tiles, ybuf, x, g1, g2, gain, bias)


def _moe_layer(x, x_rows, rwt_bf16, rbias_col, w_gate, w_up, w_down, layer, gain, bias):
    m = x.shape[0]
    idx, gates, counts = _router(x, rwt_bf16, rbias_col)
    experts, ranks = idx[0:2], idx[2:4]
    cnt = counts[:, 0].astype(I32)
    padded = (cnt + MOE_CHUNK - 1) // MOE_CHUNK * MOE_CHUNK
    pend = jnp.cumsum(padded)
    pstart = pend - padded
    eid = jnp.arange(N_EXPERTS, dtype=I32)[:, None, None]
    dest = ranks + jnp.sum(jnp.where(experts[None] == eid, pstart[:, None, None], 0), axis=0)
    n_chunks = -(-2 * m // MOE_CHUNK) + N_EXPERTS
    chunk_start = jnp.arange(n_chunks, dtype=I32) * MOE_CHUNK
    chunk_expert = jnp.minimum(jnp.sum((pend[None, :] <= chunk_start[:, None]).astype(I32), axis=1), N_EXPERTS - 1)
    tok = jnp.tile(jnp.arange(m, dtype=I32), 2)
    tok_of_row = jnp.zeros(((n_chunks + 1) * MOE_CHUNK,), I32).at[dest.reshape(-1)].set(tok)
    ybuf = _moe_ffn(x_rows, chunk_expert, tok_of_row, w_gate, w_up, w_down, layer)
    return _combine_ln(x, ybuf, dest, gates[0].reshape(m, 1), gates[1].reshape(m, 1), gain, bias)


def _block_diag_queries(q_rows, n_b, n_tok):
    q4 = q_rows.reshape(n_b, n_tok, N_HEADS, HEAD_DIM)
    tiled = jnp.tile(q4, (1, 1, 1, N_KV_HEADS))
    head = jnp.arange(N_HEADS)[:, None] // GQA
    lane_kv = jnp.arange(KV_COLS)[None, :] // HEAD_DIM
    return jnp.where(head == lane_kv, tiled, jnp.zeros((), tiled.dtype)).reshape(n_b, n_tok * N_HEADS, KV_COLS)


def _sample_rows_grouped(o, n_b, n_tok):
    o = o.reshape(n_b, n_tok, N_KV_HEADS, GQA, LANES)
    return o.transpose(0, 1, 3, 2, 4)


def _sample_out(o, n_b, n_tok):
    o = _sample_rows_grouped(o, n_b, n_tok)
    return (o[..., :HEAD_DIM] + o[..., HEAD_DIM:]).reshape(n_b * n_tok, Q_COLS)


def _sample_lse(lse, n_b, n_tok):
    return _sample_rows_grouped(lse, n_b, n_tok)[..., :HEAD_DIM].reshape(n_b * n_tok, Q_COLS)


def _new_kv_rows(kv_s, n_b, n_tok):
    return jnp.pad(kv_s.reshape(n_b, n_tok, 2 * KV_COLS), ((0, 0), (0, 8 - n_tok), (0, 0)))


def _position_minor(cache):
    lead = cache.shape[:-4]
    n = len(lead)
    t = cache.transpose(tuple(range(n)) + (n + 1, n + 2, n + 3, n))
    return t.reshape(lead + (2, KV_COLS, cache.shape[-4]))


def _position_major(kvt):
    lead = kvt.shape[:-2]
    n = len(lead)
    t = kvt.reshape(lead + (2, N_KV_HEADS, HEAD_DIM, kvt.shape[-1]))
    return t.transpose(tuple(range(n)) + (n + 3, n, n + 1, n + 2))


def _window_state(kvt, kv_s, cache, n_b, n_tok, window):
    seq = kvt.shape[-1]
    state_p = _position_major(kvt[..., seq - min(window, seq):])
    kvs = kv_s.reshape(n_b, n_tok, 2, N_KV_HEADS, HEAD_DIM)
    state_s = jnp.concatenate([cache, kvs], axis=1)[:, -cache.shape[1]:]
    return state_p, state_s


def kernel(x_prompt, x_sample, cache_swa, cache_dil0, cache_dil1, cache_dil2, cache_moba, page_table,
           w_qkv_swa, w_o_swa, sinks_swa, w_qkv_dil, w_o_dil, w_qkv_moba, w_o_moba,
           ln_gain, ln_bias, router_w, router_bias, w_gate_e, w_up_e, w_down_e):
    n_batch, seq, d = x_prompt.shape
    n_b, n_tok = x_sample.shape[:2]
    past_len = page_table.shape[1] * cache_moba.shape[2]
    mp, ms = n_batch * seq, n_b * n_tok
    m = mp + ms
    assert d == D_MODEL and cache_moba.shape[2] == PAGE_SIZE and n_tok <= 4 and ms == Q_TILE and mp % ms == 0
    x = jnp.concatenate([x_prompt.reshape(mp, d), x_sample.reshape(ms, d)], axis=0)
    pos = jnp.concatenate([jnp.tile(jnp.arange(seq), n_batch), jnp.tile(past_len + jnp.arange(n_tok), n_b)])
    cos, sin = _rope_tables(pos)
    rwt = router_w.T.astype(BF16)
    rbias = router_bias.astype(F32).reshape(N_EXPERTS, 1)
    dil_caches = (cache_dil0, cache_dil1, cache_dil2)

    def project(w_bf16, g):
        q, kvb, kvt = _qkv_rope(x, w_bf16, cos, sin, g, 1, 0, mp, seq)
        q_s, kv_s = _qkv_rope(x, w_bf16, cos, sin, g, 1, mp, ms, None)
        return q, kvb, kvt, q_s, kv_s

    def sample_local(q_s, kv_s, cache, dil, sink_col, want_lse):
        outs = _sample_attn(_block_diag_queries(q_s, n_b, n_tok), _position_minor(cache),
                            _new_kv_rows(kv_s, n_b, n_tok), dil, sink_col, want_lse)
        return _sample_out(outs[0], n_b, n_tok), (_sample_lse(outs[1], n_b, n_tok) if want_lse else None)

    swa_p, swa_s, moba_p, moba_s = [], [], [], []
    dil_p, dil_s = ([], [], []), ([], [], [])
    for i in range(DEPTH):
        kind, j = i % 3, i // 3
        gain1, bias1 = ln_gain[i, 0].reshape(1, d), ln_bias[i, 0].reshape(1, d)
        gain2, bias2 = ln_gain[i, 1].reshape(1, d), ln_bias[i, 1].reshape(1, d)
        if kind == 0:
            q, kvb, kvt, q_s, kv_s = project(w_qkv_swa[j].astype(BF16), 0)
            sinks = sinks_swa[j].astype(F32)
            sink_rows = jnp.repeat(sinks, Q_TILE).reshape(N_KV_HEADS, GQA * Q_TILE)
            (o,) = _local_attn_prompt(q, kvb, m, 0, 1, n_batch, seq, sink_rows, False)
            sink_col = jnp.tile(sinks, n_tok).reshape(n_tok * N_HEADS, 1)
            o_s, _ = sample_local(q_s[0], kv_s[0], cache_swa[j], 1, sink_col, False)
            o = lax.dynamic_update_slice(o, o_s.astype(BF16), (mp, 0))
            sp, ss = _window_state(kvt[0], kv_s[0], cache_swa[j], n_b, n_tok, SWA_WINDOW)
            swa_p.append(sp)
            swa_s.append(ss)
            x, x_rows = _oproj_ln(x, [o], None, _grouped_head_order(w_o_swa[j]).astype(BF16), gain1, bias1)
        elif kind == 1:
            w_dil = w_qkv_dil[j].astype(BF16)
            os_, lses = [], []
            for g, (win, dil) in enumerate(DIL_PAIRS):
                q, kvb, kvt, q_s, kv_s = project(w_dil, g)
                o, lse = _local_attn_prompt(q, kvb, m, 0, dil, n_batch, seq, None, True)
                o_s, lse_s = sample_local(q_s[0], kv_s[0], dil_caches[g][j], dil, None, True)
                os_.append(lax.dynamic_update_slice(o, o_s.astype(BF16), (mp, 0)))
                lses.append(lax.dynamic_update_slice(lse, lse_s, (mp, 0)))
                sp, ss = _window_state(kvt[0], kv_s[0], dil_caches[g][j], n_b, n_tok, win)
                dil_p[g].append(sp)
                dil_s[g].append(ss)
            x, x_rows = _oproj_ln(x, os_, lses, _grouped_head_order(w_o_dil[j]).astype(BF16), gain1, bias1)
        else:
            q, kvb, kvt, q_s, kv_s = project(w_qkv_moba[j].astype(BF16), 0)
            o = _moba_prompt(q[0], kvb[0], _block_means(kvt[0]), m, n_batch, seq)
            o_s = _moba_sample(_block_diag_queries(q_s[0], n_b, n_tok), _new_kv_rows(kv_s[0], n_b, n_tok),
                               _position_minor(cache_moba[j]), page_table)
            o = lax.dynamic_update_slice(o, _sample_out(o_s, n_b, n_tok).astype(BF16), (mp, 0))
            moba_p.append(_position_major(kvt[0]))
            moba_s.append(kv_s[0].reshape(n_b, n_tok, 2, N_KV_HEADS, HEAD_DIM))
            x, x_rows = _oproj_ln(x, [o], None, _grouped_head_order(w_o_moba[j]).astype(BF16), gain1, bias1)
        x = _moe_layer(x, x_rows, rwt, rbias, w_gate_e, w_up_e, w_down_e, i, gain2, bias2)
    y_prompt = x[:mp].reshape(n_batch, seq, d)
    y_sample = x[mp:].reshape(n_b, n_tok, d)
    return (y_prompt, y_sample, jnp.stack(swa_p), jnp.stack(swa_s),
            jnp.stack(dil_p[0]), jnp.stack(dil_s[0]), jnp.stack(dil_p[1]), jnp.stack(dil_s[1]),
            jnp.stack(dil_p[2]), jnp.stack(dil_s[2]), jnp.stack(moba_p), jnp.stack(moba_s))
```

```python
import functools
import math

import jax
import jax.numpy as jnp
from jax import lax
from jax.experimental import pallas as pl
from jax.experimental.pallas import tpu as pltpu

F32, BF16, I32 = jnp.float32, jnp.bfloat16, jnp.int32

D_MODEL = 2048
HEAD_DIM = 64
N_HEADS = 32
N_KV_HEADS = 8
GQA = N_HEADS // N_KV_HEADS
Q_COLS = N_HEADS * HEAD_DIM
KV_COLS = N_KV_HEADS * HEAD_DIM
QKV_COLS = Q_COLS + 2 * KV_COLS
ATTN_SCALE = HEAD_DIM ** -0.5
ROPE_THETA = 10000.0
SWA_WINDOW = 128
LOCAL_WINDOW = 128
DIL_PAIRS = ((128, 1), (512, 4), (2048, 16))
MOBA_BLOCK = 256
MOBA_TOPK = 3
PAGE_SIZE = 128
N_EXPERTS = 16
N_EXPERT_GROUPS = 4
EXPERTS_PER_GROUP = 4
D_EXPERT = D_MODEL // 4
DEPTH = 4
DEEPNORM_ALPHA = (2 * DEPTH) ** 0.25
LN_EPS = 1e-5
NEG = -1e30

LANES = 128
Q_TILE = 128
QKV_COL_TILE = 512
QKV_ROW_TILE = 512
MOE_CHUNK = 256
PAGES_PER_STEP = 32
VMEM_LIMIT = 56 * 1024 * 1024


def _params(n_axes, vmem=VMEM_LIMIT):
    return pltpu.CompilerParams(dimension_semantics=("arbitrary",) * n_axes, vmem_limit_bytes=vmem)


def _nt_dot(a, b):
    return lax.dot_general(a, b, (((1,), (1,)), ((), ())), preferred_element_type=F32)


def _row_tile(m, pref, align):
    best = m
    for t in range(align, min(pref, m) + 1, align):
        if m % t == 0:
            best = t
    return best


def _rope_tables(pos):
    half = HEAD_DIM // 2
    inv_freq = jnp.exp(jnp.arange(half, dtype=F32) * (-2.0 * math.log(ROPE_THETA) / HEAD_DIM))
    ang = pos.astype(F32)[:, None] * inv_freq[None, :]
    cos, sin = jnp.cos(ang), jnp.sin(ang)
    return jnp.tile(cos, (1, 4)), jnp.concatenate([-sin, sin, -sin, sin], axis=1)


def _qkv_rope_kernel(x_ref, w_ref, cos_ref, sin_ref, q_ref, kv_ref, *rest, fold):
    kvt_ref = rest[0] if rest else None
    slab_ref = rest[1] if fold > 1 else None
    xb = x_ref[...].astype(BF16)
    tm = xb.shape[0]
    lane = lax.broadcasted_iota(I32, (tm, LANES), 1)
    first_half = (lane & (HEAD_DIM - 1)) < HEAD_DIM // 2

    def put(ref, col0, val, slab0):
        if fold == 1:
            ref[:, col0:col0 + val.shape[1]] = val.astype(ref.dtype)
            return
        for c in range(val.shape[1] // LANES):
            slab_ref[slab0 + c] = val[:, c * LANES:(c + 1) * LANES]
            for r in range(fold):
                piece = slab_ref[slab0 + c, pl.ds(r, tm // fold, stride=fold), :]
                ref[r, :, col0 + c * LANES:col0 + (c + 1) * LANES] = piece.astype(ref.dtype)

    def rope(z):
        cos, sin = cos_ref[...], sin_ref[...]
        parts = []
        for c in range(z.shape[1] // LANES):
            zc = z[:, c * LANES:(c + 1) * LANES]
            other = jnp.where(first_half, pltpu.roll(zc, LANES - HEAD_DIM // 2, 1), pltpu.roll(zc, HEAD_DIM // 2, 1))
            parts.append(zc * cos + other * sin)
        return jnp.concatenate(parts, axis=1)

    n_q = Q_COLS // QKV_COL_TILE
    slabs_per_tile = QKV_COL_TILE // LANES
    for j in range(QKV_COLS // QKV_COL_TILE):
        cols = slice(j * QKV_COL_TILE, (j + 1) * QKV_COL_TILE)
        z = jnp.dot(xb, w_ref[:, cols], preferred_element_type=F32)
        if j < n_q:
            put(q_ref, j * QKV_COL_TILE, rope(z) * ATTN_SCALE, j * slabs_per_tile)
        else:
            val = rope(z) if j == n_q else z
            put(kv_ref, (j - n_q) * QKV_COL_TILE, val, j * slabs_per_tile)
            if kvt_ref is not None:
                kvt_ref[(j - n_q) * QKV_COL_TILE:(j - n_q + 1) * QKV_COL_TILE, :] = val.T


def _qkv_rope(x, w_bf16, cos, sin, g0, n_groups, row0, n_rows, seq, fold=1):
    tm = QKV_ROW_TILE if seq is not None else n_rows
    assert n_rows % tm == 0 and row0 % tm == 0 and (seq is None or seq % tm == 0)
    assert fold == 1 or (seq is not None and tm % (16 * fold) == 0)
    blk0 = row0 // tm
    scratch = []
    if fold == 1:
        out_specs = [
            pl.BlockSpec((None, tm, Q_COLS), lambda g, i: (g, i, 0)),
            pl.BlockSpec((None, tm, 2 * KV_COLS), lambda g, i: (g, i, 0)),
        ]
        out_shape = [
            jax.ShapeDtypeStruct((n_groups, n_rows, Q_COLS), BF16),
            jax.ShapeDtypeStruct((n_groups, n_rows, 2 * KV_COLS), BF16 if seq is not None else F32),
        ]
    else:
        per = seq // tm
        folded = lambda g, i: (g, i // per, 0, i % per, 0)
        out_specs = [
            pl.BlockSpec((None, None, fold, tm // fold, Q_COLS), folded),
            pl.BlockSpec((None, None, fold, tm // fold, 2 * KV_COLS), folded),
        ]
        out_shape = [
            jax.ShapeDtypeStruct((n_groups, n_rows // seq, fold, seq // fold, Q_COLS), BF16),
            jax.ShapeDtypeStruct((n_groups, n_rows // seq, fold, seq // fold, 2 * KV_COLS), BF16),
        ]
        scratch = [pltpu.VMEM((QKV_COLS // LANES, tm, LANES), F32)]
    if seq is not None:
        per = seq // tm
        out_specs.append(pl.BlockSpec((None, None, 2 * KV_COLS, tm), lambda g, i: (g, i // per, 0, i % per)))
        out_shape.append(jax.ShapeDtypeStruct((n_groups, n_rows // seq, 2 * KV_COLS, seq), F32))
    outs = pl.pallas_call(
        functools.partial(_qkv_rope_kernel, fold=fold),
        grid=(n_groups, n_rows // tm),
        in_specs=[
            pl.BlockSpec((tm, D_MODEL), lambda g, i: (blk0 + i, 0)),
            pl.BlockSpec((D_MODEL, QKV_COLS), lambda g, i: (0, g0 + g)),
            pl.BlockSpec((tm, LANES), lambda g, i: (blk0 + i, 0)),
            pl.BlockSpec((tm, LANES), lambda g, i: (blk0 + i, 0)),
        ],
        out_specs=out_specs,
        out_shape=out_shape,
        scratch_shapes=scratch,
        compiler_params=_params(2),
        name="qkv_rope_prompt" if seq is not None else "qkv_rope_sample",
    )(x, w_bf16, cos, sin)
    if fold > 1:
        outs = [outs[0].reshape(n_groups, n_rows, Q_COLS), outs[1].reshape(n_groups, n_rows, 2 * KV_COLS), outs[2]]
    return outs


def _stack_q_heads(q_ref, kv):
    qa = q_ref[:, (2 * kv) * LANES:(2 * kv + 1) * LANES]
    qb = q_ref[:, (2 * kv + 1) * LANES:(2 * kv + 2) * LANES]
    return jnp.concatenate([qa[:, :HEAD_DIM], qa[:, HEAD_DIM:], qb[:, :HEAD_DIM], qb[:, HEAD_DIM:]], axis=0)


def _store_transposed_heads(ref, t_scr):
    for g in range(GQA):
        ref[:, g * KV_COLS:(g + 1) * KV_COLS] = jnp.transpose(t_scr[:, g * Q_TILE:(g + 1) * Q_TILE]).astype(ref.dtype)


def _grouped_head_order(w_o):
    return w_o.reshape(N_KV_HEADS, GQA, HEAD_DIM, -1).transpose(1, 0, 2, 3).reshape(Q_COLS, -1)


def _local_attn_kernel(*refs, has_sink, want_lse, nb, steps):
    q_ref, kc_ref, kp_ref = refs[:3]
    pos = 3
    sink_ref = None
    if has_sink:
        sink_ref = refs[pos]
        pos += 1
    o_ref = refs[pos]
    pos += 1
    lse_ref = None
    if want_lse:
        lse_ref = refs[pos]
        pos += 1
    ot_scr = refs[pos]
    lt_scr = refs[pos + 1] if want_lse else None
    t = pl.program_id(0)

    @pl.when(t >= steps)
    def _():
        o_ref[...] = jnp.zeros(o_ref.shape, o_ref.dtype)
        if want_lse:
            lse_ref[...] = jnp.zeros(lse_ref.shape, lse_ref.dtype)

    @pl.when(t < steps)
    def _():
        _local_attn_step(q_ref, kc_ref, kp_ref, sink_ref, o_ref, lse_ref, ot_scr, lt_scr, t % nb)


def _local_attn_step(q_ref, kc_ref, kp_ref, sink_ref, o_ref, lse_ref, ot_scr, lt_scr, i):
    w = LOCAL_WINDOW
    cols = GQA * Q_TILE
    ci = lax.broadcasted_iota(I32, (2 * w, cols), 0)
    qi = lax.broadcasted_iota(I32, (2 * w, cols), 1) & (Q_TILE - 1)
    delta = qi + w - ci
    mask = (delta >= 0) & (delta <= w) & ((ci >= w) | (i > 0))
    for c in range(N_KV_HEADS // 2):
        ksl = slice(c * LANES, (c + 1) * LANES)
        vsl = slice(KV_COLS + c * LANES, KV_COLS + (c + 1) * LANES)
        kpair = jnp.concatenate([kp_ref[:, ksl], kc_ref[:, ksl]], axis=0)
        vpair = jnp.concatenate([kp_ref[:, vsl], kc_ref[:, vsl]], axis=0)
        vpair_t = jnp.transpose(vpair.astype(F32)).astype(BF16)
        for half in range(2):
            kv = 2 * c + half
            hs = slice(half * HEAD_DIM, (half + 1) * HEAD_DIM)
            st = jnp.where(mask, _nt_dot(kpair[:, hs], _stack_q_heads(q_ref, kv)), NEG)
            m = jnp.max(st, axis=0, keepdims=True)
            if sink_ref is not None:
                sink = sink_ref[kv:kv + 1, :]
                m = jnp.maximum(m, sink)
            pt = jnp.exp(st - m)
            l = jnp.sum(pt, axis=0, keepdims=True)
            if sink_ref is not None:
                l = l + jnp.exp(sink - m)
            acc = jnp.dot(vpair_t[hs, :], pt.astype(BF16), preferred_element_type=F32)
            rows = slice(kv * HEAD_DIM, (kv + 1) * HEAD_DIM)
            ot_scr[rows, :] = acc / l
            if lt_scr is not None:
                lt_scr[rows, :] = jnp.broadcast_to(m + jnp.log(l), (HEAD_DIM, cols))
    _store_transposed_heads(o_ref, ot_scr)
    if lt_scr is not None:
        _store_transposed_heads(lse_ref, lt_scr)


def _local_attn_prompt(q, kvb, m, g, dil, n_batch, seq, sink_rows, want_lse):
    n_g, mp, _ = q.shape
    assert m % dil == 0 and seq % (dil * Q_TILE) == 0 and (m - mp) // dil <= Q_TILE
    nb = seq // dil // Q_TILE
    qv = q.reshape(n_g, mp // dil, dil * Q_COLS)
    kvv = kvb.reshape(n_g, mp // dil, dil * 2 * KV_COLS)
    has_sink = sink_rows is not None
    steps = n_batch * dil * nb

    def decode(t):
        tt = jnp.minimum(t, steps - 1)
        return tt // (dil * nb), (tt // nb) % dil, tt % nb

    def cur_map(t):
        n, r, i = decode(t)
        return (g, n * nb + i, r)

    def prev_map(t):
        n, r, i = decode(t)
        return (g, n * nb + jnp.maximum(i - 1, 0), r)

    def out_map(t):
        n, r, i = decode(t)
        tail = t >= steps
        return (jnp.where(tail, n_batch * nb, n * nb + i), jnp.where(tail, t - steps, r))

    in_specs = [
        pl.BlockSpec((None, Q_TILE, Q_COLS), cur_map),
        pl.BlockSpec((None, Q_TILE, 2 * KV_COLS), cur_map),
        pl.BlockSpec((None, Q_TILE, 2 * KV_COLS), prev_map),
    ]
    args = [qv, kvv, kvv]
    if has_sink:
        in_specs.append(pl.BlockSpec((N_KV_HEADS, GQA * Q_TILE), lambda t: (0, 0)))
        args.append(sink_rows)
    out_spec = pl.BlockSpec((Q_TILE, Q_COLS), out_map)
    out_specs = [out_spec]
    out_shape = [jax.ShapeDtypeStruct((m // dil, dil * Q_COLS), BF16)]
    scratch = [pltpu.VMEM((KV_COLS, GQA * Q_TILE), F32)]
    if want_lse:
        out_specs.append(out_spec)
        out_shape.append(jax.ShapeDtypeStruct((m // dil, dil * Q_COLS), F32))
        scratch.append(pltpu.VMEM((KV_COLS, GQA * Q_TILE), F32))
    outs = pl.pallas_call(
        functools.partial(_local_attn_kernel, has_sink=has_sink, want_lse=want_lse, nb=nb, steps=steps),
        grid=(steps + dil,),
        in_specs=in_specs,
        out_specs=out_specs,
        out_shape=out_shape,
        scratch_shapes=scratch,
        compiler_params=_params(1),
        name=f"local_attn_prompt_d{dil}",
    )(*args)
    return [o.reshape(m, Q_COLS) for o in outs]


def _head_block_mask(rows):
    head = lax.broadcasted_iota(I32, (rows, KV_COLS), 0) & (N_HEADS - 1)
    lane = lax.broadcasted_iota(I32, (rows, KV_COLS), 1)
    return (lane // HEAD_DIM) == (head // GQA)


def _fold_lanes(x):
    return x[:, 0:LANES] + x[:, LANES:2 * LANES] + x[:, 2 * LANES:3 * LANES] + x[:, 3 * LANES:4 * LANES]


def _padded_new_rows(n_ref):
    pad = jnp.zeros((LANES - n_ref.shape[1], KV_COLS), F32)
    knew = jnp.concatenate([n_ref[0, :, 0:KV_COLS], pad], axis=0).astype(BF16)
    vnew = jnp.concatenate([n_ref[0, :, KV_COLS:2 * KV_COLS], pad], axis=0).astype(BF16)
    return knew, vnew


def _sample_attn_kernel(*refs, dil, n_tok, has_sink, want_lse):
    q_ref, c_ref, n_ref = refs[:3]
    pos = 3
    sink_ref = None
    if has_sink:
        sink_ref = refs[pos]
        pos += 1
    o_ref = refs[pos]
    lse_ref = refs[pos + 1] if want_lse else None
    rows = n_tok * N_HEADS
    win = c_ref.shape[-1]
    q = q_ref[0]
    knew, vnew = _padded_new_rows(n_ref)
    tok_c = lax.broadcasted_iota(I32, (rows, win), 0) // N_HEADS
    pos_c = lax.broadcasted_iota(I32, (rows, win), 1)
    valid_c = (((pos_c - tok_c) & (dil - 1)) == 0) & (pos_c >= tok_c)
    tok_n = lax.broadcasted_iota(I32, (rows, LANES), 0) // N_HEADS
    pos_n = lax.broadcasted_iota(I32, (rows, LANES), 1)
    valid_n = (((tok_n - pos_n) & (dil - 1)) == 0) & (pos_n <= tok_n)
    s_c = jnp.where(valid_c, jnp.dot(q, c_ref[0, 0].astype(BF16), preferred_element_type=F32), NEG)
    s_n = jnp.where(valid_n, _nt_dot(q, knew), NEG)
    m = jnp.maximum(jnp.max(s_c, axis=1, keepdims=True), jnp.max(s_n, axis=1, keepdims=True))
    if has_sink:
        sink = sink_ref[...]
        m = jnp.maximum(m, sink)
    p_c = jnp.exp(s_c - m)
    p_n = jnp.exp(s_n - m)
    l = jnp.sum(p_c, axis=1, keepdims=True) + jnp.sum(p_n, axis=1, keepdims=True)
    if has_sink:
        l = l + jnp.exp(sink - m)
    of = (_nt_dot(p_c.astype(BF16), c_ref[0, 1].astype(BF16))
          + jnp.dot(p_n.astype(BF16), vnew, preferred_element_type=F32))
    o_ref[0] = _fold_lanes(jnp.where(_head_block_mask(rows), of, 0.0)) / l
    if want_lse:
        lse_ref[0] = jnp.broadcast_to(m + jnp.log(l), (rows, LANES))


def _sample_attn(qbd, cache_t, kv_new, dil, sink_col, want_lse):
    n_b, rows, _ = qbd.shape
    n_tok = rows // N_HEADS
    win = cache_t.shape[-1]
    assert win == LOCAL_WINDOW * dil and (dil & (dil - 1)) == 0 and n_tok <= 8
    has_sink = sink_col is not None
    in_specs = [
        pl.BlockSpec((1, rows, KV_COLS), lambda b: (b, 0, 0)),
        pl.BlockSpec((1, 2, KV_COLS, win), lambda b: (b, 0, 0, 0)),
        pl.BlockSpec((1, 8, 2 * KV_COLS), lambda b: (b, 0, 0)),
    ]
    args = [qbd, cache_t, kv_new]
    if has_sink:
        in_specs.append(pl.BlockSpec((rows, 1), lambda b: (0, 0)))
        args.append(sink_col)
    out_spec = pl.BlockSpec((1, rows, LANES), lambda b: (b, 0, 0))
    out_specs = [out_spec]
    out_shape = [jax.ShapeDtypeStruct((n_b, rows, LANES), F32)]
    if want_lse:
        out_specs.append(out_spec)
        out_shape.append(jax.ShapeDtypeStruct((n_b, rows, LANES), F32))
    return pl.pallas_call(
        functools.partial(_sample_attn_kernel, dil=dil, n_tok=n_tok, has_sink=has_sink, want_lse=want_lse),
        grid=(n_b,),
        in_specs=in_specs,
        out_specs=out_specs,
        out_shape=out_shape,
        compiler_params=_params(1),
        name=f"local_attn_sample_d{dil}",
    )(*args)


def _block_mean_kernel(k_ref, o_ref):
    n_blk = k_ref.shape[1] // MOBA_BLOCK
    cols = [jnp.mean(k_ref[:, b * MOBA_BLOCK:(b + 1) * MOBA_BLOCK], axis=1, keepdims=True) for b in range(n_blk)]
    lane = lax.broadcasted_iota(I32, (k_ref.shape[0], LANES), 1)
    out = jnp.zeros((k_ref.shape[0], LANES), F32)
    for b in range(n_blk):
        out = jnp.where(lane == b, cols[b], out)
    o_ref[...] = out


def _block_means(kvt):
    n_batch, _, seq = kvt.shape
    assert seq // MOBA_BLOCK <= LANES
    return pl.pallas_call(
        _block_mean_kernel,
        grid=(n_batch,),
        in_specs=[pl.BlockSpec((None, KV_COLS, seq), lambda n: (n, 0, 0))],
        out_specs=pl.BlockSpec((None, KV_COLS, LANES), lambda n: (n, 0, 0)),
        out_shape=jax.ShapeDtypeStruct((n_batch, KV_COLS, LANES), F32),
        compiler_params=_params(1),
        name="moba_block_means",
    )(kvt)


def _moba_prompt_kernel(q_ref, kv_ref, km_ref, o_ref, *scratch, nq, steps, n_blk):
    t = pl.program_id(0)

    @pl.when(t >= steps)
    def _():
        o_ref[...] = jnp.zeros(o_ref.shape, o_ref.dtype)

    @pl.when(t < steps)
    def _():
        _moba_prompt_step(q_ref, kv_ref, km_ref, o_ref, *scratch, t % nq, n_blk)


def _moba_prompt_step(q_ref, kv_ref, km_ref, o_ref, vt_scr, q4_scr, bias_scr, m_scr, l_scr, acc_scr, i, n_blk):
    own = (i * Q_TILE) // MOBA_BLOCK
    cols = GQA * Q_TILE

    @pl.when(i == 0)
    def _():
        for b in range(n_blk):
            for c in range(KV_COLS // LANES):
                v = kv_ref[b * MOBA_BLOCK:(b + 1) * MOBA_BLOCK, KV_COLS + c * LANES:KV_COLS + (c + 1) * LANES]
                vt_scr[c * LANES:(c + 1) * LANES, b * MOBA_BLOCK:(b + 1) * MOBA_BLOCK] = (
                    jnp.transpose(v.astype(F32)).astype(BF16))

    blk_rows = bias_scr.shape[1]
    bid = lax.broadcasted_iota(I32, (blk_rows, cols), 0)
    q_off = (i * Q_TILE) % MOBA_BLOCK + (lax.broadcasted_iota(I32, (MOBA_BLOCK, cols), 1) & (Q_TILE - 1))
    own_bias = jnp.where(lax.broadcasted_iota(I32, (MOBA_BLOCK, cols), 0) <= q_off, 0.0, NEG)
    km_t = jnp.transpose(km_ref[...])[0:blk_rows, :].astype(BF16)
    for kv in range(N_KV_HEADS):
        q4 = _stack_q_heads(q_ref, kv)
        q4_scr[kv] = q4
        gate = _nt_dot(km_t[:, kv * HEAD_DIM:(kv + 1) * HEAD_DIM], q4)
        gate = jnp.where(bid < own, gate, NEG)
        rank = jnp.zeros((blk_rows, cols), F32)
        for b in range(n_blk):
            gb = gate[b:b + 1, :]
            rank = rank + jnp.where((gb > gate) | ((gb == gate) & (b < bid)), 1.0, 0.0)
        bias_scr[kv] = jnp.where((rank < min(MOBA_TOPK, n_blk)) & (bid < own), 0.0, NEG)
    m_scr[...] = jnp.full(m_scr.shape, NEG, F32)
    l_scr[...] = jnp.zeros(l_scr.shape, F32)
    acc_scr[...] = jnp.zeros(acc_scr.shape, F32)

    def update(kv, start, bias):
        c = kv // 2
        hs = slice((kv % 2) * HEAD_DIM, (kv % 2 + 1) * HEAD_DIM)
        rows = slice(kv * HEAD_DIM, (kv + 1) * HEAD_DIM)
        k = kv_ref[pl.ds(start, MOBA_BLOCK), c * LANES:(c + 1) * LANES][:, hs]
        st = _nt_dot(k, q4_scr[kv]) + bias
        m_old = m_scr[kv, 0:1, :]
        m_new = jnp.maximum(m_old, jnp.max(st, axis=0, keepdims=True))
        pt = jnp.exp(st - m_new)
        alpha = jnp.exp(m_old - m_new)
        pv = jnp.dot(vt_scr[rows, pl.ds(start, MOBA_BLOCK)], pt.astype(BF16), preferred_element_type=F32)
        acc_scr[rows, :] = alpha * acc_scr[rows, :] + pv
        l_scr[kv, 0:1, :] = alpha * l_scr[kv, 0:1, :] + jnp.sum(pt, axis=0, keepdims=True)
        m_scr[kv, 0:1, :] = m_new

    def body(j, carry):
        start = pl.multiple_of(j * MOBA_BLOCK, MOBA_BLOCK)
        for kv in range(N_KV_HEADS):
            update(kv, start, bias_scr[kv, pl.ds(j, 1), :])
        return carry

    lax.fori_loop(0, own, body, 0)
    for kv in range(N_KV_HEADS):
        update(kv, pl.multiple_of(own * MOBA_BLOCK, MOBA_BLOCK), own_bias)
    for kv in range(N_KV_HEADS):
        rows = slice(kv * HEAD_DIM, (kv + 1) * HEAD_DIM)
        acc_scr[rows, :] = acc_scr[rows, :] / l_scr[kv, 0:1, :]
    _store_transposed_heads(o_ref, acc_scr)


def _moba_prompt(q, kvb, kmean, m, n_batch, seq):
    nq = seq // Q_TILE
    n_blk = seq // MOBA_BLOCK
    steps = n_batch * nq
    assert m - steps * Q_TILE <= Q_TILE
    batch_of = lambda t: jnp.minimum(t, steps - 1) // nq
    return pl.pallas_call(
        functools.partial(_moba_prompt_kernel, nq=nq, steps=steps, n_blk=n_blk),
        grid=(steps + 1,),
        in_specs=[
            pl.BlockSpec((Q_TILE, Q_COLS), lambda t: (jnp.minimum(t, steps - 1), 0)),
            pl.BlockSpec((seq, 2 * KV_COLS), lambda t: (batch_of(t), 0)),
            pl.BlockSpec((None, KV_COLS, LANES), lambda t: (batch_of(t), 0, 0)),
        ],
        out_specs=pl.BlockSpec((Q_TILE, Q_COLS), lambda t: (t, 0)),
        out_shape=jax.ShapeDtypeStruct((m, Q_COLS), BF16),
        scratch_shapes=[
            pltpu.VMEM((KV_COLS, seq), BF16),
            pltpu.VMEM((N_KV_HEADS, GQA * Q_TILE, HEAD_DIM), BF16),
            pltpu.VMEM((N_KV_HEADS, -(-n_blk // 8) * 8, GQA * Q_TILE), F32),
            pltpu.VMEM((N_KV_HEADS, 8, GQA * Q_TILE), F32),
            pltpu.VMEM((N_KV_HEADS, 8, GQA * Q_TILE), F32),
            pltpu.VMEM((KV_COLS, GQA * Q_TILE), F32),
        ],
        compiler_params=_params(1),
        name="moba_prompt",
    )(q, kvb, kmean)


def _top_blocks(gate, lane, k):
    chosen = jnp.zeros(gate.shape, F32)
    for _ in range(k):
        mx = jnp.max(gate, axis=1, keepdims=True)
        idx = jnp.min(jnp.where(gate == mx, lane, LANES), axis=1, keepdims=True)
        hit = lane == idx
        chosen = jnp.where(hit & (mx > 0.5 * NEG), 1.0, chosen)
        gate = jnp.where(hit, NEG, gate)
    return chosen


def _moba_sample_kernel(pt_ref, q_ref, n_ref, *refs, n_chunks, n_tok):
    page_refs = refs[:PAGES_PER_STEP]
    o_ref, s_scr, acc_scr, l_scr, m_scr, ch_scr = refs[PAGES_PER_STEP:]
    c = pl.program_id(1)
    rows = q_ref.shape[1]
    step_keys = PAGES_PER_STEP * PAGE_SIZE
    blk_per_step = step_keys // MOBA_BLOCK
    n_blk = n_chunks * blk_per_step
    lane = lax.broadcasted_iota(I32, (rows, LANES), 1)

    @pl.when(c < n_chunks)
    def _scores():
        kt = jnp.concatenate([r[...] for r in page_refs], axis=1).astype(BF16)
        s_scr[:, pl.ds(pl.multiple_of(c * step_keys, step_keys), step_keys)] = (
            jnp.dot(q_ref[0], kt, preferred_element_type=F32))

    @pl.when(c == n_chunks)
    def _select():
        gate = jnp.full((rows, LANES), NEG, F32)
        bmax = jnp.full((rows, LANES), NEG, F32)
        for b in range(n_blk):
            sb = s_scr[:, b * MOBA_BLOCK:(b + 1) * MOBA_BLOCK]
            gate = jnp.where(lane == b, jnp.sum(sb, axis=1, keepdims=True), gate)
            bmax = jnp.where(lane == b, jnp.max(sb, axis=1, keepdims=True), bmax)
        chosen = _top_blocks(gate, lane, min(MOBA_TOPK, n_blk))
        knew, vnew = _padded_new_rows(n_ref)
        tok = lax.broadcasted_iota(I32, (rows, LANES), 0) // N_HEADS
        valid = (lane <= tok) & (lane < n_tok)
        s_new = jnp.where(valid, _nt_dot(q_ref[0], knew), NEG)
        m = jnp.maximum(jnp.max(jnp.where(chosen > 0.5, bmax, NEG), axis=1, keepdims=True),
                        jnp.max(s_new, axis=1, keepdims=True))
        p_new = jnp.where(valid, jnp.exp(s_new - m), 0.0)
        acc_scr[...] = jnp.dot(p_new.astype(BF16), vnew, preferred_element_type=F32)
        l_scr[...] = jnp.sum(p_new, axis=1, keepdims=True)
        m_scr[...] = m
        ch_scr[...] = chosen

    @pl.when(c >= n_chunks)
    def _accumulate():
        cc = c - n_chunks
        vt = jnp.concatenate([r[...] for r in page_refs], axis=1).astype(BF16)
        s = s_scr[:, pl.ds(pl.multiple_of(cc * step_keys, step_keys), step_keys)]
        chosen = ch_scr[...]
        keep = jnp.concatenate(
            [jnp.broadcast_to(jnp.max(jnp.where(lane == cc * blk_per_step + u, chosen, 0.0), axis=1, keepdims=True) > 0.5,
                              (rows, MOBA_BLOCK)) for u in range(blk_per_step)], axis=1)
        p = jnp.where(keep, jnp.exp(s - m_scr[...]), 0.0)
        l_scr[...] += jnp.sum(p, axis=1, keepdims=True)
        acc_scr[...] += _nt_dot(p.astype(BF16), vt)

    @pl.when(c == 2 * n_chunks - 1)
    def _finish():
        o_ref[0] = _fold_lanes(jnp.where(_head_block_mask(rows), acc_scr[...], 0.0)) / l_scr[...]


def _moba_sample(qbd, kv_new, cache_t, page_table):
    n_b, rows, _ = qbd.shape
    n_tok = rows // N_HEADS
    n_pages = page_table.shape[1]
    assert n_pages % PAGES_PER_STEP == 0 and (n_pages * PAGE_SIZE) % MOBA_BLOCK == 0
    assert n_pages * PAGE_SIZE // MOBA_BLOCK <= LANES
    n_chunks = n_pages // PAGES_PER_STEP

    def page_map(u):
        return lambda b, c, pt: (pt[b, (c % n_chunks) * PAGES_PER_STEP + u], c // n_chunks, 0, 0)

    page_specs = [pl.BlockSpec((None, None, KV_COLS, PAGE_SIZE), page_map(u)) for u in range(PAGES_PER_STEP)]
    grid_spec = pltpu.PrefetchScalarGridSpec(
        num_scalar_prefetch=1,
        grid=(n_b, 2 * n_chunks),
        in_specs=[
            pl.BlockSpec((1, rows, KV_COLS), lambda b, c, pt: (b, 0, 0)),
            pl.BlockSpec((1, 8, 2 * KV_COLS), lambda b, c, pt: (b, 0, 0)),
        ] + page_specs,
        out_specs=pl.BlockSpec((1, rows, LANES), lambda b, c, pt: (b, 0, 0)),
        scratch_shapes=[
            pltpu.VMEM((rows, n_pages * PAGE_SIZE), F32),
            pltpu.VMEM((rows, KV_COLS), F32),
            pltpu.VMEM((rows, 1), F32),
            pltpu.VMEM((rows, 1), F32),
            pltpu.VMEM((rows, LANES), F32),
        ],
    )
    return pl.pallas_call(
        functools.partial(_moba_sample_kernel, n_chunks=n_chunks, n_tok=n_tok),
        grid_spec=grid_spec,
        out_shape=jax.ShapeDtypeStruct((n_b, rows, LANES), F32),
        compiler_params=_params(2),
        name="moba_sample",
    )(page_table, qbd, kv_new, *([cache_t] * PAGES_PER_STEP))


def _layer_norm(h, gain, bias):
    mu = jnp.mean(h, axis=-1, keepdims=True)
    d = h - mu
    var = jnp.mean(d * d, axis=-1, keepdims=True)
    return d * lax.rsqrt(var + LN_EPS) * gain + bias


def _oproj_ln_kernel(*refs, n_merge):
    x_ref = refs[0]
    o_refs = refs[1:1 + n_merge]
    w_ref, g_ref, b_ref, out_ref, rows_ref = refs[-5:]
    if n_merge == 1:
        o = o_refs[0][...]
    else:
        lses = [r[...] for r in refs[1 + n_merge:1 + 2 * n_merge]]
        mx = functools.reduce(jnp.maximum, lses)
        es = [jnp.exp(l - mx) for l in lses]
        num = sum(e * r[...].astype(F32) for r, e in zip(o_refs, es))
        o = (num / sum(es)).astype(BF16)
    y = jnp.dot(o, w_ref[...], preferred_element_type=F32)
    h = _layer_norm(DEEPNORM_ALPHA * x_ref[...] + y, g_ref[...], b_ref[...])
    out_ref[...] = h
    for j in range(D_MODEL // LANES):
        rows_ref[:, j, :] = h[:, j * LANES:(j + 1) * LANES]


def _oproj_ln(x, o, lse, w_bf16, gain, bias):
    m = x.shape[0]
    n_merge = len(o)
    tm = _row_tile(m, 320 if n_merge == 1 else 160, 16)
    row = pl.BlockSpec((tm, D_MODEL), lambda i: (i, 0))
    vec = pl.BlockSpec((1, D_MODEL), lambda i: (0, 0))
    wspec = pl.BlockSpec((Q_COLS, D_MODEL), lambda i: (0, 0))
    groups = list(o) + (list(lse) if n_merge > 1 else [])
    in_specs = [row] + [pl.BlockSpec((tm, Q_COLS), lambda i: (i, 0))] * len(groups) + [wspec, vec, vec]
    args = [x] + groups + [w_bf16, gain, bias]
    return pl.pallas_call(
        functools.partial(_oproj_ln_kernel, n_merge=n_merge),
        grid=(m // tm,),
        in_specs=in_specs,
        out_specs=[row, pl.BlockSpec((tm, D_MODEL // LANES, LANES), lambda i: (i, 0, 0))],
        out_shape=[jax.ShapeDtypeStruct((m, D_MODEL), F32),
                   jax.ShapeDtypeStruct((m, D_MODEL // LANES, LANES), F32)],
        compiler_params=_params(1),
        name=f"oproj_ln_m{n_merge}",
    )(*args)


def _router_kernel(x_ref, rwt_ref, bias_ref, idx_ref, gate_ref, cnt_ref, carry_scr):
    i = pl.program_id(0)
    tm = x_ref.shape[0]

    @pl.when(i == 0)
    def _():
        carry_scr[...] = jnp.zeros(carry_scr.shape, F32)

    logits = _nt_dot(rwt_ref[...], x_ref[...].astype(BF16))
    scores = 1.0 / (1.0 + jnp.exp(-logits))
    biased = scores + bias_ref[...]
    srow = [scores[e:e + 1, :] for e in range(N_EXPERTS)]
    brow = [biased[e:e + 1, :] for e in range(N_EXPERTS)]

    gscore = []
    for g in range(N_EXPERT_GROUPS):
        a0, a1, a2, a3 = brow[4 * g:4 * g + 4]
        hi01, lo01 = jnp.maximum(a0, a1), jnp.minimum(a0, a1)
        hi23, lo23 = jnp.maximum(a2, a3), jnp.minimum(a2, a3)
        gscore.append(jnp.maximum(hi01, hi23) + jnp.maximum(jnp.minimum(hi01, hi23), jnp.maximum(lo01, lo23)))
    gsel = jnp.zeros((1, tm), I32)
    best = gscore[0]
    for g in range(1, N_EXPERT_GROUPS):
        better = gscore[g] > best
        gsel = jnp.where(better, g, gsel)
        best = jnp.where(better, gscore[g], best)

    def in_group(rows_, j):
        v = rows_[j]
        for g in range(1, N_EXPERT_GROUPS):
            v = jnp.where(gsel == g, rows_[4 * g + j], v)
        return v

    ab = [in_group(brow, j) for j in range(EXPERTS_PER_GROUP)]
    au = [in_group(srow, j) for j in range(EXPERTS_PER_GROUP)]
    i1 = jnp.zeros((1, tm), I32)
    v1 = ab[0]
    for j in range(1, EXPERTS_PER_GROUP):
        better = ab[j] > v1
        i1 = jnp.where(better, j, i1)
        v1 = jnp.where(better, ab[j], v1)
    i2 = jnp.full((1, tm), -1, I32)
    v2 = jnp.full((1, tm), -jnp.inf, F32)
    for j in range(EXPERTS_PER_GROUP):
        better = (i1 != j) & ((ab[j] > v2) | (i2 < 0))
        i2 = jnp.where(better, j, i2)
        v2 = jnp.where(better, ab[j], v2)

    def pick(vals, idx):
        v = vals[0]
        for j in range(1, EXPERTS_PER_GROUP):
            v = jnp.where(idx == j, vals[j], v)
        return v

    s1, s2 = pick(au, i1), pick(au, i2)
    den = s1 + s2
    e1 = gsel * EXPERTS_PER_GROUP + i1
    e2 = gsel * EXPERTS_PER_GROUP + i2

    eid = lax.broadcasted_iota(I32, (N_EXPERTS, tm), 0)
    hit1, hit2 = eid == e1, eid == e2
    onehot = jnp.where(hit1 | hit2, 1.0, 0.0)
    before = lax.broadcasted_iota(I32, (tm, tm), 0) < lax.broadcasted_iota(I32, (tm, tm), 1)
    prefix = jnp.dot(onehot.astype(BF16), jnp.where(before, 1.0, 0.0).astype(BF16), preferred_element_type=F32)
    offset = carry_scr[...][:, 0:1] + prefix
    r1 = jnp.sum(jnp.where(hit1, offset, 0.0), axis=0, keepdims=True).astype(I32)
    r2 = jnp.sum(jnp.where(hit2, offset, 0.0), axis=0, keepdims=True).astype(I32)
    carry_scr[...] = carry_scr[...] + jnp.sum(onehot, axis=1, keepdims=True)

    row8 = lax.broadcasted_iota(I32, (8, tm), 0)
    idx_ref[...] = jnp.where(row8 == 0, e1, jnp.where(row8 == 1, e2, jnp.where(row8 == 2, r1, jnp.where(row8 == 3, r2, 0))))
    gate_ref[...] = jnp.where(row8 == 0, s1 / den, jnp.where(row8 == 1, s2 / den, 0.0))
    cnt_ref[...] = carry_scr[...]


def _router(x, rwt_bf16, bias_col):
    m = x.shape[0]
    tm = _row_tile(m, 640, LANES)
    return pl.pallas_call(
        _router_kernel,
        grid=(m // tm,),
        in_specs=[
            pl.BlockSpec((tm, D_MODEL), lambda i: (i, 0)),
            pl.BlockSpec((N_EXPERTS, D_MODEL), lambda i: (0, 0)),
            pl.BlockSpec((N_EXPERTS, 1), lambda i: (0, 0)),
        ],
        out_specs=[
            pl.BlockSpec((8, tm), lambda i: (0, i)),
            pl.BlockSpec((8, tm), lambda i: (0, i)),
            pl.BlockSpec((N_EXPERTS, LANES), lambda i: (0, 0)),
        ],
        out_shape=[
            jax.ShapeDtypeStruct((8, m), I32),
            jax.ShapeDtypeStruct((8, m), F32),
            jax.ShapeDtypeStruct((N_EXPERTS, LANES), F32),
        ],
        scratch_shapes=[pltpu.VMEM((N_EXPERTS, LANES), F32)],
        compiler_params=_params(1),
        name="router",
    )(x, rwt_bf16, bias_col)


def _start_row_gather(src_hbm, row_of, n_rows, dst, sem):
    def body(r, carry):
        pltpu.make_async_copy(src_hbm.at[pl.ds(row_of(r), 1)], dst.at[pl.ds(r, 1)], sem).start()
        return carry
    lax.fori_loop(0, n_rows, body, 0, unroll=8)


def _wait_row_gather(src_hbm, n_rows, dst, sem):
    pltpu.make_async_copy(src_hbm.at[pl.ds(0, n_rows)], dst, sem).wait()


def _start_row_gather_inline(src_hbm, idx_ref, base, n_rows, dst, sem):
    for r in range(n_rows):
        pltpu.make_async_copy(src_hbm.at[pl.ds(idx_ref[base + r], 1)], dst.at[pl.ds(r, 1)], sem).start()


def _moe_ffn_kernel(ce_ref, tok_ref, x_hbm, wg_ref, wu_ref, wd_ref, y_ref, xbuf0, xbuf1, sem, wgb, wub, wdb, *,
                    n_chunks):
    c = pl.program_id(0)
    bufs = (xbuf0, xbuf1)

    @pl.when(c == 0)
    def _():
        _start_row_gather(x_hbm, lambda r: tok_ref[r], MOE_CHUNK, xbuf0, sem.at[0])

    @pl.when((c == 0) | (ce_ref[c] != ce_ref[jnp.maximum(c - 1, 0)]))
    def _():
        wgb[...] = wg_ref[...].astype(BF16)
        wub[...] = wu_ref[...].astype(BF16)
        wdb[...] = wd_ref[...].astype(BF16)

    def run(cur):
        _wait_row_gather(x_hbm, MOE_CHUNK, bufs[cur], sem.at[cur])
        _start_row_gather_inline(x_hbm, tok_ref, (c + 1) * MOE_CHUNK, MOE_CHUNK, bufs[1 - cur], sem.at[1 - cur])
        xb = jnp.concatenate([bufs[cur][:, j, :] for j in range(D_MODEL // LANES)], axis=1).astype(BF16)
        a = jnp.dot(xb, wgb[...], preferred_element_type=F32)
        u = jnp.dot(xb, wub[...], preferred_element_type=F32)
        h = (a / (1.0 + jnp.exp(-a))) * u
        y_ref[...] = jnp.dot(h.astype(BF16), wdb[...], preferred_element_type=F32)

    @pl.when(c % 2 == 0)
    def _():
        run(0)

    @pl.when(c % 2 == 1)
    def _():
        run(1)

    @pl.when(c == n_chunks - 1)
    def _():
        _wait_row_gather(x_hbm, MOE_CHUNK, bufs[n_chunks % 2], sem.at[n_chunks % 2])


def _moe_ffn(x, chunk_expert, tok_of_row, w_gate, w_up, w_down, layer):
    n_chunks = chunk_expert.shape[0]
    assert tok_of_row.shape[0] == (n_chunks + 1) * MOE_CHUNK
    w_in = pl.BlockSpec((None, None, D_MODEL, D_EXPERT), lambda c, ce, tok: (layer, ce[c], 0, 0))
    grid_spec = pltpu.PrefetchScalarGridSpec(
        num_scalar_prefetch=2,
        grid=(n_chunks,),
        in_specs=[
            pl.BlockSpec(memory_space=pl.ANY),
            w_in,
            w_in,
            pl.BlockSpec((None, None, D_EXPERT, D_MODEL), lambda c, ce, tok: (layer, ce[c], 0, 0)),
        ],
        out_specs=pl.BlockSpec((MOE_CHUNK, D_MODEL), lambda c, ce, tok: (c, 0)),
        scratch_shapes=[
            pltpu.VMEM((MOE_CHUNK, D_MODEL // LANES, LANES), F32),
            pltpu.VMEM((MOE_CHUNK, D_MODEL // LANES, LANES), F32),
            pltpu.SemaphoreType.DMA((2,)),
            pltpu.VMEM((D_MODEL, D_EXPERT), BF16),
            pltpu.VMEM((D_MODEL, D_EXPERT), BF16),
            pltpu.VMEM((D_EXPERT, D_MODEL), BF16),
        ],
    )
    return pl.pallas_call(
        functools.partial(_moe_ffn_kernel, n_chunks=n_chunks),
        grid_spec=grid_spec,
        out_shape=jax.ShapeDtypeStruct((n_chunks * MOE_CHUNK, D_MODEL), F32),
        compiler_params=_params(1),
        name="moe_ffn",
    )(chunk_expert, tok_of_row, x, w_gate, w_up, w_down)


def _combine_ln_kernel(dest_ref, y_hbm, x_ref, g1_ref, g2_ref, gain_ref, bias_ref, out_ref, ybuf0, ybuf1, sem, *,
                       n_tiles):
    i = pl.program_id(0)
    tm = x_ref.shape[0]
    bufs = (ybuf0, ybuf1)

    @pl.when(i == 0)
    def _():
        _start_row_gather(y_hbm, lambda r: dest_ref[r], 2 * tm, ybuf0, sem.at[0])

    def run(cur):
        _wait_row_gather(y_hbm, 2 * tm, bufs[cur], sem.at[cur])
        _start_row_gather_inline(y_hbm, dest_ref, (i + 1) * 2 * tm, 2 * tm, bufs[1 - cur], sem.at[1 - cur])
        f = g1_ref[...] * bufs[cur][0:tm, :] + g2_ref[...] * bufs[cur][tm:2 * tm, :]
        out_ref[...] = _layer_norm(DEEPNORM_ALPHA * x_ref[...] + f, gain_ref[...], bias_ref[...])

    @pl.when(i % 2 == 0)
    def _():
        run(0)

    @pl.when(i % 2 == 1)
    def _():
        run(1)

    @pl.when(i == n_tiles - 1)
    def _():
        _wait_row_gather(y_hbm, 2 * tm, bufs[n_tiles % 2], sem.at[n_tiles % 2])


def _combine_ln(x, ybuf, dest, g1, g2, gain, bias):
    m = x.shape[0]
    tm = _row_tile(m, 128, 8)
    n_tiles = m // tm
    dest_tiles = dest.reshape(2, n_tiles, tm).transpose(1, 0, 2).reshape(-1)
    dest_tiles = jnp.concatenate([dest_tiles, jnp.zeros((2 * tm,), I32)])
    row = pl.BlockSpec((tm, D_MODEL), lambda i, d: (i, 0))
    col = pl.BlockSpec((tm, 1), lambda i, d: (i, 0))
    vec = pl.BlockSpec((1, D_MODEL), lambda i, d: (0, 0))
    grid_spec = pltpu.PrefetchScalarGridSpec(
        num_scalar_prefetch=1,
        grid=(n_tiles,),
        in_specs=[pl.BlockSpec(memory_space=pl.ANY), row, col, col, vec, vec],
        out_specs=row,
        scratch_shapes=[pltpu.VMEM((2 * tm, D_MODEL), F32), pltpu.VMEM((2 * tm, D_MODEL), F32),
                        pltpu.SemaphoreType.DMA((2,))],
    )
    return pl.pallas_call(
        functools.partial(_combine_ln_kernel, n_tiles=n_tiles),
        grid_spec=grid_spec,
        out_shape=jax.ShapeDtypeStruct((m, D_MODEL), F32),
        compiler_params=_params(1),
        name="moe_combine_ln",
    )(dest_tiles, ybuf, x, g1, g2, gain, bias)


def _moe_layer(x, x_rows, rwt_bf16, rbias_col, w_gate, w_up, w_down, layer, gain, bias):
    m = x.shape[0]
    idx, gates, counts = _router(x, rwt_bf16, rbias_col)
    experts, ranks = idx[0:2], idx[2:4]
    cnt = counts[:, 0].astype(I32)
    padded = (cnt + MOE_CHUNK - 1) // MOE_CHUNK * MOE_CHUNK
    pend = jnp.cumsum(padded)
    pstart = pend - padded
    eid = jnp.arange(N_EXPERTS, dtype=I32)[:, None, None]
    dest = ranks + jnp.sum(jnp.where(experts[None] == eid, pstart[:, None, None], 0), axis=0)
    n_chunks = -(-2 * m // MOE_CHUNK) + N_EXPERTS
    chunk_start = jnp.arange(n_chunks, dtype=I32) * MOE_CHUNK
    chunk_expert = jnp.minimum(jnp.sum((pend[None, :] <= chunk_start[:, None]).astype(I32), axis=1), N_EXPERTS - 1)
    tok = jnp.tile(jnp.arange(m, dtype=I32), 2)
    tok_of_row = jnp.zeros(((n_chunks + 1) * MOE_CHUNK,), I32).at[dest.reshape(-1)].set(tok)
    ybuf = _moe_ffn(x_rows, chunk_expert, tok_of_row, w_gate, w_up, w_down, layer)
    return _combine_ln(x, ybuf, dest, gates[0].reshape(m, 1), gates[1].reshape(m, 1), gain, bias)


def _block_diag_queries(q_rows, n_b, n_tok):
    q4 = q_rows.reshape(n_b, n_tok, N_HEADS, HEAD_DIM)
    tiled = jnp.tile(q4, (1, 1, 1, N_KV_HEADS))
    head = jnp.arange(N_HEADS)[:, None] // GQA
    lane_kv = jnp.arange(KV_COLS)[None, :] // HEAD_DIM
    return jnp.where(head == lane_kv, tiled, jnp.zeros((), tiled.dtype)).reshape(n_b, n_tok * N_HEADS, KV_COLS)


def _sample_rows_grouped(o, n_b, n_tok):
    o = o.reshape(n_b, n_tok, N_KV_HEADS, GQA, LANES)
    return o.transpose(0, 1, 3, 2, 4)


def _sample_out(o, n_b, n_tok):
    o = _sample_rows_grouped(o, n_b, n_tok)
    return (o[..., :HEAD_DIM] + o[..., HEAD_DIM:]).reshape(n_b * n_tok, Q_COLS)


def _sample_lse(lse, n_b, n_tok):
    return _sample_rows_grouped(lse, n_b, n_tok)[..., :HEAD_DIM].reshape(n_b * n_tok, Q_COLS)


def _new_kv_rows(kv_s, n_b, n_tok):
    return jnp.pad(kv_s.reshape(n_b, n_tok, 2 * KV_COLS), ((0, 0), (0, 8 - n_tok), (0, 0)))


def _position_minor(cache):
    lead = cache.shape[:-4]
    n = len(lead)
    t = cache.transpose(tuple(range(n)) + (n + 1, n + 2, n + 3, n))
    return t.reshape(lead + (2, KV_COLS, cache.shape[-4]))


def _position_major(kvt):
    lead = kvt.shape[:-2]
    n = len(lead)
    t = kvt.reshape(lead + (2, N_KV_HEADS, HEAD_DIM, kvt.shape[-1]))
    return t.transpose(tuple(range(n)) + (n + 3, n, n + 1, n + 2))


def _window_state(kvt, kv_s, cache, n_b, n_tok, window):
    seq = kvt.shape[-1]
    state_p = _position_major(kvt[..., seq - min(window, seq):])
    kvs = kv_s.reshape(n_b, n_tok, 2, N_KV_HEADS, HEAD_DIM)
    state_s = jnp.concatenate([cache, kvs], axis=1)[:, -cache.shape[1]:]
    return state_p, state_s


def kernel(x_prompt, x_sample, cache_swa, cache_dil0, cache_dil1, cache_dil2, cache_moba, page_table,
           w_qkv_swa, w_o_swa, sinks_swa, w_qkv_dil, w_o_dil, w_qkv_moba, w_o_moba,
           ln_gain, ln_bias, router_w, router_bias, w_gate_e, w_up_e, w_down_e):
    n_batch, seq, d = x_prompt.shape
    n_b, n_tok = x_sample.shape[:2]
    past_len = page_table.shape[1] * cache_moba.shape[2]
    mp, ms = n_batch * seq, n_b * n_tok
    m = mp + ms
    assert d == D_MODEL and cache_moba.shape[2] == PAGE_SIZE and n_tok <= 4 and ms == Q_TILE and mp % ms == 0
    x = jnp.concatenate([x_prompt.reshape(mp, d), x_sample.reshape(ms, d)], axis=0)
    pos = jnp.concatenate([jnp.tile(jnp.arange(seq), n_batch), jnp.tile(past_len + jnp.arange(n_tok), n_b)])
    cos, sin = _rope_tables(pos)
    rwt = router_w.T.astype(BF16)
    rbias = router_bias.astype(F32).reshape(N_EXPERTS, 1)
    dil_caches = (cache_dil0, cache_dil1, cache_dil2)

    def project(w_bf16, g, fold=1):
        q, kvb, kvt = _qkv_rope(x, w_bf16, cos, sin, g, 1, 0, mp, seq, fold)
        q_s, kv_s = _qkv_rope(x, w_bf16, cos, sin, g, 1, mp, ms, None)
        return q, kvb, kvt, q_s, kv_s

    def with_sample_rows(a, a_s, dil):
        if dil == 1:
            return lax.dynamic_update_slice(a, a_s, (mp, 0))
        c = a.shape[1]
        nat = a[:mp].reshape(n_batch, dil, seq // dil, c).transpose(0, 2, 1, 3).reshape(mp, c)
        return jnp.concatenate([nat, a_s], axis=0)

    def sample_local(q_s, kv_s, cache, dil, sink_col, want_lse):
        outs = _sample_attn(_block_diag_queries(q_s, n_b, n_tok), _position_minor(cache),
                            _new_kv_rows(kv_s, n_b, n_tok), dil, sink_col, want_lse)
        return _sample_out(outs[0], n_b, n_tok), (_sample_lse(outs[1], n_b, n_tok) if want_lse else None)

    swa_p, swa_s, moba_p, moba_s = [], [], [], []
    dil_p, dil_s = ([], [], []), ([], [], [])
    for i in range(DEPTH):
        kind, j = i % 3, i // 3
        gain1, bias1 = ln_gain[i, 0].reshape(1, d), ln_bias[i, 0].reshape(1, d)
        gain2, bias2 = ln_gain[i, 1].reshape(1, d), ln_bias[i, 1].reshape(1, d)
        if kind == 0:
            q, kvb, kvt, q_s, kv_s = project(w_qkv_swa[j].astype(BF16), 0)
            sinks = sinks_swa[j].astype(F32)
            sink_rows = jnp.repeat(sinks, Q_TILE).reshape(N_KV_HEADS, GQA * Q_TILE)
            (o,) = _local_attn_prompt(q, kvb, m, 0, 1, n_batch, seq, sink_rows, False)
            sink_col = jnp.tile(sinks, n_tok).reshape(n_tok * N_HEADS, 1)
            o_s, _ = sample_local(q_s[0], kv_s[0], cache_swa[j], 1, sink_col, False)
            o = lax.dynamic_update_slice(o, o_s.astype(BF16), (mp, 0))
            sp, ss = _window_state(kvt[0], kv_s[0], cache_swa[j], n_b, n_tok, SWA_WINDOW)
            swa_p.append(sp)
            swa_s.append(ss)
            x, x_rows = _oproj_ln(x, [o], None, _grouped_head_order(w_o_swa[j]).astype(BF16), gain1, bias1)
        elif kind == 1:
            w_dil = w_qkv_dil[j].astype(BF16)
            os_, lses = [], []
            for g, (win, dil) in enumerate(DIL_PAIRS):
                q, kvb, kvt, q_s, kv_s = project(w_dil, g, dil)
                o, lse = _local_attn_prompt(q, kvb, m, 0, 1, n_batch * dil, seq // dil, None, True)
                o_s, lse_s = sample_local(q_s[0], kv_s[0], dil_caches[g][j], dil, None, True)
                os_.append(with_sample_rows(o, o_s.astype(BF16), dil))
                lses.append(with_sample_rows(lse, lse_s, dil))
                sp, ss = _window_state(kvt[0], kv_s[0], dil_caches[g][j], n_b, n_tok, win)
                dil_p[g].append(sp)
                dil_s[g].append(ss)
            x, x_rows = _oproj_ln(x, os_, lses, _grouped_head_order(w_o_dil[j]).astype(BF16), gain1, bias1)
        else:
            q, kvb, kvt, q_s, kv_s = project(w_qkv_moba[j].astype(BF16), 0)
            o = _moba_prompt(q[0], kvb[0], _block_means(kvt[0]), m, n_batch, seq)
            o_s = _moba_sample(_block_diag_queries(q_s[0], n_b, n_tok), _new_kv_rows(kv_s[0], n_b, n_tok),
                               _position_minor(cache_moba[j]), page_table)
            o = lax.dynamic_update_slice(o, _sample_out(o_s, n_b, n_tok).astype(BF16), (mp, 0))
            moba_p.append(_position_major(kvt[0]))
            moba_s.append(kv_s[0].reshape(n_b, n_tok, 2, N_KV_HEADS, HEAD_DIM))
            x, x_rows = _oproj_ln(x, [o], None, _grouped_head_order(w_o_moba[j]).astype(BF16), gain1, bias1)
        x = _moe_layer(x, x_rows, rwt, rbias, w_gate_e, w_up_e, w_down_e, i, gain2, bias2)
    y_prompt = x[:mp].reshape(n_batch, seq, d)
    y_sample = x[mp:].reshape(n_b, n_tok, d)
    return (y_prompt, y_sample, jnp.stack(swa_p), jnp.stack(swa_s),
            jnp.stack(dil_p[0]), jnp.stack(dil_s[0]), jnp.stack(dil_p[1]), jnp.stack(dil_s[1]),
            jnp.stack(dil_p[2]), jnp.stack(dil_s[2]), jnp.stack(moba_p), jnp.stack(moba_s))
```

```python
import functools
import math

import jax
import jax.numpy as jnp
from jax import lax
from jax.experimental import pallas as pl
from jax.experimental.pallas import tpu as pltpu

F32, BF16, I32 = jnp.float32, jnp.bfloat16, jnp.int32

D_MODEL = 2048
HEAD_DIM = 64
N_HEADS = 32
N_KV_HEADS = 8
GQA = N_HEADS // N_KV_HEADS
Q_COLS = N_HEADS * HEAD_DIM
KV_COLS = N_KV_HEADS * HEAD_DIM
QKV_COLS = Q_COLS + 2 * KV_COLS
ATTN_SCALE = HEAD_DIM ** -0.5
ROPE_THETA = 10000.0
SWA_WINDOW = 128
LOCAL_WINDOW = 128
DIL_PAIRS = ((128, 1), (512, 4), (2048, 16))
MOBA_BLOCK = 256
MOBA_TOPK = 3
PAGE_SIZE = 128
N_EXPERTS = 16
N_EXPERT_GROUPS = 4
EXPERTS_PER_GROUP = 4
D_EXPERT = D_MODEL // 4
DEPTH = 4
DEEPNORM_ALPHA = (2 * DEPTH) ** 0.25
LN_EPS = 1e-5
NEG = -1e30

LANES = 128
Q_TILE = 128
QKV_COL_TILE = 512
QKV_ROW_TILE = 512
MOE_CHUNK = 256
PAGES_PER_STEP = 32
VMEM_LIMIT = 56 * 1024 * 1024


def _params(n_axes, vmem=VMEM_LIMIT):
    return pltpu.CompilerParams(dimension_semantics=("arbitrary",) * n_axes, vmem_limit_bytes=vmem)


def _nt_dot(a, b):
    return lax.dot_general(a, b, (((1,), (1,)), ((), ())), preferred_element_type=F32)


def _row_tile(m, pref, align):
    best = m
    for t in range(align, min(pref, m) + 1, align):
        if m % t == 0:
            best = t
    return best


def _rope_tables(pos):
    half = HEAD_DIM // 2
    inv_freq = jnp.exp(jnp.arange(half, dtype=F32) * (-2.0 * math.log(ROPE_THETA) / HEAD_DIM))
    ang = pos.astype(F32)[:, None] * inv_freq[None, :]
    cos, sin = jnp.cos(ang), jnp.sin(ang)
    return jnp.tile(cos, (1, 4)), jnp.concatenate([-sin, sin, -sin, sin], axis=1)


def _qkv_rope_kernel(x_ref, w_ref, cos_ref, sin_ref, q_ref, kv_ref, *kvt_ref):
    xb = x_ref[...].astype(BF16)
    tm = xb.shape[0]
    lane = lax.broadcasted_iota(I32, (tm, LANES), 1)
    first_half = (lane & (HEAD_DIM - 1)) < HEAD_DIM // 2

    def rope(z):
        cos, sin = cos_ref[...], sin_ref[...]
        parts = []
        for c in range(z.shape[1] // LANES):
            zc = z[:, c * LANES:(c + 1) * LANES]
            other = jnp.where(first_half, pltpu.roll(zc, LANES - HEAD_DIM // 2, 1), pltpu.roll(zc, HEAD_DIM // 2, 1))
            parts.append(zc * cos + other * sin)
        return jnp.concatenate(parts, axis=1)

    n_q = Q_COLS // QKV_COL_TILE
    for j in range(QKV_COLS // QKV_COL_TILE):
        cols = slice(j * QKV_COL_TILE, (j + 1) * QKV_COL_TILE)
        z = jnp.dot(xb, w_ref[:, cols], preferred_element_type=F32)
        if j < n_q:
            q_ref[:, cols] = (rope(z) * ATTN_SCALE).astype(BF16)
        else:
            val = rope(z) if j == n_q else z
            kcols = slice((j - n_q) * QKV_COL_TILE, (j - n_q + 1) * QKV_COL_TILE)
            kv_ref[:, kcols] = val.astype(kv_ref.dtype)
            if kvt_ref:
                kvt_ref[0][kcols, :] = val.T


def _qkv_rope(x, w_bf16, cos, sin, g0, n_groups, row0, n_rows, seq):
    tm = QKV_ROW_TILE if seq is not None else n_rows
    assert n_rows % tm == 0 and row0 % tm == 0 and (seq is None or seq % tm == 0)
    blk0 = row0 // tm
    out_specs = [
        pl.BlockSpec((None, tm, Q_COLS), lambda g, i: (g, i, 0)),
        pl.BlockSpec((None, tm, 2 * KV_COLS), lambda g, i: (g, i, 0)),
    ]
    out_shape = [
        jax.ShapeDtypeStruct((n_groups, n_rows, Q_COLS), BF16),
        jax.ShapeDtypeStruct((n_groups, n_rows, 2 * KV_COLS), BF16 if seq is not None else F32),
    ]
    if seq is not None:
        per = seq // tm
        out_specs.append(pl.BlockSpec((None, None, 2 * KV_COLS, tm), lambda g, i: (g, i // per, 0, i % per)))
        out_shape.append(jax.ShapeDtypeStruct((n_groups, n_rows // seq, 2 * KV_COLS, seq), F32))
    return pl.pallas_call(
        _qkv_rope_kernel,
        grid=(n_groups, n_rows // tm),
        in_specs=[
            pl.BlockSpec((tm, D_MODEL), lambda g, i: (blk0 + i, 0)),
            pl.BlockSpec((D_MODEL, QKV_COLS), lambda g, i: (0, g0 + g)),
            pl.BlockSpec((tm, LANES), lambda g, i: (blk0 + i, 0)),
            pl.BlockSpec((tm, LANES), lambda g, i: (blk0 + i, 0)),
        ],
        out_specs=out_specs,
        out_shape=out_shape,
        compiler_params=_params(2),
        name="qkv_rope_prompt" if seq is not None else "qkv_rope_sample",
    )(x, w_bf16, cos, sin)


def _stack_q_heads(q_ref, kv):
    qa = q_ref[:, (2 * kv) * LANES:(2 * kv + 1) * LANES]
    qb = q_ref[:, (2 * kv + 1) * LANES:(2 * kv + 2) * LANES]
    return jnp.concatenate([qa[:, :HEAD_DIM], qa[:, HEAD_DIM:], qb[:, :HEAD_DIM], qb[:, HEAD_DIM:]], axis=0)


def _store_transposed_heads(ref, t_scr):
    for g in range(GQA):
        ref[:, g * KV_COLS:(g + 1) * KV_COLS] = jnp.transpose(t_scr[:, g * Q_TILE:(g + 1) * Q_TILE]).astype(ref.dtype)


def _grouped_head_order(w_o):
    return w_o.reshape(N_KV_HEADS, GQA, HEAD_DIM, -1).transpose(1, 0, 2, 3).reshape(Q_COLS, -1)


def _local_attn_kernel(*refs, has_sink, want_lse, nb, steps):
    q_ref, kc_ref, kp_ref = refs[:3]
    pos = 3
    sink_ref = None
    if has_sink:
        sink_ref = refs[pos]
        pos += 1
    o_ref = refs[pos]
    pos += 1
    lse_ref = None
    if want_lse:
        lse_ref = refs[pos]
        pos += 1
    ot_scr = refs[pos]
    lt_scr = refs[pos + 1] if want_lse else None
    t = pl.program_id(0)

    @pl.when(t >= steps)
    def _():
        o_ref[...] = jnp.zeros(o_ref.shape, o_ref.dtype)
        if want_lse:
            lse_ref[...] = jnp.zeros(lse_ref.shape, lse_ref.dtype)

    @pl.when(t < steps)
    def _():
        _local_attn_step(q_ref, kc_ref, kp_ref, sink_ref, o_ref, lse_ref, ot_scr, lt_scr, t % nb)


def _local_attn_step(q_ref, kc_ref, kp_ref, sink_ref, o_ref, lse_ref, ot_scr, lt_scr, i):
    w = LOCAL_WINDOW
    cols = GQA * Q_TILE
    ci = lax.broadcasted_iota(I32, (2 * w, cols), 0)
    qi = lax.broadcasted_iota(I32, (2 * w, cols), 1) & (Q_TILE - 1)
    delta = qi + w - ci
    mask = (delta >= 0) & (delta <= w) & ((ci >= w) | (i > 0))
    for c in range(N_KV_HEADS // 2):
        ksl = slice(c * LANES, (c + 1) * LANES)
        vsl = slice(KV_COLS + c * LANES, KV_COLS + (c + 1) * LANES)
        kpair = jnp.concatenate([kp_ref[:, ksl], kc_ref[:, ksl]], axis=0)
        vpair = jnp.concatenate([kp_ref[:, vsl], kc_ref[:, vsl]], axis=0)
        vpair_t = jnp.transpose(vpair.astype(F32)).astype(BF16)
        for half in range(2):
            kv = 2 * c + half
            hs = slice(half * HEAD_DIM, (half + 1) * HEAD_DIM)
            st = jnp.where(mask, _nt_dot(kpair[:, hs], _stack_q_heads(q_ref, kv)), NEG)
            m = jnp.max(st, axis=0, keepdims=True)
            if sink_ref is not None:
                sink = sink_ref[kv:kv + 1, :]
                m = jnp.maximum(m, sink)
            pt = jnp.exp(st - m)
            l = jnp.sum(pt, axis=0, keepdims=True)
            if sink_ref is not None:
                l = l + jnp.exp(sink - m)
            acc = jnp.dot(vpair_t[hs, :], pt.astype(BF16), preferred_element_type=F32)
            rows = slice(kv * HEAD_DIM, (kv + 1) * HEAD_DIM)
            ot_scr[rows, :] = acc / l
            if lt_scr is not None:
                lt_scr[rows, :] = jnp.broadcast_to(m + jnp.log(l), (HEAD_DIM, cols))
    _store_transposed_heads(o_ref, ot_scr)
    if lt_scr is not None:
        _store_transposed_heads(lse_ref, lt_scr)


def _local_attn_prompt(q, kvb, m, g, dil, n_batch, seq, sink_rows, want_lse):
    n_g, mp, _ = q.shape
    assert m % dil == 0 and seq % (dil * Q_TILE) == 0 and (m - mp) // dil <= Q_TILE
    nb = seq // dil // Q_TILE
    qv = q.reshape(n_g, mp // dil, dil * Q_COLS)
    kvv = kvb.reshape(n_g, mp // dil, dil * 2 * KV_COLS)
    has_sink = sink_rows is not None
    steps = n_batch * dil * nb

    def decode(t):
        tt = jnp.minimum(t, steps - 1)
        return tt // (dil * nb), (tt // nb) % dil, tt % nb

    def cur_map(t):
        n, r, i = decode(t)
        return (g, n * nb + i, r)

    def prev_map(t):
        n, r, i = decode(t)
        return (g, n * nb + jnp.maximum(i - 1, 0), r)

    def out_map(t):
        n, r, i = decode(t)
        tail = t >= steps
        return (jnp.where(tail, n_batch * nb, n * nb + i), jnp.where(tail, t - steps, r))

    in_specs = [
        pl.BlockSpec((None, Q_TILE, Q_COLS), cur_map),
        pl.BlockSpec((None, Q_TILE, 2 * KV_COLS), cur_map),
        pl.BlockSpec((None, Q_TILE, 2 * KV_COLS), prev_map),
    ]
    args = [qv, kvv, kvv]
    if has_sink:
        in_specs.append(pl.BlockSpec((N_KV_HEADS, GQA * Q_TILE), lambda t: (0, 0)))
        args.append(sink_rows)
    out_spec = pl.BlockSpec((Q_TILE, Q_COLS), out_map)
    out_specs = [out_spec]
    out_shape = [jax.ShapeDtypeStruct((m // dil, dil * Q_COLS), BF16)]
    scratch = [pltpu.VMEM((KV_COLS, GQA * Q_TILE), F32)]
    if want_lse:
        out_specs.append(out_spec)
        out_shape.append(jax.ShapeDtypeStruct((m // dil, dil * Q_COLS), F32))
        scratch.append(pltpu.VMEM((KV_COLS, GQA * Q_TILE), F32))
    outs = pl.pallas_call(
        functools.partial(_local_attn_kernel, has_sink=has_sink, want_lse=want_lse, nb=nb, steps=steps),
        grid=(steps + dil,),
        in_specs=in_specs,
        out_specs=out_specs,
        out_shape=out_shape,
        scratch_shapes=scratch,
        compiler_params=_params(1),
        name=f"local_attn_prompt_d{dil}",
    )(*args)
    return [o.reshape(m, Q_COLS) for o in outs]


def _head_block_mask(rows):
    head = lax.broadcasted_iota(I32, (rows, KV_COLS), 0) & (N_HEADS - 1)
    lane = lax.broadcasted_iota(I32, (rows, KV_COLS), 1)
    return (lane // HEAD_DIM) == (head // GQA)


def _fold_lanes(x):
    return x[:, 0:LANES] + x[:, LANES:2 * LANES] + x[:, 2 * LANES:3 * LANES] + x[:, 3 * LANES:4 * LANES]


def _padded_new_rows(n_ref):
    pad = jnp.zeros((LANES - n_ref.shape[1], KV_COLS), F32)
    knew = jnp.concatenate([n_ref[0, :, 0:KV_COLS], pad], axis=0).astype(BF16)
    vnew = jnp.concatenate([n_ref[0, :, KV_COLS:2 * KV_COLS], pad], axis=0).astype(BF16)
    return knew, vnew


def _sample_attn_kernel(*refs, dil, n_tok, has_sink, want_lse):
    q_ref, c_ref, n_ref = refs[:3]
    pos = 3
    sink_ref = None
    if has_sink:
        sink_ref = refs[pos]
        pos += 1
    o_ref = refs[pos]
    lse_ref = refs[pos + 1] if want_lse else None
    rows = n_tok * N_HEADS
    win = c_ref.shape[-1]
    q = q_ref[0]
    knew, vnew = _padded_new_rows(n_ref)
    tok_c = lax.broadcasted_iota(I32, (rows, win), 0) // N_HEADS
    pos_c = lax.broadcasted_iota(I32, (rows, win), 1)
    valid_c = (((pos_c - tok_c) & (dil - 1)) == 0) & (pos_c >= tok_c)
    tok_n = lax.broadcasted_iota(I32, (rows, LANES), 0) // N_HEADS
    pos_n = lax.broadcasted_iota(I32, (rows, LANES), 1)
    valid_n = (((tok_n - pos_n) & (dil - 1)) == 0) & (pos_n <= tok_n)
    s_c = jnp.where(valid_c, jnp.dot(q, c_ref[0, 0].astype(BF16), preferred_element_type=F32), NEG)
    s_n = jnp.where(valid_n, _nt_dot(q, knew), NEG)
    m = jnp.maximum(jnp.max(s_c, axis=1, keepdims=True), jnp.max(s_n, axis=1, keepdims=True))
    if has_sink:
        sink = sink_ref[...]
        m = jnp.maximum(m, sink)
    p_c = jnp.exp(s_c - m)
    p_n = jnp.exp(s_n - m)
    l = jnp.sum(p_c, axis=1, keepdims=True) + jnp.sum(p_n, axis=1, keepdims=True)
    if has_sink:
        l = l + jnp.exp(sink - m)
    of = (_nt_dot(p_c.astype(BF16), c_ref[0, 1].astype(BF16))
          + jnp.dot(p_n.astype(BF16), vnew, preferred_element_type=F32))
    o_ref[0] = _fold_lanes(jnp.where(_head_block_mask(rows), of, 0.0)) / l
    if want_lse:
        lse_ref[0] = jnp.broadcast_to(m + jnp.log(l), (rows, LANES))


def _sample_attn(qbd, cache_t, kv_new, dil, sink_col, want_lse):
    n_b, rows, _ = qbd.shape
    n_tok = rows // N_HEADS
    win = cache_t.shape[-1]
    assert win == LOCAL_WINDOW * dil and (dil & (dil - 1)) == 0 and n_tok <= 8
    has_sink = sink_col is not None
    in_specs = [
        pl.BlockSpec((1, rows, KV_COLS), lambda b: (b, 0, 0)),
        pl.BlockSpec((1, 2, KV_COLS, win), lambda b: (b, 0, 0, 0)),
        pl.BlockSpec((1, 8, 2 * KV_COLS), lambda b: (b, 0, 0)),
    ]
    args = [qbd, cache_t, kv_new]
    if has_sink:
        in_specs.append(pl.BlockSpec((rows, 1), lambda b: (0, 0)))
        args.append(sink_col)
    out_spec = pl.BlockSpec((1, rows, LANES), lambda b: (b, 0, 0))
    out_specs = [out_spec]
    out_shape = [jax.ShapeDtypeStruct((n_b, rows, LANES), F32)]
    if want_lse:
        out_specs.append(out_spec)
        out_shape.append(jax.ShapeDtypeStruct((n_b, rows, LANES), F32))
    return pl.pallas_call(
        functools.partial(_sample_attn_kernel, dil=dil, n_tok=n_tok, has_sink=has_sink, want_lse=want_lse),
        grid=(n_b,),
        in_specs=in_specs,
        out_specs=out_specs,
        out_shape=out_shape,
        compiler_params=_params(1),
        name=f"local_attn_sample_d{dil}",
    )(*args)


def _block_mean_kernel(k_ref, o_ref):
    n_blk = k_ref.shape[1] // MOBA_BLOCK
    cols = [jnp.mean(k_ref[:, b * MOBA_BLOCK:(b + 1) * MOBA_BLOCK], axis=1, keepdims=True) for b in range(n_blk)]
    lane = lax.broadcasted_iota(I32, (k_ref.shape[0], LANES), 1)
    out = jnp.zeros((k_ref.shape[0], LANES), F32)
    for b in range(n_blk):
        out = jnp.where(lane == b, cols[b], out)
    o_ref[...] = out


def _block_means(kvt):
    n_batch, _, seq = kvt.shape
    assert seq // MOBA_BLOCK <= LANES
    return pl.pallas_call(
        _block_mean_kernel,
        grid=(n_batch,),
        in_specs=[pl.BlockSpec((None, KV_COLS, seq), lambda n: (n, 0, 0))],
        out_specs=pl.BlockSpec((None, KV_COLS, LANES), lambda n: (n, 0, 0)),
        out_shape=jax.ShapeDtypeStruct((n_batch, KV_COLS, LANES), F32),
        compiler_params=_params(1),
        name="moba_block_means",
    )(kvt)


def _moba_prompt_kernel(q_ref, kv_ref, km_ref, o_ref, *scratch, nq, steps, n_blk):
    t = pl.program_id(0)

    @pl.when(t >= steps)
    def _():
        o_ref[...] = jnp.zeros(o_ref.shape, o_ref.dtype)

    @pl.when(t < steps)
    def _():
        _moba_prompt_step(q_ref, kv_ref, km_ref, o_ref, *scratch, t % nq, n_blk)


def _moba_prompt_step(q_ref, kv_ref, km_ref, o_ref, vt_scr, q4_scr, bias_scr, m_scr, l_scr, acc_scr, i, n_blk):
    own = (i * Q_TILE) // MOBA_BLOCK
    cols = GQA * Q_TILE

    @pl.when(i == 0)
    def _():
        for b in range(n_blk):
            for c in range(KV_COLS // LANES):
                v = kv_ref[b * MOBA_BLOCK:(b + 1) * MOBA_BLOCK, KV_COLS + c * LANES:KV_COLS + (c + 1) * LANES]
                vt_scr[c * LANES:(c + 1) * LANES, b * MOBA_BLOCK:(b + 1) * MOBA_BLOCK] = (
                    jnp.transpose(v.astype(F32)).astype(BF16))

    blk_rows = bias_scr.shape[1]
    bid = lax.broadcasted_iota(I32, (blk_rows, cols), 0)
    q_off = (i * Q_TILE) % MOBA_BLOCK + (lax.broadcasted_iota(I32, (MOBA_BLOCK, cols), 1) & (Q_TILE - 1))
    own_bias = jnp.where(lax.broadcasted_iota(I32, (MOBA_BLOCK, cols), 0) <= q_off, 0.0, NEG)
    km_t = jnp.transpose(km_ref[...])[0:blk_rows, :].astype(BF16)
    for kv in range(N_KV_HEADS):
        q4 = _stack_q_heads(q_ref, kv)
        q4_scr[kv] = q4
        gate = _nt_dot(km_t[:, kv * HEAD_DIM:(kv + 1) * HEAD_DIM], q4)
        gate = jnp.where(bid < own, gate, NEG)
        rank = jnp.zeros((blk_rows, cols), F32)
        for b in range(n_blk):
            gb = gate[b:b + 1, :]
            rank = rank + jnp.where((gb > gate) | ((gb == gate) & (b < bid)), 1.0, 0.0)
        bias_scr[kv] = jnp.where((rank < min(MOBA_TOPK, n_blk)) & (bid < own), 0.0, NEG)
    m_scr[...] = jnp.full(m_scr.shape, NEG, F32)
    l_scr[...] = jnp.zeros(l_scr.shape, F32)
    acc_scr[...] = jnp.zeros(acc_scr.shape, F32)

    def update(kv, start, bias):
        c = kv // 2
        hs = slice((kv % 2) * HEAD_DIM, (kv % 2 + 1) * HEAD_DIM)
        rows = slice(kv * HEAD_DIM, (kv + 1) * HEAD_DIM)
        k = kv_ref[pl.ds(start, MOBA_BLOCK), c * LANES:(c + 1) * LANES][:, hs]
        st = _nt_dot(k, q4_scr[kv]) + bias
        m_old = m_scr[kv, 0:1, :]
        m_new = jnp.maximum(m_old, jnp.max(st, axis=0, keepdims=True))
        pt = jnp.exp(st - m_new)
        alpha = jnp.exp(m_old - m_new)
        pv = jnp.dot(vt_scr[rows, pl.ds(start, MOBA_BLOCK)], pt.astype(BF16), preferred_element_type=F32)
        acc_scr[rows, :] = alpha * acc_scr[rows, :] + pv
        l_scr[kv, 0:1, :] = alpha * l_scr[kv, 0:1, :] + jnp.sum(pt, axis=0, keepdims=True)
        m_scr[kv, 0:1, :] = m_new

    def body(j, carry):
        start = pl.multiple_of(j * MOBA_BLOCK, MOBA_BLOCK)
        for kv in range(N_KV_HEADS):
            update(kv, start, bias_scr[kv, pl.ds(j, 1), :])
        return carry

    lax.fori_loop(0, own, body, 0)
    for kv in range(N_KV_HEADS):
        update(kv, pl.multiple_of(own * MOBA_BLOCK, MOBA_BLOCK), own_bias)
    for kv in range(N_KV_HEADS):
        rows = slice(kv * HEAD_DIM, (kv + 1) * HEAD_DIM)
        acc_scr[rows, :] = acc_scr[rows, :] / l_scr[kv, 0:1, :]
    _store_transposed_heads(o_ref, acc_scr)


def _moba_prompt(q, kvb, kmean, m, n_batch, seq):
    nq = seq // Q_TILE
    n_blk = seq // MOBA_BLOCK
    steps = n_batch * nq
    assert m - steps * Q_TILE <= Q_TILE
    batch_of = lambda t: jnp.minimum(t, steps - 1) // nq
    return pl.pallas_call(
        functools.partial(_moba_prompt_kernel, nq=nq, steps=steps, n_blk=n_blk),
        grid=(steps + 1,),
        in_specs=[
            pl.BlockSpec((Q_TILE, Q_COLS), lambda t: (jnp.minimum(t, steps - 1), 0)),
            pl.BlockSpec((seq, 2 * KV_COLS), lambda t: (batch_of(t), 0)),
            pl.BlockSpec((None, KV_COLS, LANES), lambda t: (batch_of(t), 0, 0)),
        ],
        out_specs=pl.BlockSpec((Q_TILE, Q_COLS), lambda t: (t, 0)),
        out_shape=jax.ShapeDtypeStruct((m, Q_COLS), BF16),
        scratch_shapes=[
            pltpu.VMEM((KV_COLS, seq), BF16),
            pltpu.VMEM((N_KV_HEADS, GQA * Q_TILE, HEAD_DIM), BF16),
            pltpu.VMEM((N_KV_HEADS, -(-n_blk // 8) * 8, GQA * Q_TILE), F32),
            pltpu.VMEM((N_KV_HEADS, 8, GQA * Q_TILE), F32),
            pltpu.VMEM((N_KV_HEADS, 8, GQA * Q_TILE), F32),
            pltpu.VMEM((KV_COLS, GQA * Q_TILE), F32),
        ],
        compiler_params=_params(1),
        name="moba_prompt",
    )(q, kvb, kmean)


def _top_blocks(gate, lane, k):
    chosen = jnp.zeros(gate.shape, F32)
    for _ in range(k):
        mx = jnp.max(gate, axis=1, keepdims=True)
        idx = jnp.min(jnp.where(gate == mx, lane, LANES), axis=1, keepdims=True)
        hit = lane == idx
        chosen = jnp.where(hit & (mx > 0.5 * NEG), 1.0, chosen)
        gate = jnp.where(hit, NEG, gate)
    return chosen


def _moba_sample_kernel(pt_ref, q_ref, n_ref, *refs, n_chunks, n_tok):
    page_refs = refs[:PAGES_PER_STEP]
    o_ref, s_scr, acc_scr, l_scr, m_scr, ch_scr = refs[PAGES_PER_STEP:]
    c = pl.program_id(1)
    rows = q_ref.shape[1]
    step_keys = PAGES_PER_STEP * PAGE_SIZE
    blk_per_step = step_keys // MOBA_BLOCK
    n_blk = n_chunks * blk_per_step
    lane = lax.broadcasted_iota(I32, (rows, LANES), 1)

    @pl.when(c < n_chunks)
    def _scores():
        kt = jnp.concatenate([r[...] for r in page_refs], axis=1).astype(BF16)
        s_scr[:, pl.ds(pl.multiple_of(c * step_keys, step_keys), step_keys)] = (
            jnp.dot(q_ref[0], kt, preferred_element_type=F32))

    @pl.when(c == n_chunks)
    def _select():
        gate = jnp.full((rows, LANES), NEG, F32)
        bmax = jnp.full((rows, LANES), NEG, F32)
        for b in range(n_blk):
            sb = s_scr[:, b * MOBA_BLOCK:(b + 1) * MOBA_BLOCK]
            gate = jnp.where(lane == b, jnp.sum(sb, axis=1, keepdims=True), gate)
            bmax = jnp.where(lane == b, jnp.max(sb, axis=1, keepdims=True), bmax)
        chosen = _top_blocks(gate, lane, min(MOBA_TOPK, n_blk))
        knew, vnew = _padded_new_rows(n_ref)
        tok = lax.broadcasted_iota(I32, (rows, LANES), 0) // N_HEADS
        valid = (lane <= tok) & (lane < n_tok)
        s_new = jnp.where(valid, _nt_dot(q_ref[0], knew), NEG)
        m = jnp.maximum(jnp.max(jnp.where(chosen > 0.5, bmax, NEG), axis=1, keepdims=True),
                        jnp.max(s_new, axis=1, keepdims=True))
        p_new = jnp.where(valid, jnp.exp(s_new - m), 0.0)
        acc_scr[...] = jnp.dot(p_new.astype(BF16), vnew, preferred_element_type=F32)
        l_scr[...] = jnp.sum(p_new, axis=1, keepdims=True)
        m_scr[...] = m
        ch_scr[...] = chosen

    @pl.when(c >= n_chunks)
    def _accumulate():
        cc = c - n_chunks
        vt = jnp.concatenate([r[...] for r in page_refs], axis=1).astype(BF16)
        s = s_scr[:, pl.ds(pl.multiple_of(cc * step_keys, step_keys), step_keys)]
        chosen = ch_scr[...]
        keep = jnp.concatenate(
            [jnp.broadcast_to(jnp.max(jnp.where(lane == cc * blk_per_step + u, chosen, 0.0), axis=1, keepdims=True) > 0.5,
                              (rows, MOBA_BLOCK)) for u in range(blk_per_step)], axis=1)
        p = jnp.where(keep, jnp.exp(s - m_scr[...]), 0.0)
        l_scr[...] += jnp.sum(p, axis=1, keepdims=True)
        acc_scr[...] += _nt_dot(p.astype(BF16), vt)

    @pl.when(c == 2 * n_chunks - 1)
    def _finish():
        o_ref[0] = _fold_lanes(jnp.where(_head_block_mask(rows), acc_scr[...], 0.0)) / l_scr[...]


def _moba_sample(qbd, kv_new, cache_t, page_table):
    n_b, rows, _ = qbd.shape
    n_tok = rows // N_HEADS
    n_pages = page_table.shape[1]
    assert n_pages % PAGES_PER_STEP == 0 and (n_pages * PAGE_SIZE) % MOBA_BLOCK == 0
    assert n_pages * PAGE_SIZE // MOBA_BLOCK <= LANES
    n_chunks = n_pages // PAGES_PER_STEP

    def page_map(u):
        return lambda b, c, pt: (pt[b, (c % n_chunks) * PAGES_PER_STEP + u], c // n_chunks, 0, 0)

    page_specs = [pl.BlockSpec((None, None, KV_COLS, PAGE_SIZE), page_map(u)) for u in range(PAGES_PER_STEP)]
    grid_spec = pltpu.PrefetchScalarGridSpec(
        num_scalar_prefetch=1,
        grid=(n_b, 2 * n_chunks),
        in_specs=[
            pl.BlockSpec((1, rows, KV_COLS), lambda b, c, pt: (b, 0, 0)),
            pl.BlockSpec((1, 8, 2 * KV_COLS), lambda b, c, pt: (b, 0, 0)),
        ] + page_specs,
        out_specs=pl.BlockSpec((1, rows, LANES), lambda b, c, pt: (b, 0, 0)),
        scratch_shapes=[
            pltpu.VMEM((rows, n_pages * PAGE_SIZE), F32),
            pltpu.VMEM((rows, KV_COLS), F32),
            pltpu.VMEM((rows, 1), F32),
            pltpu.VMEM((rows, 1), F32),
            pltpu.VMEM((rows, LANES), F32),
        ],
    )
    return pl.pallas_call(
        functools.partial(_moba_sample_kernel, n_chunks=n_chunks, n_tok=n_tok),
        grid_spec=grid_spec,
        out_shape=jax.ShapeDtypeStruct((n_b, rows, LANES), F32),
        compiler_params=_params(2),
        name="moba_sample",
    )(page_table, qbd, kv_new, *([cache_t] * PAGES_PER_STEP))


def _layer_norm(h, gain, bias):
    mu = jnp.mean(h, axis=-1, keepdims=True)
    d = h - mu
    var = jnp.mean(d * d, axis=-1, keepdims=True)
    return d * lax.rsqrt(var + LN_EPS) * gain + bias


def _oproj_ln_kernel(*refs, n_merge):
    x_ref = refs[0]
    o_refs = refs[1:1 + n_merge]
    w_ref, g_ref, b_ref, out_ref, rows_ref = refs[-5:]
    if n_merge == 1:
        o = o_refs[0][...]
    else:
        lses = [r[...] for r in refs[1 + n_merge:1 + 2 * n_merge]]
        mx = functools.reduce(jnp.maximum, lses)
        es = [jnp.exp(l - mx) for l in lses]
        num = sum(e * r[...].astype(F32) for r, e in zip(o_refs, es))
        o = (num / sum(es)).astype(BF16)
    y = jnp.dot(o, w_ref[...], preferred_element_type=F32)
    h = _layer_norm(DEEPNORM_ALPHA * x_ref[...] + y, g_ref[...], b_ref[...])
    out_ref[...] = h
    for j in range(D_MODEL // LANES):
        rows_ref[:, j, :] = h[:, j * LANES:(j + 1) * LANES]


def _oproj_ln(x, o, lse, w_bf16, gain, bias):
    m = x.shape[0]
    n_merge = len(o)
    tm = _row_tile(m, 320 if n_merge == 1 else 160, 16)
    row = pl.BlockSpec((tm, D_MODEL), lambda i: (i, 0))
    vec = pl.BlockSpec((1, D_MODEL), lambda i: (0, 0))
    wspec = pl.BlockSpec((Q_COLS, D_MODEL), lambda i: (0, 0))
    groups = list(o) + (list(lse) if n_merge > 1 else [])
    in_specs = [row] + [pl.BlockSpec((tm, Q_COLS), lambda i: (i, 0))] * len(groups) + [wspec, vec, vec]
    args = [x] + groups + [w_bf16, gain, bias]
    return pl.pallas_call(
        functools.partial(_oproj_ln_kernel, n_merge=n_merge),
        grid=(m // tm,),
        in_specs=in_specs,
        out_specs=[row, pl.BlockSpec((tm, D_MODEL // LANES, LANES), lambda i: (i, 0, 0))],
        out_shape=[jax.ShapeDtypeStruct((m, D_MODEL), F32),
                   jax.ShapeDtypeStruct((m, D_MODEL // LANES, LANES), F32)],
        compiler_params=_params(1),
        name=f"oproj_ln_m{n_merge}",
    )(*args)


def _router_kernel(x_ref, rwt_ref, bias_ref, idx_ref, gate_ref, cnt_ref, carry_scr):
    i = pl.program_id(0)
    tm = x_ref.shape[0]

    @pl.when(i == 0)
    def _():
        carry_scr[...] = jnp.zeros(carry_scr.shape, F32)

    logits = _nt_dot(rwt_ref[...], x_ref[...].astype(BF16))
    scores = 1.0 / (1.0 + jnp.exp(-logits))
    biased = scores + bias_ref[...]
    srow = [scores[e:e + 1, :] for e in range(N_EXPERTS)]
    brow = [biased[e:e + 1, :] for e in range(N_EXPERTS)]

    gscore = []
    for g in range(N_EXPERT_GROUPS):
        a0, a1, a2, a3 = brow[4 * g:4 * g + 4]
        hi01, lo01 = jnp.maximum(a0, a1), jnp.minimum(a0, a1)
        hi23, lo23 = jnp.maximum(a2, a3), jnp.minimum(a2, a3)
        gscore.append(jnp.maximum(hi01, hi23) + jnp.maximum(jnp.minimum(hi01, hi23), jnp.maximum(lo01, lo23)))
    gsel = jnp.zeros((1, tm), I32)
    best = gscore[0]
    for g in range(1, N_EXPERT_GROUPS):
        better = gscore[g] > best
        gsel = jnp.where(better, g, gsel)
        best = jnp.where(better, gscore[g], best)

    def in_group(rows_, j):
        v = rows_[j]
        for g in range(1, N_EXPERT_GROUPS):
            v = jnp.where(gsel == g, rows_[4 * g + j], v)
        return v

    ab = [in_group(brow, j) for j in range(EXPERTS_PER_GROUP)]
    au = [in_group(srow, j) for j in range(EXPERTS_PER_GROUP)]
    i1 = jnp.zeros((1, tm), I32)
    v1 = ab[0]
    for j in range(1, EXPERTS_PER_GROUP):
        better = ab[j] > v1
        i1 = jnp.where(better, j, i1)
        v1 = jnp.where(better, ab[j], v1)
    i2 = jnp.full((1, tm), -1, I32)
    v2 = jnp.full((1, tm), -jnp.inf, F32)
    for j in range(EXPERTS_PER_GROUP):
        better = (i1 != j) & ((ab[j] > v2) | (i2 < 0))
        i2 = jnp.where(better, j, i2)
        v2 = jnp.where(better, ab[j], v2)

    def pick(vals, idx):
        v = vals[0]
        for j in range(1, EXPERTS_PER_GROUP):
            v = jnp.where(idx == j, vals[j], v)
        return v

    s1, s2 = pick(au, i1), pick(au, i2)
    den = s1 + s2
    e1 = gsel * EXPERTS_PER_GROUP + i1
    e2 = gsel * EXPERTS_PER_GROUP + i2

    eid = lax.broadcasted_iota(I32, (N_EXPERTS, tm), 0)
    hit1, hit2 = eid == e1, eid == e2
    onehot = jnp.where(hit1 | hit2, 1.0, 0.0)
    before = lax.broadcasted_iota(I32, (tm, tm), 0) < lax.broadcasted_iota(I32, (tm, tm), 1)
    prefix = jnp.dot(onehot.astype(BF16), jnp.where(before, 1.0, 0.0).astype(BF16), preferred_element_type=F32)
    offset = carry_scr[...][:, 0:1] + prefix
    r1 = jnp.sum(jnp.where(hit1, offset, 0.0), axis=0, keepdims=True).astype(I32)
    r2 = jnp.sum(jnp.where(hit2, offset, 0.0), axis=0, keepdims=True).astype(I32)
    carry_scr[...] = carry_scr[...] + jnp.sum(onehot, axis=1, keepdims=True)

    row8 = lax.broadcasted_iota(I32, (8, tm), 0)
    idx_ref[...] = jnp.where(row8 == 0, e1, jnp.where(row8 == 1, e2, jnp.where(row8 == 2, r1, jnp.where(row8 == 3, r2, 0))))
    gate_ref[...] = jnp.where(row8 == 0, s1 / den, jnp.where(row8 == 1, s2 / den, 0.0))
    cnt_ref[...] = carry_scr[...]


def _router(x, rwt_bf16, bias_col):
    m = x.shape[0]
    tm = _row_tile(m, 640, LANES)
    return pl.pallas_call(
        _router_kernel,
        grid=(m // tm,),
        in_specs=[
            pl.BlockSpec((tm, D_MODEL), lambda i: (i, 0)),
            pl.BlockSpec((N_EXPERTS, D_MODEL), lambda i: (0, 0)),
            pl.BlockSpec((N_EXPERTS, 1), lambda i: (0, 0)),
        ],
        out_specs=[
            pl.BlockSpec((8, tm), lambda i: (0, i)),
            pl.BlockSpec((8, tm), lambda i: (0, i)),
            pl.BlockSpec((N_EXPERTS, LANES), lambda i: (0, 0)),
        ],
        out_shape=[
            jax.ShapeDtypeStruct((8, m), I32),
            jax.ShapeDtypeStruct((8, m), F32),
            jax.ShapeDtypeStruct((N_EXPERTS, LANES), F32),
        ],
        scratch_shapes=[pltpu.VMEM((N_EXPERTS, LANES), F32)],
        compiler_params=_params(1),
        name="router",
    )(x, rwt_bf16, bias_col)


def _start_row_gather(src_hbm, row_of, n_rows, dst, sem):
    def body(r, carry):
        pltpu.make_async_copy(src_hbm.at[pl.ds(row_of(r), 1)], dst.at[pl.ds(r, 1)], sem).start()
        return carry
    lax.fori_loop(0, n_rows, body, 0, unroll=8)


def _wait_row_gather(src_hbm, n_rows, dst, sem):
    pltpu.make_async_copy(src_hbm.at[pl.ds(0, n_rows)], dst, sem).wait()


def _start_row_gather_inline(src_hbm, idx_ref, base, n_rows, dst, sem):
    for r in range(n_rows):
        pltpu.make_async_copy(src_hbm.at[pl.ds(idx_ref[base + r], 1)], dst.at[pl.ds(r, 1)], sem).start()


def _moe_ffn_kernel(ce_ref, tok_ref, x_hbm, wg_ref, wu_ref, wd_ref, y_ref, xbuf0, xbuf1, sem, wgb, wub, wdb, *,
                    n_chunks):
    c = pl.program_id(0)
    bufs = (xbuf0, xbuf1)

    @pl.when(c == 0)
    def _():
        _start_row_gather(x_hbm, lambda r: tok_ref[r], MOE_CHUNK, xbuf0, sem.at[0])

    @pl.when((c == 0) | (ce_ref[c] != ce_ref[jnp.maximum(c - 1, 0)]))
    def _():
        wgb[...] = wg_ref[...].astype(BF16)
        wub[...] = wu_ref[...].astype(BF16)
        wdb[...] = wd_ref[...].astype(BF16)

    def run(cur):
        _wait_row_gather(x_hbm, MOE_CHUNK, bufs[cur], sem.at[cur])
        _start_row_gather_inline(x_hbm, tok_ref, (c + 1) * MOE_CHUNK, MOE_CHUNK, bufs[1 - cur], sem.at[1 - cur])
        xb = jnp.concatenate([bufs[cur][:, j, :] for j in range(D_MODEL // LANES)], axis=1).astype(BF16)
        a = jnp.dot(xb, wgb[...], preferred_element_type=F32)
        u = jnp.dot(xb, wub[...], preferred_element_type=F32)
        h = (a / (1.0 + jnp.exp(-a))) * u
        y_ref[...] = jnp.dot(h.astype(BF16), wdb[...], preferred_element_type=F32)

    @pl.when(c % 2 == 0)
    def _():
        run(0)

    @pl.when(c % 2 == 1)
    def _():
        run(1)

    @pl.when(c == n_chunks - 1)
    def _():
        _wait_row_gather(x_hbm, MOE_CHUNK, bufs[n_chunks % 2], sem.at[n_chunks % 2])


def _moe_ffn(x, chunk_expert, tok_of_row, w_gate, w_up, w_down, layer):
    n_chunks = chunk_expert.shape[0]
    assert tok_of_row.shape[0] == (n_chunks + 1) * MOE_CHUNK
    w_in = pl.BlockSpec((None, None, D_MODEL, D_EXPERT), lambda c, ce, tok: (layer, ce[c], 0, 0))
    grid_spec = pltpu.PrefetchScalarGridSpec(
        num_scalar_prefetch=2,
        grid=(n_chunks,),
        in_specs=[
            pl.BlockSpec(memory_space=pl.ANY),
            w_in,
            w_in,
            pl.BlockSpec((None, None, D_EXPERT, D_MODEL), lambda c, ce, tok: (layer, ce[c], 0, 0)),
        ],
        out_specs=pl.BlockSpec((MOE_CHUNK, D_MODEL), lambda c, ce, tok: (c, 0)),
        scratch_shapes=[
            pltpu.VMEM((MOE_CHUNK, D_MODEL // LANES, LANES), F32),
            pltpu.VMEM((MOE_CHUNK, D_MODEL // LANES, LANES), F32),
            pltpu.SemaphoreType.DMA((2,)),
            pltpu.VMEM((D_MODEL, D_EXPERT), BF16),
            pltpu.VMEM((D_MODEL, D_EXPERT), BF16),
            pltpu.VMEM((D_EXPERT, D_MODEL), BF16),
        ],
    )
    return pl.pallas_call(
        functools.partial(_moe_ffn_kernel, n_chunks=n_chunks),
        grid_spec=grid_spec,
        out_shape=jax.ShapeDtypeStruct((n_chunks * MOE_CHUNK, D_MODEL), F32),
        compiler_params=_params(1),
        name="moe_ffn",
    )(chunk_expert, tok_of_row, x, w_gate, w_up, w_down)


def _combine_ln_kernel(dest_ref, y_hbm, x_ref, g1_ref, g2_ref, gain_ref, bias_ref, out_ref, ybuf0, ybuf1, sem, *,
                       n_tiles):
    i = pl.program_id(0)
    tm = x_ref.shape[0]
    bufs = (ybuf0, ybuf1)

    @pl.when(i == 0)
    def _():
        _start_row_gather(y_hbm, lambda r: dest_ref[r], 2 * tm, ybuf0, sem.at[0])

    def run(cur):
        _wait_row_gather(y_hbm, 2 * tm, bufs[cur], sem.at[cur])
        _start_row_gather_inline(y_hbm, dest_ref, (i + 1) * 2 * tm, 2 * tm, bufs[1 - cur], sem.at[1 - cur])
        f = g1_ref[...] * bufs[cur][0:tm, :] + g2_ref[...] * bufs[cur][tm:2 * tm, :]
        out_ref[...] = _layer_norm(DEEPNORM_ALPHA * x_ref[...] + f, gain_ref[...], bias_ref[...])

    @pl.when(i % 2 == 0)
    def _():
        run(0)

    @pl.when(i % 2 == 1)
    def _():
        run(1)

    @pl.when(i == n_tiles - 1)
    def _():
        _wait_row_gather(y_hbm, 2 * tm, bufs[n_tiles % 2], sem.at[n_tiles % 2])


def _combine_ln(x, ybuf, dest, g1, g2, gain, bias):
    m = x.shape[0]
    tm = _row_tile(m, 128, 8)
    n_tiles = m // tm
    dest_tiles = dest.reshape(2, n_tiles, tm).transpose(1, 0, 2).reshape(-1)
    dest_tiles = jnp.concatenate([dest_tiles, jnp.zeros((2 * tm,), I32)])
    row = pl.BlockSpec((tm, D_MODEL), lambda i, d: (i, 0))
    col = pl.BlockSpec((tm, 1), lambda i, d: (i, 0))
    vec = pl.BlockSpec((1, D_MODEL), lambda i, d: (0, 0))
    grid_spec = pltpu.PrefetchScalarGridSpec(
        num_scalar_prefetch=1,
        grid=(n_tiles,),
        in_specs=[pl.BlockSpec(memory_space=pl.ANY), row, col, col, vec, vec],
        out_specs=row,
        scratch_shapes=[pltpu.VMEM((2 * tm, D_MODEL), F32), pltpu.VMEM((2 * tm, D_MODEL), F32),
                        pltpu.SemaphoreType.DMA((2,))],
    )
    return pl.pallas_call(
        functools.partial(_combine_ln_kernel, n_tiles=n_tiles),
        grid_spec=grid_spec,
        out_shape=jax.ShapeDtypeStruct((m, D_MODEL), F32),
        compiler_params=_params(1),
        name="moe_combine_ln",
    )(dest_tiles, ybuf, x, g1, g2, gain, bias)


def _moe_layer(x, x_rows, rwt_bf16, rbias_col, w_gate, w_up, w_down, layer, gain, bias):
    m = x.shape[0]
    idx, gates, counts = _router(x, rwt_bf16, rbias_col)
    experts, ranks = idx[0:2], idx[2:4]
    cnt = counts[:, 0].astype(I32)
    padded = (cnt + MOE_CHUNK - 1) // MOE_CHUNK * MOE_CHUNK
    pend = jnp.cumsum(padded)
    pstart = pend - padded
    eid = jnp.arange(N_EXPERTS, dtype=I32)[:, None, None]
    dest = ranks + jnp.sum(jnp.where(experts[None] == eid, pstart[:, None, None], 0), axis=0)
    n_chunks = -(-2 * m // MOE_CHUNK) + N_EXPERTS
    chunk_start = jnp.arange(n_chunks, dtype=I32) * MOE_CHUNK
    chunk_expert = jnp.minimum(jnp.sum((pend[None, :] <= chunk_start[:, None]).astype(I32), axis=1), N_EXPERTS - 1)
    tok = jnp.tile(jnp.arange(m, dtype=I32), 2)
    tok_of_row = jnp.zeros(((n_chunks + 1) * MOE_CHUNK,), I32).at[dest.reshape(-1)].set(tok)
    ybuf = _moe_ffn(x_rows, chunk_expert, tok_of_row, w_gate, w_up, w_down, layer)
    return _combine_ln(x, ybuf, dest, gates[0].reshape(m, 1), gates[1].reshape(m, 1), gain, bias)


def _block_diag_queries(q_rows, n_b, n_tok):
    q4 = q_rows.reshape(n_b, n_tok, N_HEADS, HEAD_DIM)
    tiled = jnp.tile(q4, (1, 1, 1, N_KV_HEADS))
    head = jnp.arange(N_HEADS)[:, None] // GQA
    lane_kv = jnp.arange(KV_COLS)[None, :] // HEAD_DIM
    return jnp.where(head == lane_kv, tiled, jnp.zeros((), tiled.dtype)).reshape(n_b, n_tok * N_HEADS, KV_COLS)


def _sample_rows_grouped(o, n_b, n_tok):
    o = o.reshape(n_b, n_tok, N_KV_HEADS, GQA, LANES)
    return o.transpose(0, 1, 3, 2, 4)


def _sample_out(o, n_b, n_tok):
    o = _sample_rows_grouped(o, n_b, n_tok)
    return (o[..., :HEAD_DIM] + o[..., HEAD_DIM:]).reshape(n_b * n_tok, Q_COLS)


def _sample_lse(lse, n_b, n_tok):
    return _sample_rows_grouped(lse, n_b, n_tok)[..., :HEAD_DIM].reshape(n_b * n_tok, Q_COLS)


def _new_kv_rows(kv_s, n_b, n_tok):
    return jnp.pad(kv_s.reshape(n_b, n_tok, 2 * KV_COLS), ((0, 0), (0, 8 - n_tok), (0, 0)))


def _position_minor(cache):
    lead = cache.shape[:-4]
    n = len(lead)
    t = cache.transpose(tuple(range(n)) + (n + 1, n + 2, n + 3, n))
    return t.reshape(lead + (2, KV_COLS, cache.shape[-4]))


def _position_major(kvt):
    lead = kvt.shape[:-2]
    n = len(lead)
    t = kvt.reshape(lead + (2, N_KV_HEADS, HEAD_DIM, kvt.shape[-1]))
    return t.transpose(tuple(range(n)) + (n + 3, n, n + 1, n + 2))


def _window_state(kvt, kv_s, cache, n_b, n_tok, window):
    seq = kvt.shape[-1]
    state_p = _position_major(kvt[..., seq - min(window, seq):])
    kvs = kv_s.reshape(n_b, n_tok, 2, N_KV_HEADS, HEAD_DIM)
    state_s = jnp.concatenate([cache, kvs], axis=1)[:, -cache.shape[1]:]
    return state_p, state_s


def kernel(x_prompt, x_sample, cache_swa, cache_dil0, cache_dil1, cache_dil2, cache_moba, page_table,
           w_qkv_swa, w_o_swa, sinks_swa, w_qkv_dil, w_o_dil, w_qkv_moba, w_o_moba,
           ln_gain, ln_bias, router_w, router_bias, w_gate_e, w_up_e, w_down_e):
    n_batch, seq, d = x_prompt.shape
    n_b, n_tok = x_sample.shape[:2]
    past_len = page_table.shape[1] * cache_moba.shape[2]
    mp, ms = n_batch * seq, n_b * n_tok
    m = mp + ms
    assert d == D_MODEL and cache_moba.shape[2] == PAGE_SIZE and n_tok <= 4 and ms == Q_TILE and mp % ms == 0
    x = jnp.concatenate([x_prompt.reshape(mp, d), x_sample.reshape(ms, d)], axis=0)
    pos = jnp.concatenate([jnp.tile(jnp.arange(seq), n_batch), jnp.tile(past_len + jnp.arange(n_tok), n_b)])
    cos, sin = _rope_tables(pos)
    rwt = router_w.T.astype(BF16)
    rbias = router_bias.astype(F32).reshape(N_EXPERTS, 1)
    dil_caches = (cache_dil0, cache_dil1, cache_dil2)

    def project(w_bf16, g):
        q, kvb, kvt = _qkv_rope(x, w_bf16, cos, sin, g, 1, 0, mp, seq)
        q_s, kv_s = _qkv_rope(x, w_bf16, cos, sin, g, 1, mp, ms, None)
        return q, kvb, kvt, q_s, kv_s

    def sample_local(q_s, kv_s, cache, dil, sink_col, want_lse):
        outs = _sample_attn(_block_diag_queries(q_s, n_b, n_tok), _position_minor(cache),
                            _new_kv_rows(kv_s, n_b, n_tok), dil, sink_col, want_lse)
        return _sample_out(outs[0], n_b, n_tok), (_sample_lse(outs[1], n_b, n_tok) if want_lse else None)

    swa_p, swa_s, moba_p, moba_s = [], [], [], []
    dil_p, dil_s = ([], [], []), ([], [], [])
    for i in range(DEPTH):
        kind, j = i % 3, i // 3
        gain1, bias1 = ln_gain[i, 0].reshape(1, d), ln_bias[i, 0].reshape(1, d)
        gain2, bias2 = ln_gain[i, 1].reshape(1, d), ln_bias[i, 1].reshape(1, d)
        if kind == 0:
            q, kvb, kvt, q_s, kv_s = project(w_qkv_swa[j].astype(BF16), 0)
            sinks = sinks_swa[j].astype(F32)
            sink_rows = jnp.repeat(sinks, Q_TILE).reshape(N_KV_HEADS, GQA * Q_TILE)
            (o,) = _local_attn_prompt(q, kvb, m, 0, 1, n_batch, seq, sink_rows, False)
            sink_col = jnp.tile(sinks, n_tok).reshape(n_tok * N_HEADS, 1)
            o_s, _ = sample_local(q_s[0], kv_s[0], cache_swa[j], 1, sink_col, False)
            o = lax.dynamic_update_slice(o, o_s.astype(BF16), (mp, 0))
            sp, ss = _window_state(kvt[0], kv_s[0], cache_swa[j], n_b, n_tok, SWA_WINDOW)
            swa_p.append(sp)
            swa_s.append(ss)
            x, x_rows = _oproj_ln(x, [o], None, _grouped_head_order(w_o_swa[j]).astype(BF16), gain1, bias1)
        elif kind == 1:
            w_dil = w_qkv_dil[j].astype(BF16)
            os_, lses = [], []
            for g, (win, dil) in enumerate(DIL_PAIRS):
                q, kvb, kvt, q_s, kv_s = project(w_dil, g)
                o, lse = _local_attn_prompt(q, kvb, m, 0, dil, n_batch, seq, None, True)
                o_s, lse_s = sample_local(q_s[0], kv_s[0], dil_caches[g][j], dil, None, True)
                os_.append(lax.dynamic_update_slice(o, o_s.astype(BF16), (mp, 0)))
                lses.append(lax.dynamic_update_slice(lse, lse_s, (mp, 0)))
                sp, ss = _window_state(kvt[0], kv_s[0], dil_caches[g][j], n_b, n_tok, win)
                dil_p[g].append(sp)
                dil_s[g].append(ss)
            x, x_rows = _oproj_ln(x, os_, lses, _grouped_head_order(w_o_dil[j]).astype(BF16), gain1, bias1)
        else:
            q, kvb, kvt, q_s, kv_s = project(w_qkv_moba[j].astype(BF16), 0)
            o = _moba_prompt(q[0], kvb[0], _block_means(kvt[0]), m, n_batch, seq)
            o_s = _moba_sample(_block_diag_queries(q_s[0], n_b, n_tok), _new_kv_rows(kv_s[0], n_b, n_tok),
                               _position_minor(cache_moba[j]), page_table)
            o = lax.dynamic_update_slice(o, _sample_out(o_s, n_b, n_tok).astype(BF16), (mp, 0))
            moba_p.append(_position_major(kvt[0]))
            moba_s.append(kv_s[0].reshape(n_b, n_tok, 2, N_KV_HEADS, HEAD_DIM))
            x, x_rows = _oproj_ln(x, [o], None, _grouped_head_order(w_o_moba[j]).astype(BF16), gain1, bias1)
        x = _moe_layer(x, x_rows, rwt, rbias, w_gate_e, w_up_e, w_down_e, i, gain2, bias2)
    y_prompt = x[:mp].reshape(n_batch, seq, d)
    y_sample = x[mp:].reshape(n_b, n_tok, d)
    return (y_prompt, y_sample, jnp.stack(swa_p), jnp.stack(swa_s),
            jnp.stack(dil_p[0]), jnp.stack(dil_s[0]), jnp.stack(dil_p[1]), jnp.stack(dil_s[1]),
            jnp.stack(dil_p[2]), jnp.stack(dil_s[2]), jnp.stack(moba_p), jnp.stack(moba_s))
```
